```python
import math
import jax
import jax.numpy as jnp
from jax import lax
import numpy as np

D_MODEL = 2048
BATCH = 8
SEQ = 2048
DEPTH = 1

GRID_W = 64
CTX_LEN = 256
NORM_EPS = 1e-6

RW_HEADS = 16
RW_HEAD = 64
RW_WIDTH = RW_HEADS * RW_HEAD
DECAY_RANK = 96
ICL_RANK = 96
GATE_RANK = 256
RW_GN_EPS = 64e-5

GD_HEADS = 8
GD_HEAD = 128
GD_WIDTH = GD_HEADS * GD_HEAD
SHORT_CONV = 3
CHUNK = 64

D_FF = 5632
FFN_CONV = 3

RW_SPLITS = (RW_WIDTH, 2 * RW_WIDTH, 3 * RW_WIDTH, 3 * RW_WIDTH + 2 * DECAY_RANK, 3 * RW_WIDTH + 2 * DECAY_RANK + 2 * ICL_RANK)
RW_COLS = 3 * RW_WIDTH + 2 * DECAY_RANK + 2 * ICL_RANK + GATE_RANK
GD_SPLITS = (3 * GD_WIDTH, 4 * GD_WIDTH, 4 * GD_WIDTH + 2 * GD_HEADS)
GD_COLS = 4 * GD_WIDTH + 4 * GD_HEADS
GATE_COLS = 2 * D_MODEL
IN_COLS = RW_COLS + GD_COLS + GATE_COLS

kernel_name = 'hybrid_rwkv7_gdn_convglu_prefix_dit'


def rms_norm(x, g, eps=NORM_EPS):
    xf = x.astype(jnp.float32)
    y = xf * lax.rsqrt(jnp.mean(xf * xf, axis=-1, keepdims=True) + eps)
    return (y * g.astype(jnp.float32)).astype(x.dtype)


def l2_normalize(x):
    xf = x.astype(jnp.float32)
    y = xf * lax.rsqrt(jnp.maximum(jnp.sum(xf * xf, axis=-1, keepdims=True), 1e-12))
    return y.astype(x.dtype)


def centred_neighbour_mean(p):
    pp = jnp.pad(p, ((0, 0), (1, 1), (0, 0)))
    return 0.5 * (pp[:, :-2] + pp[:, 2:])


def dwconv1d_centred(x, w):
    K = w.shape[0]
    half = K // 2
    L = x.shape[1]
    xp = jnp.pad(x, ((0, 0), (half, half), (0, 0)))
    acc = xp[:, 0:L] * w[0]
    for i in range(1, K):
        acc = acc + xp[:, i:i + L] * w[i]
    return acc


def dwconv2d_same(x, w):
    C = x.shape[-1]
    return lax.conv_general_dilated(x, w[:, :, None, :], window_strides=(1, 1), padding='SAME',
                                    dimension_numbers=('NHWC', 'HWIO', 'NHWC'), feature_group_count=C)


def rwkv7_prepare(p, mu, k_k, k_a, w0, w_up, a0, a_up, g_up):
    B, L, _ = p.shape
    p = p + mu * (centred_neighbour_mean(p) - p)
    r, k, v, wd, ad, gd = jnp.split(p, RW_SPLITS, axis=-1)
    wd = wd.reshape(B, L, 2, DECAY_RANK)
    ad = ad.reshape(B, L, 2, ICL_RANK)
    wlog = -jax.nn.softplus(-(w0 + jnp.einsum('bldr,drc->bldc', jnp.tanh(wd), w_up))) - 0.5
    decay = jnp.exp(-jnp.exp(wlog.astype(jnp.float32)))
    a = jax.nn.sigmoid(a0 + jnp.einsum('bldr,drc->bldc', ad, a_up))
    g = jnp.dot(jax.nn.sigmoid(gd), g_up)
    heads = lambda t: t.reshape(t.shape[:-1] + (RW_HEADS, RW_HEAD))
    kk = l2_normalize(heads(k * k_k))
    k_dir = k[:, :, None, :] * (1 + (a - 1) * k_a)
    return heads(r), heads(k_dir), heads(v), kk, heads(decay), heads(a), g


def wkv7_args(feats, d):
    r, k_dir, v, kk, decay, a, _ = feats
    return r, decay[:, :, d], k_dir[:, :, d], v, -kk, kk * a[:, :, d]


def wkv7_scan(S0, r, w, k, v, kneg, kka, reverse):
    def step(S, inp):
        r_t, w_t, k_t, v_t, kneg_t, kka_t = inp
        sa = jnp.einsum('bhvk,bhk->bhv', S, kneg_t)
        S = S * w_t[:, :, None, :] + sa[..., None] * kka_t[:, :, None, :] + v_t[..., None] * k_t[:, :, None, :]
        return S, jnp.einsum('bhvk,bhk->bhv', S, r_t)
    xs = tuple(jnp.moveaxis(t.astype(jnp.float32), 1, 0) for t in (r, w, k, v, kneg, kka))
    S, o = lax.scan(step, S0, xs, reverse=reverse)
    return S, jnp.moveaxis(o, 0, 1)


def rwkv7_readout(o, feats, r_k, gn_g, gn_b):
    r, k_dir, v, _, _, _, g = feats
    B, L = r.shape[:2]
    mean = jnp.mean(o, axis=-1, keepdims=True)
    var = jnp.mean(jnp.square(o - mean), axis=-1, keepdims=True)
    on = ((o - mean) * lax.rsqrt(var + RW_GN_EPS)).reshape(B, L, RW_WIDTH)
    on = (on * gn_g + gn_b).astype(g.dtype)
    k_sum = k_dir[:, :, 0] + k_dir[:, :, 1]
    bonus = jnp.sum(r * k_sum * r_k.reshape(RW_HEADS, RW_HEAD), axis=-1, keepdims=True) * v
    return (on + bonus.reshape(B, L, RW_WIDTH)) * g


def rwkv7_mixer(p_ctx, p_lat, mu, k_k, k_a, r_k, w0, w_up, a0, a_up, g_up, gn_g, gn_b):
    f_ctx = rwkv7_prepare(p_ctx, mu, k_k, k_a, w0, w_up, a0, a_up, g_up)
    f_lat = rwkv7_prepare(p_lat, mu, k_k, k_a, w0, w_up, a0, a_up, g_up)
    B = p_lat.shape[0]
    outs_ctx, outs_lat = [], []
    for d in range(2):
        S0 = jnp.zeros((B, RW_HEADS, RW_HEAD, RW_HEAD), jnp.float32)
        S_ctx, o_c = wkv7_scan(S0, *wkv7_args(f_ctx, d), reverse=(d == 1))
        _, o_l = wkv7_scan(S_ctx, *wkv7_args(f_lat, d), reverse=(d == 1))
        outs_ctx.append(o_c)
        outs_lat.append(o_l)
    y_ctx = rwkv7_readout(outs_ctx[0] + outs_ctx[1], f_ctx, r_k, gn_g, gn_b)
    y_lat = rwkv7_readout(outs_lat[0] + outs_lat[1], f_lat, r_k, gn_g, gn_b)
    return y_ctx, y_lat


def gdn_prepare(p, conv_w, a_log, dt_bias):
    B, L, _ = p.shape
    qkv, z, a, b = jnp.split(p, GD_SPLITS, axis=-1)
    qkv = jax.nn.silu(dwconv1d_centred(qkv, conv_w))
    q, k, v = [t.reshape(B, L, GD_HEADS, GD_HEAD) for t in jnp.split(qkv, 3, axis=-1)]
    q = l2_normalize(q) * (GD_HEAD ** -0.5)
    k = l2_normalize(k)
    glog = -jnp.exp(a_log) * jax.nn.softplus(a.reshape(B, L, 2, GD_HEADS) + dt_bias)
    beta = jax.nn.sigmoid(b.reshape(B, L, 2, GD_HEADS))
    return q, k, v, z, glog, beta


def gated_delta_chunked(q, k, v, g, beta, S0, reverse):
    if reverse:
        q, k, v, g, beta = (t[:, ::-1] for t in (q, k, v, g, beta))
    out_dtype = v.dtype
    q, k, v, g, beta = (t.astype(jnp.float32) for t in (q, k, v, g, beta))
    B, L, H, dk = q.shape
    dv = v.shape[-1]
    n = L // CHUNK
    chunks = lambda t: t.reshape(B, n, CHUNK, H, -1).transpose(0, 3, 1, 2, 4)
    qc, kc, vc = chunks(q), chunks(k), chunks(v)
    gc = g.reshape(B, n, CHUNK, H).transpose(0, 3, 1, 2)
    bc = beta.reshape(B, n, CHUNK, H).transpose(0, 3, 1, 2)
    G = jnp.cumsum(gc, axis=-1)
    incl = jnp.tril(jnp.ones((CHUNK, CHUNK), bool))
    strict = jnp.tril(jnp.ones((CHUNK, CHUNK), bool), -1)
    diff = G[..., :, None] - G[..., None, :]
    decay = jnp.where(incl, jnp.exp(jnp.where(incl, diff, 0.0)), 0.0)
    kb = kc * bc[..., None]
    A = jnp.where(strict, jnp.einsum('bhnid,bhnjd->bhnij', kb, kc) * decay, 0.0)
    eye = jnp.eye(CHUNK, dtype=jnp.float32)
    rhs = jnp.concatenate([vc * bc[..., None], kb * jnp.exp(G)[..., None]], axis=-1)
    sol = lax.linalg.triangular_solve(eye + A, rhs, left_side=True, lower=True, unit_diagonal=True)
    u, w = sol[..., :dv], sol[..., dv:]
    attn = jnp.where(incl, jnp.einsum('bhnid,bhnjd->bhnij', qc, kc) * decay, 0.0)

    def step(S, inp):
        q_i, k_i, u_i, w_i, G_i, attn_i = inp
        v_new = u_i - jnp.einsum('bhck,bhkv->bhcv', w_i, S)
        o = jnp.einsum('bhck,bhkv->bhcv', q_i * jnp.exp(G_i)[..., None], S) + jnp.einsum('bhij,bhjv->bhiv', attn_i, v_new)
        G_last = G_i[..., -1]
        k_dec = k_i * jnp.exp(G_last[..., None] - G_i)[..., None]
        S = S * jnp.exp(G_last)[..., None, None] + jnp.einsum('bhck,bhcv->bhkv', k_dec, v_new)
        return S, o

    xs = tuple(jnp.moveaxis(t, 2, 0) for t in (qc, kc, u, w, G, attn))
    S, o = lax.scan(step, S0, xs)
    o = o.transpose(1, 0, 3, 2, 4).reshape(B, L, H, dv).astype(out_dtype)
    if reverse:
        o = o[:, ::-1]
    return S, o


def gdn_readout(o, z, norm_g):
    B, L = z.shape[:2]
    zh = z.reshape(B, L, GD_HEADS, GD_HEAD)
    return (rms_norm(o, norm_g).astype(z.dtype) * jax.nn.silu(zh)).reshape(B, L, GD_WIDTH)


def gdn_mixer(p_ctx, p_lat, conv_w, a_log, dt_bias, norm_g):
    f_ctx = gdn_prepare(p_ctx, conv_w, a_log, dt_bias)
    f_lat = gdn_prepare(p_lat, conv_w, a_log, dt_bias)
    B = p_lat.shape[0]
    outs_ctx, outs_lat = [], []
    for d in range(2):
        S0 = jnp.zeros((B, GD_HEADS, GD_HEAD, GD_HEAD), jnp.float32)
        q, k, v, _, g, beta = f_ctx
        S_ctx, o_c = gated_delta_chunked(q, k, v, g[:, :, d], beta[:, :, d], S0, d == 1)
        q, k, v, _, g, beta = f_lat
        _, o_l = gated_delta_chunked(q, k, v, g[:, :, d], beta[:, :, d], S_ctx, d == 1)
        outs_ctx.append(o_c)
        outs_lat.append(o_l)
    y_ctx = gdn_readout(outs_ctx[0] + outs_ctx[1], f_ctx[3], norm_g)
    y_lat = gdn_readout(outs_lat[0] + outs_lat[1], f_lat[3], norm_g)
    return y_ctx, y_lat


def branch_merge(y_a, y_b, gates, w_a_out, w_b_out, w_o):
    g_a, g_b = jnp.split(gates, 2, axis=-1)
    merged = jax.nn.sigmoid(g_a) * jnp.dot(y_a, w_a_out) + jax.nn.sigmoid(g_b) * jnp.dot(y_b, w_b_out)
    return jnp.dot(merged, w_o)


def conv_ffn(h, n_rows, w1, conv_w, w2):
    B, L, _ = h.shape
    gate, val = jnp.split(jnp.dot(h, w1), 2, axis=-1)
    gate = dwconv2d_same(gate.reshape(B, n_rows, L // n_rows, D_FF), conv_w).reshape(B, L, D_FF)
    return jnp.dot(jax.nn.gelu(gate, approximate=False) * val, w2)


def setup_inputs(seed: int = 0) -> dict:
    key = jax.random.key(seed)
    ks = jax.random.split(key, 32)
    f32 = jnp.float32
    nrm = lambda k, shape, s: jax.random.normal(k, shape, f32) * s
    gain = lambda k, shape: 1.0 + 0.02 * jax.random.normal(k, shape, f32)
    L = DEPTH
    dt = jnp.exp(jax.random.uniform(ks[22], (L, 2, GD_HEADS), f32, math.log(1e-3), math.log(1e-1)))
    return {
        'x': nrm(ks[0], (BATCH, SEQ, D_MODEL), 1.0),
        'c': nrm(ks[1], (BATCH, D_MODEL), 1.0),
        'ctx': nrm(ks[2], (BATCH, CTX_LEN, D_MODEL), 1.0),
        'c_ctx': nrm(ks[3], (D_MODEL,), 1.0),
        'w_ada': nrm(ks[4], (L, D_MODEL, 6 * D_MODEL), D_MODEL ** -0.5),
        'b_ada': nrm(ks[5], (L, 6 * D_MODEL), 0.01),
        'norm1_g': gain(ks[6], (L, D_MODEL)),
        'norm2_g': gain(ks[7], (L, D_MODEL)),
        'w_in': nrm(ks[8], (L, D_MODEL, IN_COLS), D_MODEL ** -0.5),
        'rw_mu': jax.random.uniform(ks[9], (L, RW_COLS), f32),
        'rw_k_k': 0.85 + 0.02 * jax.random.normal(ks[10], (L, RW_WIDTH), f32),
        'rw_k_a': gain(ks[11], (L, RW_WIDTH)),
        'rw_r_k': nrm(ks[12], (L, RW_WIDTH), 0.1),
        'rw_w0': jax.random.uniform(ks[13], (L, 2, RW_WIDTH), f32, -6.5, -1.5),
        'rw_w_up': nrm(ks[14], (L, 2, DECAY_RANK, RW_WIDTH), 0.1 * DECAY_RANK ** -0.5),
        'rw_a0': nrm(ks[15], (L, 2, RW_WIDTH), 0.1),
        'rw_a_up': nrm(ks[16], (L, 2, ICL_RANK, RW_WIDTH), 0.3 * ICL_RANK ** -0.5),
        'rw_g_up': nrm(ks[17], (L, GATE_RANK, RW_WIDTH), GATE_RANK ** -0.5),
        'rw_gn_g': gain(ks[18], (L, RW_WIDTH)),
        'rw_gn_b': nrm(ks[19], (L, RW_WIDTH), 0.01),
        'gd_conv_w': nrm(ks[20], (L, SHORT_CONV, 3 * GD_WIDTH), SHORT_CONV ** -0.5),
        'gd_a_log': jnp.log(jax.random.uniform(ks[21], (L, 2, GD_HEADS), f32, 1.0, 16.0)),
        'gd_dt_bias': dt + jnp.log(-jnp.expm1(-dt)),
        'gd_norm_g': gain(ks[23], (L, GD_HEAD)),
        'w_a_out': nrm(ks[24], (L, RW_WIDTH, D_MODEL), RW_WIDTH ** -0.5),
        'w_b_out': nrm(ks[25], (L, GD_WIDTH, D_MODEL), GD_WIDTH ** -0.5),
        'w_o': nrm(ks[26], (L, D_MODEL, D_MODEL), D_MODEL ** -0.5),
        'ffn_w1': nrm(ks[27], (L, D_MODEL, 2 * D_FF), D_MODEL ** -0.5),
        'ffn_conv_w': nrm(ks[28], (L, FFN_CONV, FFN_CONV, D_FF), 1.0 / FFN_CONV),
        'ffn_w2': nrm(ks[29], (L, D_FF, D_MODEL), D_FF ** -0.5),
        'final_norm_g': gain(ks[30], (D_MODEL,)),
    }


def reference(x, c, ctx, c_ctx, w_ada, b_ada, norm1_g, norm2_g, w_in, rw_mu, rw_k_k, rw_k_a, rw_r_k,
              rw_w0, rw_w_up, rw_a0, rw_a_up, rw_g_up, rw_gn_g, rw_gn_b, gd_conv_w, gd_a_log, gd_dt_bias,
              gd_norm_g, w_a_out, w_b_out, w_o, ffn_w1, ffn_conv_w, ffn_w2, final_norm_g):
    n_rows = x.shape[1] // GRID_W
    for l in range(DEPTH):
        mod_lat = jnp.dot(jax.nn.silu(c), w_ada[l]) + b_ada[l]
        mod_ctx = jnp.dot(jax.nn.silu(c_ctx), w_ada[l]) + b_ada[l]
        sh1, sc1, g1, sh2, sc2, g2 = jnp.split(mod_lat[:, None, :], 6, axis=-1)
        csh1, csc1, cg1, csh2, csc2, cg2 = jnp.split(mod_ctx[None, None, :], 6, axis=-1)

        h_lat = rms_norm(x, norm1_g[l]) * (1 + sc1) + sh1
        h_ctx = rms_norm(ctx, norm1_g[l]) * (1 + csc1) + csh1
        p_lat = jnp.dot(h_lat, w_in[l])
        p_ctx = jnp.dot(h_ctx, w_in[l])
        rw_lat, gd_lat, gate_lat = jnp.split(p_lat, (RW_COLS, RW_COLS + GD_COLS), axis=-1)
        rw_ctx, gd_ctx, gate_ctx = jnp.split(p_ctx, (RW_COLS, RW_COLS + GD_COLS), axis=-1)
        ya_ctx, ya_lat = rwkv7_mixer(rw_ctx, rw_lat, rw_mu[l], rw_k_k[l], rw_k_a[l], rw_r_k[l], rw_w0[l],
                                     rw_w_up[l], rw_a0[l], rw_a_up[l], rw_g_up[l], rw_gn_g[l], rw_gn_b[l])
        yb_ctx, yb_lat = gdn_mixer(gd_ctx, gd_lat, gd_conv_w[l], gd_a_log[l], gd_dt_bias[l], gd_norm_g[l])
        x = x + g1 * branch_merge(ya_lat, yb_lat, gate_lat, w_a_out[l], w_b_out[l], w_o[l])

        h_lat = rms_norm(x, norm2_g[l]) * (1 + sc2) + sh2
        x = x + g2 * conv_ffn(h_lat, n_rows, ffn_w1[l], ffn_conv_w[l], ffn_w2[l])

        if l < DEPTH - 1:
            ctx = ctx + cg1 * branch_merge(ya_ctx, yb_ctx, gate_ctx, w_a_out[l], w_b_out[l], w_o[l])
            h_ctx = rms_norm(ctx, norm2_g[l]) * (1 + csc2) + csh2
            ctx = ctx + cg2 * conv_ffn(h_ctx, 1, ffn_w1[l], ffn_conv_w[l], ffn_w2[l])
    return rms_norm(x, final_norm_g)
```

```python
import functools

import jax
import jax.numpy as jnp
from jax import lax
from jax.experimental import pallas as pl
from jax.experimental.pallas import tpu as pltpu

F32 = jnp.float32
BF16 = jnp.bfloat16
HIGHEST = lax.Precision.HIGHEST

NORM_EPS = 1e-6
RW_GN_EPS = 64e-5
RW_HEAD = 64
GD_HEAD = 128
LANES = 128
CHUNK = 64
GRID_W = 64
RANK_PAD = 128
VMEM_LIMIT = 56 * 1024 * 1024


def _params(sem):
    return pltpu.CompilerParams(dimension_semantics=sem, vmem_limit_bytes=VMEM_LIMIT)


def _tile(n, pref, mult=8):
    if n <= pref:
        return n
    t = (pref // mult) * mult
    while t >= mult:
        if n % t == 0:
            return t
        t -= mult
    return n


def _mm(a, b):
    return jnp.dot(a.astype(BF16), b.astype(BF16), preferred_element_type=F32)


def _mm_nt(a, b):
    return lax.dot_general(a.astype(BF16), b.astype(BF16), (((1,), (1,)), ((), ())),
                           preferred_element_type=F32)


def _mm_tn(a, b):
    return lax.dot_general(a.astype(BF16), b.astype(BF16), (((0,), (0,)), ((), ())),
                           preferred_element_type=F32)


def _mm_hi(a, b):
    return jnp.dot(a, b, precision=HIGHEST, preferred_element_type=F32)


def _softplus(x):
    return jnp.maximum(x, 0.0) + jnp.log(1.0 + jnp.exp(-jnp.abs(x)))


def _sigmoid(x):
    return 1.0 / (1.0 + jnp.exp(-x))


def _seg_ones(width):
    i = lax.broadcasted_iota(jnp.int32, (LANES, LANES), 0) // width
    j = lax.broadcasted_iota(jnp.int32, (LANES, LANES), 1) // width
    return (i == j).astype(F32)


def _seg_sum(x, width):
    e = _seg_ones(width)
    n = x.shape[-1] // LANES
    parts = [_mm_hi(x[:, g * LANES:(g + 1) * LANES], e) for g in range(n)]
    return parts[0] if n == 1 else jnp.concatenate(parts, axis=-1)


def _tri(rev):
    i = lax.broadcasted_iota(jnp.int32, (CHUNK, CHUNK), 0)
    j = lax.broadcasted_iota(jnp.int32, (CHUNK, CHUNK), 1)
    return ((j >= i) if rev else (j <= i)).astype(F32)


def _level_masks(i, j):
    masks = [(i // 2) == (j // 2)]
    s = 2
    while s < CHUNK:
        masks.append(((i // (2 * s)) == (j // (2 * s))) & ((i // s) != (j // s)))
        s *= 2
    return masks


def _shift_rows(x, prev_row, next_row):
    n = x.shape[0]
    row = lax.broadcasted_iota(jnp.int32, x.shape, 0)
    xm1 = jnp.where(row == 0, prev_row, pltpu.roll(x, 1, 0))
    xp1 = jnp.where(row == n - 1, next_row, pltpu.roll(x, n - 1, 0))
    return xm1, xp1


def _mod_kernel(c_ref, w_ref, b_ref, o_ref):
    c = c_ref[...]
    s = c * _sigmoid(c)
    o_ref[...] = _mm_hi(s, w_ref[...]) + b_ref[...]


def _mod(cc, w_ada, b_ada):
    rows, d = cc.shape
    n = w_ada.shape[1]
    tn = _tile(n, 1024, LANES)
    return pl.pallas_call(
        _mod_kernel,
        grid=(n // tn,),
        in_specs=[pl.BlockSpec((rows, d), lambda j: (0, 0)),
                  pl.BlockSpec((d, tn), lambda j: (0, j)),
                  pl.BlockSpec((1, tn), lambda j: (0, j))],
        out_specs=pl.BlockSpec((rows, tn), lambda j: (0, j)),
        out_shape=jax.ShapeDtypeStruct((rows, n), F32),
        compiler_params=_params(("arbitrary",)),
        name="mod",
    )(cc, w_ada, b_ada)


def _normproj_kernel(x_ref, mod_ref, g_ref, w_ref, o_ref, h_ref, *, sh_row, sc_row):
    @pl.when(pl.program_id(1) == 0)
    def _():
        x = x_ref[...]
        ms = jnp.mean(x * x, axis=-1, keepdims=True)
        y = x * lax.rsqrt(ms + NORM_EPS) * g_ref[...]
        sh = mod_ref[0, sh_row:sh_row + 1, :]
        sc = mod_ref[0, sc_row:sc_row + 1, :]
        h_ref[...] = (y * (1.0 + sc) + sh).astype(BF16)

    o_ref[...] = jnp.dot(h_ref[...], w_ref[...], preferred_element_type=F32)


def _normproj(x2, mod, gain, w, rows_per_mod, n_cols, sh_row, sc_row, name):
    m, d = x2.shape
    tm = _tile(rows_per_mod, 1024)
    tn = _tile(n_cols, 1024, LANES)
    per = rows_per_mod // tm
    return pl.pallas_call(
        functools.partial(_normproj_kernel, sh_row=sh_row, sc_row=sc_row),
        grid=(m // tm, n_cols // tn),
        in_specs=[pl.BlockSpec((tm, d), lambda i, j: (i, 0)),
                  pl.BlockSpec((1, 8, d), lambda i, j: (i // per, 0, 0)),
                  pl.BlockSpec((1, d), lambda i, j: (0, 0)),
                  pl.BlockSpec((d, tn), lambda i, j: (0, j))],
        out_specs=pl.BlockSpec((tm, tn), lambda i, j: (i, j)),
        out_shape=jax.ShapeDtypeStruct((m, n_cols), F32),
        scratch_shapes=[pltpu.VMEM((tm, d), BF16)],
        compiler_params=_params(("parallel", "arbitrary")),
        name=name,
    )(x2, mod, gain, w)


def _rwprep_kernel(pc_ref, pp_ref, pn_ref, mu_ref, kk_ref, ka_ref, rk_ref, w0_ref, wup_ref,
                   a0_ref, aup_ref, gup_ref,
                   at_ref, bt_ref, kt_ref, rt_ref, v_ref, pt_ref, bonus_ref, g_ref, *, tiles_per_seq):
    i = pl.program_id(0)
    tm = pc_ref.shape[0]
    w = RW_HEAD * (kk_ref.shape[1] // RW_HEAD)
    first = (i % tiles_per_seq) == 0
    last = (i % tiles_per_seq) == tiles_per_seq - 1
    x = pc_ref[...]
    prev_row = jnp.where(first, 0.0, pp_ref[7:8, :])
    next_row = jnp.where(last, 0.0, pn_ref[0:1, :])
    xm1, xp1 = _shift_rows(x, prev_row, next_row)
    xs = x + mu_ref[...] * (0.5 * (xm1 + xp1) - x)

    r = xs[:, 0:w]
    k = xs[:, w:2 * w]
    v = xs[:, 2 * w:3 * w]
    base = 3 * w
    gd = xs[:, base + 4 * RANK_PAD: base + 4 * RANK_PAD + gup_ref.shape[0]]
    g_ref[...] = _mm(_sigmoid(gd), gup_ref[...])
    v_ref[...] = v.astype(BF16)

    kx = k * kk_ref[...]
    kk = kx * lax.rsqrt(jnp.maximum(_seg_sum(kx * kx, RW_HEAD), 1e-12))

    ksum = jnp.zeros_like(k)
    for d in range(2):
        wd = xs[:, base + d * RANK_PAD: base + (d + 1) * RANK_PAD]
        ad = xs[:, base + (2 + d) * RANK_PAD: base + (3 + d) * RANK_PAD]
        wl = w0_ref[d:d + 1, :] + _mm(jnp.tanh(wd), wup_ref[d])
        lw = -jnp.exp(-_softplus(-wl) - 0.5)
        a = _sigmoid(a0_ref[d:d + 1, :] + _mm(ad, aup_ref[d]))
        kd = k * (1.0 + (a - 1.0) * ka_ref[...])
        ksum = ksum + kd
        tri = _tri(rev=(d == 1))
        for c in range(tm // CHUNK):
            rows = slice(c * CHUNK, (c + 1) * CHUNK)
            lwc = lw[rows]
            cum = _mm_hi(tri, lwc)
            p_in = jnp.exp(cum)
            p_inv = jnp.exp(-cum)
            p_ex = jnp.exp(cum - lwc)
            at_ref[d, rows, :] = (-kk[rows] * p_ex).astype(BF16)
            bt_ref[d, rows, :] = (kk[rows] * a[rows] * p_inv).astype(BF16)
            kt_ref[d, rows, :] = (kd[rows] * p_inv).astype(BF16)
            rt_ref[d, rows, :] = (r[rows] * p_in).astype(BF16)
            tot = cum[CHUNK - 1:CHUNK] if d == 0 else cum[0:1]
            pt_ref[d, c, :, :] = jnp.exp(tot)
    bonus_ref[...] = _seg_sum(r * ksum * rk_ref[...], RW_HEAD) * v


def _rwprep(p, seq_len, wts):
    m = p.shape[0]
    mu, k_k, k_a, r_k, w0, w_up, a0, a_up, g_up = wts
    w = k_k.shape[1]
    blk = mu.shape[1]
    tm = _tile(seq_len, 256, CHUNK)
    tps = seq_len // tm
    nb8 = m // 8
    full = lambda a: pl.BlockSpec(a.shape, lambda i: (0,) * a.ndim)
    feat = jax.ShapeDtypeStruct((2, m, w), BF16)
    feat_spec = pl.BlockSpec((2, tm, w), lambda i: (0, i, 0))
    row_spec = pl.BlockSpec((tm, w), lambda i: (i, 0))
    return pl.pallas_call(
        functools.partial(_rwprep_kernel, tiles_per_seq=tps),
        grid=(m // tm,),
        in_specs=[pl.BlockSpec((tm, blk), lambda i: (i, 0)),
                  pl.BlockSpec((8, blk), lambda i: (jnp.maximum(i * (tm // 8) - 1, 0), 0)),
                  pl.BlockSpec((8, blk), lambda i: (jnp.minimum((i + 1) * (tm // 8), nb8 - 1), 0)),
                  full(mu), full(k_k), full(k_a), full(r_k), full(w0), full(w_up), full(a0),
                  full(a_up), full(g_up)],
        out_specs=[feat_spec, feat_spec, feat_spec, feat_spec, row_spec,
                   pl.BlockSpec((2, tm // CHUNK, 1, w), lambda i: (0, i, 0, 0)),
                   row_spec, row_spec],
        out_shape=[feat, feat, feat, feat, jax.ShapeDtypeStruct((m, w), BF16),
                   jax.ShapeDtypeStruct((2, m // CHUNK, 1, w), F32),
                   jax.ShapeDtypeStruct((m, w), F32), jax.ShapeDtypeStruct((m, w), F32)],
        compiler_params=_params(("parallel",)),
        name="rwprep",
    )(p, p, p, mu, k_k, k_a, r_k, w0, w_up, a0, a_up, g_up)


def _pair_blockdiag(x, m0):
    zero = jnp.zeros_like(x)
    return jnp.concatenate([jnp.where(m0, x, zero), jnp.where(m0, zero, x)], axis=0)


def _rwscan_kernel(at_ref, bt_ref, kt_ref, rt_ref, v_ref, pt_ref, s0_ref, *refs, want_out):
    if want_out:
        o_ref, s_ref, h_ref = refs
    else:
        s_ref, h_ref = refs
    d = pl.program_id(1)
    c = pl.program_id(2)
    n_pairs = h_ref.shape[0]

    @pl.when(c == 0)
    def _():
        h_ref[...] = s0_ref[0, 0]

    sgn = 1 - 2 * d
    i = lax.broadcasted_iota(jnp.int32, (CHUNK, LANES), 0)
    lane = lax.broadcasted_iota(jnp.int32, (CHUNK, LANES), 1)
    rel = ((lane % RW_HEAD) - i) * sgn
    strict = rel < 0
    incl = rel <= 0
    eye = (rel == 0).astype(F32)
    blk = _level_masks(i, lane % RW_HEAD)
    m0 = lane < RW_HEAD
    r2 = lax.broadcasted_iota(jnp.int32, (LANES, LANES), 0) // RW_HEAD
    c2 = lax.broadcasted_iota(jnp.int32, (LANES, LANES), 1) // RW_HEAD
    diag2 = r2 == c2

    for p in range(n_pairs):
        cols = slice(p * LANES, (p + 1) * LANES)
        at = at_ref[0, :, cols]
        bt = bt_ref[0, :, cols]
        kt = kt_ref[0, :, cols]
        rt = rt_ref[0, :, cols]
        v = v_ref[:, cols]
        zero = jnp.zeros_like(bt)
        bk = jnp.concatenate([jnp.where(m0, bt, zero), jnp.where(m0, zero, bt),
                              jnp.where(m0, kt, zero), jnp.where(m0, zero, kt)], axis=0)
        s4 = _mm_nt(jnp.concatenate([at, rt], axis=0), bk)
        a_ab = jnp.where(strict, s4[:CHUNK, :LANES], 0.0)
        a_ak = jnp.where(strict, s4[:CHUNK, LANES:], 0.0)
        t = eye + jnp.where(blk[0], a_ab, 0.0)
        for lvl in range(1, len(blk)):
            off = jnp.where(blk[lvl], a_ab, 0.0)
            t = t + _mm(_mm(t, _pair_blockdiag(off, m0)), _pair_blockdiag(t, m0))
        wt = _mm(t, _pair_blockdiag(at, m0))
        u0 = _mm(t, _pair_blockdiag(_mm(a_ak, _pair_blockdiag(v, m0)), m0))
        ht = h_ref[p]
        u = _mm_nt(wt, ht) + u0
        if want_out:
            m_rb = jnp.where(incl, s4[CHUNK:, :LANES], 0.0)
            m_rk = jnp.where(incl, s4[CHUNK:, LANES:], 0.0)
            o = (_mm_nt(rt, ht) + _mm(m_rb, _pair_blockdiag(u, m0))
                 + _mm(m_rk, _pair_blockdiag(v, m0)))
            o_ref[0, :, cols] = o
        upd = _mm_tn(jnp.concatenate([u.astype(BF16), v], axis=0),
                     jnp.concatenate([bt, kt], axis=0))
        h_ref[p] = (ht + jnp.where(diag2, upd, 0.0)) * pt_ref[0, 0, :, cols]

    @pl.when(c == pl.num_programs(2) - 1)
    def _():
        s_ref[0, 0] = h_ref[...]


def _rwscan(at, bt, kt, rt, v, pt, s0, batch, want_out):
    m, w = v.shape
    n_chunks = m // batch // CHUNK
    n_pairs = w // LANES

    def row(b, d, c):
        return b * n_chunks + c + d * (n_chunks - 1 - 2 * c)

    feat_spec = pl.BlockSpec((1, CHUNK, w), lambda b, d, c: (d, row(b, d, c), 0))
    state_spec = pl.BlockSpec((1, 1, n_pairs, LANES, LANES), lambda b, d, c: (b, d, 0, 0, 0))
    out_specs = [state_spec]
    out_shape = [jax.ShapeDtypeStruct(s0.shape, F32)]
    if want_out:
        out_specs = [feat_spec] + out_specs
        out_shape = [jax.ShapeDtypeStruct((2, m, w), F32)] + out_shape
    return pl.pallas_call(
        functools.partial(_rwscan_kernel, want_out=want_out),
        grid=(batch, 2, n_chunks),
        in_specs=[feat_spec, feat_spec, feat_spec, feat_spec,
                  pl.BlockSpec((CHUNK, w), lambda b, d, c: (row(b, d, c), 0)),
                  pl.BlockSpec((1, 1, 1, w), lambda b, d, c: (d, row(b, d, c), 0, 0)),
                  state_spec],
        out_specs=out_specs,
        out_shape=out_shape,
        scratch_shapes=[pltpu.VMEM((n_pairs, LANES, LANES), F32)],
        compiler_params=_params(("parallel", "arbitrary", "arbitrary")),
        name="rwscan_out" if want_out else "rwscan_state",
    )(at, bt, kt, rt, v, pt, s0)


def _rwread_kernel(of_ref, ob_ref, bonus_ref, g_ref, gng_ref, gnb_ref, y_ref):
    o = of_ref[0] + ob_ref[0]
    inv = 1.0 / RW_HEAD
    mean = _seg_sum(o, RW_HEAD) * inv
    cen = o - mean
    var = _seg_sum(cen * cen, RW_HEAD) * inv
    on = cen * lax.rsqrt(var + RW_GN_EPS) * gng_ref[...] + gnb_ref[...]
    y_ref[...] = ((on + bonus_ref[...]) * g_ref[...]).astype(BF16)


def _rwread(o, bonus, g, gn_g, gn_b):
    _, m, w = o.shape
    tm = _tile(m, 512)
    row_spec = pl.BlockSpec((tm, w), lambda i: (i, 0))
    vec_spec = pl.BlockSpec((1, w), lambda i: (0, 0))
    return pl.pallas_call(
        _rwread_kernel,
        grid=(m // tm,),
        in_specs=[pl.BlockSpec((1, tm, w), lambda i: (0, i, 0)),
                  pl.BlockSpec((1, tm, w), lambda i: (1, i, 0)),
                  row_spec, row_spec, vec_spec, vec_spec],
        out_specs=row_spec,
        out_shape=jax.ShapeDtypeStruct((m, w), BF16),
        compiler_params=_params(("parallel",)),
        name="rwread",
    )(o, o, bonus, g, gn_g, gn_b)


def _gdprep_kernel(pc_ref, pp_ref, pn_ref, ab_ref, cw_ref, alog_ref, dtb_ref,
                   q_ref, k_ref, v_ref, gcum_ref, beta_ref, *, tiles_per_seq):
    i = pl.program_id(0)
    tm = pc_ref.shape[0]
    w = q_ref.shape[1]
    n_heads = w // GD_HEAD
    first = (i % tiles_per_seq) == 0
    last = (i % tiles_per_seq) == tiles_per_seq - 1
    x = pc_ref[:, 0:3 * w]
    prev_row = jnp.where(first, 0.0, pp_ref[7:8, 0:3 * w])
    next_row = jnp.where(last, 0.0, pn_ref[0:1, 0:3 * w])
    xm1, xp1 = _shift_rows(x, prev_row, next_row)
    y = xm1 * cw_ref[0:1, :] + x * cw_ref[1:2, :] + xp1 * cw_ref[2:3, :]
    y = y * _sigmoid(y)
    for h in range(n_heads):
        for part, ref, scale in ((0, q_ref, GD_HEAD ** -0.5), (1, k_ref, 1.0)):
            cols = slice(part * w + h * GD_HEAD, part * w + (h + 1) * GD_HEAD)
            t = y[:, cols]
            ss = jnp.sum(t * t, axis=-1, keepdims=True)
            ref[:, h * GD_HEAD:(h + 1) * GD_HEAD] = (
                t * (lax.rsqrt(jnp.maximum(ss, 1e-12)) * scale)).astype(BF16)
    v_ref[...] = y[:, 2 * w:3 * w].astype(BF16)

    ab = ab_ref[:, 6 * LANES:7 * LANES]
    a = ab[:, 0:2 * n_heads]
    b = ab[:, 2 * n_heads:4 * n_heads]
    glog = -jnp.exp(alog_ref[...]) * _softplus(a + dtb_ref[...])
    beta_ref[...] = _sigmoid(b)
    for c in range(tm // CHUNK):
        rows = slice(c * CHUNK, (c + 1) * CHUNK)
        gc = glog[rows]
        fwd = _mm_hi(_tri(False), gc)
        bwd = _mm_hi(_tri(True), gc)
        col = lax.broadcasted_iota(jnp.int32, gc.shape, 1)
        gcum_ref[rows, :] = jnp.where(col < n_heads, fwd, bwd)


def _gdprep(p, seq_len, conv_w, a_log, dt_bias, qkvz_block, ab_block, width):
    m = p.shape[0]
    blk = 4 * width
    tm = _tile(seq_len, 256, CHUNK)
    tps = seq_len // tm
    nb8 = m // 8
    n2h = a_log.shape[1]
    full = lambda a: pl.BlockSpec(a.shape, lambda i: (0,) * a.ndim)
    row_spec = pl.BlockSpec((tm, width), lambda i: (i, 0))
    small_spec = pl.BlockSpec((tm, n2h), lambda i: (i, 0))
    return pl.pallas_call(
        functools.partial(_gdprep_kernel, tiles_per_seq=tps),
        grid=(m // tm,),
        in_specs=[pl.BlockSpec((tm, blk), lambda i: (i, qkvz_block)),
                  pl.BlockSpec((8, blk), lambda i: (jnp.maximum(i * (tm // 8) - 1, 0), qkvz_block)),
                  pl.BlockSpec((8, blk),
                               lambda i: (jnp.minimum((i + 1) * (tm // 8), nb8 - 1), qkvz_block)),
                  pl.BlockSpec((tm, 8 * LANES), lambda i: (i, ab_block)),
                  full(conv_w), full(a_log), full(dt_bias)],
        out_specs=[row_spec, row_spec, row_spec, small_spec, small_spec],
        out_shape=[jax.ShapeDtypeStruct((m, width), BF16)] * 3
        + [jax.ShapeDtypeStruct((m, n2h), F32)] * 2,
        compiler_params=_params(("parallel",)),
        name="gdprep",
    )(p, p, p, p, conv_w, a_log, dt_bias)


def _gdscan_kernel(q_ref, k_ref, v_ref, g_ref, gt_ref, beta_ref, s0_ref, *refs, want_out):
    if want_out:
        o_ref, s_ref, st_ref = refs
    else:
        s_ref, st_ref = refs
    d = pl.program_id(1)
    c = pl.program_id(2)
    n_heads = st_ref.shape[0]

    @pl.when(c == 0)
    def _():
        st_ref[...] = s0_ref[0, 0]

    sgn = 1 - 2 * d
    i = lax.broadcasted_iota(jnp.int32, (CHUNK, CHUNK), 0)
    j = lax.broadcasted_iota(jnp.int32, (CHUNK, CHUNK), 1)
    rel = (j - i) * sgn
    strict = rel < 0
    incl = rel <= 0
    eye = (rel == 0).astype(F32)
    blk = _level_masks(i, j)

    for h in range(n_heads):
        cols = slice(h * GD_HEAD, (h + 1) * GD_HEAD)
        q = q_ref[:, cols].astype(F32)
        k = k_ref[:, cols].astype(F32)
        v = v_ref[:, cols].astype(F32)
        gcol = g_ref[0, :, h:h + 1]
        grow = gt_ref[0, 0, h:h + 1, :]
        bcol = beta_ref[0, :, h:h + 1]
        decay = jnp.where(incl, jnp.exp(jnp.where(incl, gcol - grow, 0.0)), 0.0)
        kb = k * bcol
        s2 = _mm_nt(jnp.concatenate([kb, q], axis=0), k)
        a = jnp.where(strict, s2[:CHUNK] * decay, 0.0)
        t = eye - jnp.where(blk[0], a, 0.0)
        for lvl in range(1, len(blk)):
            off = jnp.where(blk[lvl], a, 0.0)
            t = t - _mm(_mm(t, off), t)
        eg = jnp.exp(gcol)
        sol = _mm(t, jnp.concatenate([v * bcol, kb * eg], axis=1))
        u = sol[:, :GD_HEAD]
        wmat = sol[:, GD_HEAD:]
        s = st_ref[h]
        v_new = u - _mm(wmat, s)
        if want_out:
            attn = s2[CHUNK:] * decay
            o_ref[0, :, cols] = _mm(q * eg, s) + _mm(attn, v_new)
        g_last = jnp.min(gcol, axis=0, keepdims=True)
        k_dec = k * jnp.exp(g_last - gcol)
        st_ref[h] = s * jnp.exp(g_last) + _mm_tn(k_dec, v_new)

    @pl.when(c == pl.num_programs(2) - 1)
    def _():
        s_ref[0, 0] = st_ref[...]


def _gdscan(q, k, v, g, gt, beta, s0, batch, want_out):
    m, w = q.shape
    n_chunks = m // batch // CHUNK
    n_heads = w // GD_HEAD

    def row(b, d, c):
        return b * n_chunks + c + d * (n_chunks - 1 - 2 * c)

    row_spec = pl.BlockSpec((CHUNK, w), lambda b, d, c: (row(b, d, c), 0))
    col_spec = pl.BlockSpec((1, CHUNK, n_heads), lambda b, d, c: (d, row(b, d, c), 0))
    state_spec = pl.BlockSpec((1, 1, n_heads, GD_HEAD, GD_HEAD), lambda b, d, c: (b, d, 0, 0, 0))
    out_specs = [state_spec]
    out_shape = [jax.ShapeDtypeStruct(s0.shape, F32)]
    if want_out:
        out_specs = [pl.BlockSpec((1, CHUNK, w), lambda b, d, c: (d, row(b, d, c), 0))] + out_specs
        out_shape = [jax.ShapeDtypeStruct((2, m, w), F32)] + out_shape
    return pl.pallas_call(
        functools.partial(_gdscan_kernel, want_out=want_out),
        grid=(batch, 2, n_chunks),
        in_specs=[row_spec, row_spec, row_spec, col_spec,
                  pl.BlockSpec((1, 1, n_heads, CHUNK), lambda b, d, c: (d, row(b, d, c), 0, 0)),
                  col_spec, state_spec],
        out_specs=out_specs,
        out_shape=out_shape,
        scratch_shapes=[pltpu.VMEM((n_heads, GD_HEAD, GD_HEAD), F32)],
        compiler_params=_params(("parallel", "arbitrary", "arbitrary")),
        name="gdscan_out" if want_out else "gdscan_state",
    )(q, k, v, g, gt, beta, s0)


def _gdread_kernel(of_ref, ob_ref, z_ref, ng_ref, y_ref):
    o = of_ref[0] + ob_ref[0]
    z = z_ref[...]
    gate = z * _sigmoid(z)
    for h in range(o.shape[1] // GD_HEAD):
        cols = slice(h * GD_HEAD, (h + 1) * GD_HEAD)
        oh = o[:, cols]
        ms = jnp.mean(oh * oh, axis=-1, keepdims=True)
        y_ref[:, cols] = (oh * lax.rsqrt(ms + NORM_EPS) * ng_ref[...] * gate[:, cols]).astype(BF16)


def _gdread(o, p, z_block, norm_g):
    _, m, w = o.shape
    tm = _tile(m, 512)
    return pl.pallas_call(
        _gdread_kernel,
        grid=(m // tm,),
        in_specs=[pl.BlockSpec((1, tm, w), lambda i: (0, i, 0)),
                  pl.BlockSpec((1, tm, w), lambda i: (1, i, 0)),
                  pl.BlockSpec((tm, w), lambda i: (i, z_block)),
                  pl.BlockSpec((1, GD_HEAD), lambda i: (0, 0))],
        out_specs=pl.BlockSpec((tm, w), lambda i: (i, 0)),
        out_shape=jax.ShapeDtypeStruct((m, w), BF16),
        compiler_params=_params(("parallel",)),
        name="gdread",
    )(o, o, p, norm_g)


def _merge1_kernel(ya_ref, yb_ref, wa_ref, wb_ref, ga_ref, gb_ref, o_ref):
    a = jnp.dot(ya_ref[...], wa_ref[...], preferred_element_type=F32)
    b = jnp.dot(yb_ref[...], wb_ref[...], preferred_element_type=F32)
    o_ref[...] = (_sigmoid(ga_ref[...]) * a + _sigmoid(gb_ref[...]) * b).astype(BF16)


def _merge1(ya, yb, wa, wb, p, gate_col0):
    m, ka = ya.shape
    kb = yb.shape[1]
    d = wa.shape[1]
    tm = _tile(m, 1024)
    tn = _tile(d, 1024, LANES)
    ga0 = gate_col0 // tn
    gb0 = (gate_col0 + d) // tn
    return pl.pallas_call(
        _merge1_kernel,
        grid=(m // tm, d // tn),
        in_specs=[pl.BlockSpec((tm, ka), lambda i, j: (i, 0)),
                  pl.BlockSpec((tm, kb), lambda i, j: (i, 0)),
                  pl.BlockSpec((ka, tn), lambda i, j: (0, j)),
                  pl.BlockSpec((kb, tn), lambda i, j: (0, j)),
                  pl.BlockSpec((tm, tn), lambda i, j: (i, ga0 + j)),
                  pl.BlockSpec((tm, tn), lambda i, j: (i, gb0 + j))],
        out_specs=pl.BlockSpec((tm, tn), lambda i, j: (i, j)),
        out_shape=jax.ShapeDtypeStruct((m, d), BF16),
        compiler_params=_params(("parallel", "arbitrary")),
        name="merge1",
    )(ya, yb, wa, wb, p, p)


def _merge2_kernel(mg_ref, wo_ref, x_ref, mod_ref, o_ref, *, gate_row):
    y = jnp.dot(mg_ref[...], wo_ref[...], preferred_element_type=F32)
    o_ref[...] = x_ref[...] + mod_ref[0, gate_row:gate_row + 1, :] * y


def _merge2(merged, wo, x2, mod, rows_per_mod, gate_row):
    m, d = x2.shape
    tm = _tile(rows_per_mod, 1024)
    tn = _tile(d, 1024, LANES)
    per = rows_per_mod // tm
    return pl.pallas_call(
        functools.partial(_merge2_kernel, gate_row=gate_row),
        grid=(m // tm, d // tn),
        in_specs=[pl.BlockSpec((tm, d), lambda i, j: (i, 0)),
                  pl.BlockSpec((d, tn), lambda i, j: (0, j)),
                  pl.BlockSpec((tm, tn), lambda i, j: (i, j)),
                  pl.BlockSpec((1, 8, tn), lambda i, j: (i // per, 0, j))],
        out_specs=pl.BlockSpec((tm, tn), lambda i, j: (i, j)),
        out_shape=jax.ShapeDtypeStruct((m, d), F32),
        compiler_params=_params(("parallel", "arbitrary")),
        name="merge2",
    )(merged, wo, x2, mod)


def _convglu_kernel(gate_ref, val_ref, cw_ref, o_ref):
    g = gate_ref[...]
    n = g.shape[0]
    t = lax.broadcasted_iota(jnp.int32, g.shape, 0)
    col = t % GRID_W
    row = t // GRID_W
    n_rows = n // GRID_W
    left = jnp.where(col > 0, pltpu.roll(g, 1, 0), 0.0)
    right = jnp.where(col < GRID_W - 1, pltpu.roll(g, n - 1, 0), 0.0)
    acc = jnp.zeros_like(g)
    for dr in (-1, 0, 1):
        line = (left * cw_ref[3 * (dr + 1):3 * (dr + 1) + 1, :]
                + g * cw_ref[3 * (dr + 1) + 1:3 * (dr + 1) + 2, :]
                + right * cw_ref[3 * (dr + 1) + 2:3 * (dr + 1) + 3, :])
        if dr == -1:
            line = jnp.where(row > 0, pltpu.roll(line, GRID_W, 0), 0.0)
        elif dr == 1:
            line = jnp.where(row < n_rows - 1, pltpu.roll(line, n - GRID_W, 0), 0.0)
        acc = acc + line
    gelu = 0.5 * acc * (1.0 + lax.erf(acc * (2.0 ** -0.5)))
    o_ref[...] = (gelu * val_ref[...]).astype(BF16)


def _convglu(gv, conv_w, batch):
    m = gv.shape[0]
    dff = conv_w.shape[1]
    seq = m // batch
    tc = _tile(dff, 512, LANES)
    nblk = dff // tc
    return pl.pallas_call(
        _convglu_kernel,
        grid=(batch, nblk),
        in_specs=[pl.BlockSpec((seq, tc), lambda b, j: (b, j)),
                  pl.BlockSpec((seq, tc), lambda b, j: (b, nblk + j)),
                  pl.BlockSpec((conv_w.shape[0], tc), lambda b, j: (0, j))],
        out_specs=pl.BlockSpec((seq, tc), lambda b, j: (b, j)),
        out_shape=jax.ShapeDtypeStruct((m, dff), BF16),
        compiler_params=_params(("parallel", "arbitrary")),
        name="convglu",
    )(gv, gv, conv_w)


def _ffn_out_kernel(act_ref, w2_ref, x_ref, mod_ref, g_ref, o_ref, acc_ref, *, gate_row):
    kstep = pl.program_id(1)

    @pl.when(kstep == 0)
    def _():
        acc_ref[...] = jnp.zeros_like(acc_ref)

    acc_ref[...] += jnp.dot(act_ref[...], w2_ref[...], preferred_element_type=F32)

    @pl.when(kstep == pl.num_programs(1) - 1)
    def _():
        y = x_ref[...] + mod_ref[0, gate_row:gate_row + 1, :] * acc_ref[...]
        ms = jnp.mean(y * y, axis=-1, keepdims=True)
        o_ref[...] = y * lax.rsqrt(ms + NORM_EPS) * g_ref[...]


def _ffn_out(act, w2, x1, mod, final_g, rows_per_mod, gate_row):
    m, d = x1.shape
    dff = act.shape[1]
    tm = _tile(rows_per_mod, 512)
    tk = _tile(dff, 512, LANES)
    per = rows_per_mod // tm
    return pl.pallas_call(
        functools.partial(_ffn_out_kernel, gate_row=gate_row),
        grid=(m // tm, dff // tk),
        in_specs=[pl.BlockSpec((tm, tk), lambda i, k: (i, k)),
                  pl.BlockSpec((tk, d), lambda i, k: (k, 0)),
                  pl.BlockSpec((tm, d), lambda i, k: (i, 0)),
                  pl.BlockSpec((1, 8, d), lambda i, k: (i // per, 0, 0)),
                  pl.BlockSpec((1, d), lambda i, k: (0, 0))],
        out_specs=pl.BlockSpec((tm, d), lambda i, k: (i, 0)),
        out_shape=jax.ShapeDtypeStruct((m, d), F32),
        scratch_shapes=[pltpu.VMEM((tm, d), F32)],
        compiler_params=_params(("parallel", "arbitrary")),
        name="ffn_out",
    )(act, w2, x1, mod, final_g)


def _pad_cols(a, width):
    return jnp.pad(a, [(0, 0)] * (a.ndim - 1) + [(0, width - a.shape[-1])])


def _pad_rank(a):
    return jnp.pad(a, [(0, 0)] * (a.ndim - 2) + [(0, RANK_PAD - a.shape[-2]), (0, 0)])


def kernel(x, c, ctx, c_ctx, w_ada, b_ada, norm1_g, norm2_g, w_in, rw_mu, rw_k_k, rw_k_a, rw_r_k, rw_w0, rw_w_up, rw_a0, rw_a_up, rw_g_up, rw_gn_g, rw_gn_b, gd_conv_w, gd_a_log, gd_dt_bias, gd_norm_g, w_a_out, w_b_out, w_o, ffn_w1, ffn_conv_w, ffn_w2, final_norm_g):
    batch, seq, d = x.shape
    ctx_len = ctx.shape[1]
    assert w_ada.shape[0] == 1, "single layer only"
    rw_w = rw_k_k.shape[1]
    gd_w = w_b_out.shape[1]
    dec_rank = rw_w_up.shape[2]
    icl_rank = rw_a_up.shape[2]
    gate_rank = rw_g_up.shape[1]
    gd_heads = gd_a_log.shape[2]
    assert max(dec_rank, icl_rank) <= RANK_PAD and 4 * gd_heads <= 2 * LANES
    assert seq % CHUNK == 0 and ctx_len % CHUNK == 0 and seq % GRID_W == 0
    assert 3 * rw_w == 3 * gd_w and gate_rank <= 2 * LANES

    low_w = 8 * LANES
    blk0 = 3 * rw_w + low_w
    assert blk0 == 4 * gd_w
    wi = w_in[0]
    o_rw = 3 * rw_w
    o_gd = o_rw + 2 * dec_rank + 2 * icl_rank + gate_rank
    o_ab = o_gd + 4 * gd_w
    o_gate = o_ab + 4 * gd_heads

    def pack_cols(a):
        pieces = [a[..., :o_rw]]
        off = o_rw
        for r in (dec_rank, dec_rank, icl_rank, icl_rank):
            pieces.append(_pad_cols(a[..., off:off + r], RANK_PAD))
            off += r
        pieces.append(_pad_cols(a[..., off:off + gate_rank], 2 * LANES))
        return pieces

    w_pack = jnp.concatenate(
        pack_cols(wi) + [_pad_cols(wi[:, o_ab:o_gate], 2 * LANES), wi[:, o_gd:o_ab], wi[:, o_gate:]],
        axis=1).astype(BF16)
    n_ctx_cols = 2 * blk0
    gate_col0 = 2 * blk0
    mu_pack = jnp.concatenate(pack_cols(rw_mu) + [jnp.zeros((1, 2 * LANES), F32)], axis=1)

    cc = jnp.concatenate([c, c_ctx[None, :], jnp.zeros((16 - batch - 1, d), F32)], axis=0)
    mods = _mod(cc, w_ada[0], b_ada)
    mod_lat = _pad_rows8(mods[:batch].reshape(batch, 6, d))
    mod_ctx = _pad_rows8(mods[batch:batch + 1].reshape(1, 6, d))

    x2 = x.reshape(batch * seq, d)
    ctx2 = ctx.reshape(batch * ctx_len, d)
    p_lat = _normproj(x2, mod_lat, norm1_g, w_pack, seq, w_pack.shape[1], 0, 1, "inproj_lat")
    p_ctx = _normproj(ctx2, mod_ctx, norm1_g, w_pack, batch * ctx_len, n_ctx_cols, 0, 1, "inproj_ctx")

    rw_wts = (mu_pack, rw_k_k, rw_k_a, rw_r_k, rw_w0[0], _pad_rank(rw_w_up[0]).astype(BF16),
              rw_a0[0], _pad_rank(rw_a_up[0]).astype(BF16),
              jnp.pad(rw_g_up[0], ((0, 2 * LANES - gate_rank), (0, 0))).astype(BF16))
    n_pairs = rw_w // LANES
    s0 = jnp.zeros((batch, 2, n_pairs, LANES, LANES), F32)
    f_ctx = _rwprep(p_ctx, ctx_len, rw_wts)
    f_lat = _rwprep(p_lat, seq, rw_wts)
    (s_ctx,) = _rwscan(*f_ctx[:6], s0, batch, want_out=False)
    o_rwkv, _ = _rwscan(*f_lat[:6], s_ctx, batch, want_out=True)
    ya = _rwread(o_rwkv, f_lat[6], f_lat[7], rw_gn_g, rw_gn_b)

    s0g = jnp.zeros((batch, 2, gd_heads, GD_HEAD, GD_HEAD), F32)
    a_log2 = gd_a_log[0].reshape(1, 2 * gd_heads)
    dtb2 = gd_dt_bias[0].reshape(1, 2 * gd_heads)

    def gd_feats(p, seq_len):
        q, k, v, gcum, beta = _gdprep(p, seq_len, gd_conv_w[0], a_log2, dtb2, 1, 3, gd_w)
        m = p.shape[0]
        g3 = gcum.reshape(m, 2, gd_heads).transpose(1, 0, 2)
        gt = g3.reshape(2, m // CHUNK, CHUNK, gd_heads).transpose(0, 1, 3, 2)
        b3 = beta.reshape(m, 2, gd_heads).transpose(1, 0, 2)
        return q, k, v, g3, gt, b3

    (sg_ctx,) = _gdscan(*gd_feats(p_ctx, ctx_len), s0g, batch, want_out=False)
    o_gdn, _ = _gdscan(*gd_feats(p_lat, seq), sg_ctx, batch, want_out=True)
    yb = _gdread(o_gdn, p_lat, 2 * blk0 // gd_w - 1, gd_norm_g)

    merged = _merge1(ya, yb, w_a_out[0].astype(BF16), w_b_out[0].astype(BF16), p_lat, gate_col0)
    x1 = _merge2(merged, w_o[0].astype(BF16), x2, mod_lat, seq, 2)

    gv = _normproj(x1, mod_lat, norm2_g, ffn_w1[0].astype(BF16), seq, ffn_w1.shape[2], 3, 4, "ffn_in")
    act = _convglu(gv, ffn_conv_w[0].reshape(-1, ffn_conv_w.shape[-1]), batch)
    out = _ffn_out(act, ffn_w2[0].astype(BF16), x1, mod_lat, final_norm_g[None, :], seq, 5)
    return out.reshape(batch, seq, d)


def _pad_rows8(a):
    return jnp.pad(a, ((0, 0), (0, 8 - a.shape[1]), (0, 0)))
```

```python
import functools

import jax
import jax.numpy as jnp
from jax import lax
from jax.experimental import pallas as pl
from jax.experimental.pallas import tpu as pltpu

F32 = jnp.float32
BF16 = jnp.bfloat16
HIGHEST = lax.Precision.HIGHEST

NORM_EPS = 1e-6
RW_GN_EPS = 64e-5
RW_HEAD = 64
GD_HEAD = 128
LANES = 128
CHUNK = 64
GRID_W = 64
RANK_PAD = 128
VMEM_LIMIT = 56 * 1024 * 1024


def _params(sem):
    return pltpu.CompilerParams(dimension_semantics=sem, vmem_limit_bytes=VMEM_LIMIT)


def _tile(n, pref, mult=8):
    if n <= pref:
        return n
    t = (pref // mult) * mult
    while t >= mult:
        if n % t == 0:
            return t
        t -= mult
    return n


def _mm(a, b):
    return jnp.dot(a.astype(BF16), b.astype(BF16), preferred_element_type=F32)


def _mm_nt(a, b):
    return lax.dot_general(a.astype(BF16), b.astype(BF16), (((1,), (1,)), ((), ())),
                           preferred_element_type=F32)


def _mm_tn(a, b):
    return lax.dot_general(a.astype(BF16), b.astype(BF16), (((0,), (0,)), ((), ())),
                           preferred_element_type=F32)


def _mm_hi(a, b):
    return jnp.dot(a, b, precision=HIGHEST, preferred_element_type=F32)


def _softplus(x):
    return jnp.maximum(x, 0.0) + jnp.log(1.0 + jnp.exp(-jnp.abs(x)))


def _sigmoid(x):
    return 1.0 / (1.0 + jnp.exp(-x))


def _seg_ones(width):
    i = lax.broadcasted_iota(jnp.int32, (LANES, LANES), 0) // width
    j = lax.broadcasted_iota(jnp.int32, (LANES, LANES), 1) // width
    return (i == j).astype(F32)


def _seg_sum(x, width):
    e = _seg_ones(width)
    n = x.shape[-1] // LANES
    parts = [_mm_hi(x[:, g * LANES:(g + 1) * LANES], e) for g in range(n)]
    return parts[0] if n == 1 else jnp.concatenate(parts, axis=-1)


def _tri(rev):
    i = lax.broadcasted_iota(jnp.int32, (CHUNK, CHUNK), 0)
    j = lax.broadcasted_iota(jnp.int32, (CHUNK, CHUNK), 1)
    return ((j >= i) if rev else (j <= i)).astype(F32)


def _level_masks(i, j):
    masks = [(i // 2) == (j // 2)]
    s = 2
    while s < CHUNK:
        masks.append(((i // (2 * s)) == (j // (2 * s))) & ((i // s) != (j // s)))
        s *= 2
    return masks


def _shift_rows(x, prev_row, next_row):
    n = x.shape[0]
    row = lax.broadcasted_iota(jnp.int32, x.shape, 0)
    xm1 = jnp.where(row == 0, prev_row, pltpu.roll(x, 1, 0))
    xp1 = jnp.where(row == n - 1, next_row, pltpu.roll(x, n - 1, 0))
    return xm1, xp1


def _mod_kernel(c_ref, w_ref, b_ref, o_ref):
    c = c_ref[...]
    s = c * _sigmoid(c)
    o_ref[...] = _mm_hi(s, w_ref[...]) + b_ref[...]


def _mod(cc, w_ada, b_ada):
    rows, d = cc.shape
    n = w_ada.shape[1]
    tn = _tile(n, 1024, LANES)
    return pl.pallas_call(
        _mod_kernel,
        grid=(n // tn,),
        in_specs=[pl.BlockSpec((rows, d), lambda j: (0, 0)),
                  pl.BlockSpec((d, tn), lambda j: (0, j)),
                  pl.BlockSpec((1, tn), lambda j: (0, j))],
        out_specs=pl.BlockSpec((rows, tn), lambda j: (0, j)),
        out_shape=jax.ShapeDtypeStruct((rows, n), F32),
        compiler_params=_params(("arbitrary",)),
        name="mod",
    )(cc, w_ada, b_ada)


def _normproj_kernel(x_ref, mod_ref, g_ref, w_ref, o_ref, h_ref, *, sh_row, sc_row):
    @pl.when(pl.program_id(1) == 0)
    def _():
        x = x_ref[...]
        ms = jnp.mean(x * x, axis=-1, keepdims=True)
        y = x * lax.rsqrt(ms + NORM_EPS) * g_ref[...]
        sh = mod_ref[0, sh_row:sh_row + 1, :]
        sc = mod_ref[0, sc_row:sc_row + 1, :]
        h_ref[...] = (y * (1.0 + sc) + sh).astype(BF16)

    o_ref[...] = jnp.dot(h_ref[...], w_ref[...], preferred_element_type=F32)


def _normproj(x2, mod, gain, w, rows_per_mod, n_cols, sh_row, sc_row, name):
    m, d = x2.shape
    tm = _tile(rows_per_mod, 1024)
    tn = _tile(n_cols, 1024, LANES)
    per = rows_per_mod // tm
    return pl.pallas_call(
        functools.partial(_normproj_kernel, sh_row=sh_row, sc_row=sc_row),
        grid=(m // tm, n_cols // tn),
        in_specs=[pl.BlockSpec((tm, d), lambda i, j: (i, 0)),
                  pl.BlockSpec((1, 8, d), lambda i, j: (i // per, 0, 0)),
                  pl.BlockSpec((1, d), lambda i, j: (0, 0)),
                  pl.BlockSpec((d, tn), lambda i, j: (0, j))],
        out_specs=pl.BlockSpec((tm, tn), lambda i, j: (i, j)),
        out_shape=jax.ShapeDtypeStruct((m, n_cols), F32),
        scratch_shapes=[pltpu.VMEM((tm, d), BF16)],
        compiler_params=_params(("parallel", "arbitrary")),
        name=name,
    )(x2, mod, gain, w)


def _rwprep_kernel(pc_ref, pp_ref, pn_ref, mu_ref, kk_ref, ka_ref, rk_ref, w0_ref, wup_ref,
                   a0_ref, aup_ref, gup_ref,
                   at_ref, bt_ref, kt_ref, rt_ref, v_ref, pt_ref, bonus_ref, g_ref, *, tiles_per_seq):
    i = pl.program_id(0)
    tm = pc_ref.shape[0]
    w = RW_HEAD * (kk_ref.shape[1] // RW_HEAD)
    first = (i % tiles_per_seq) == 0
    last = (i % tiles_per_seq) == tiles_per_seq - 1
    x = pc_ref[...]
    prev_row = jnp.where(first, 0.0, pp_ref[7:8, :])
    next_row = jnp.where(last, 0.0, pn_ref[0:1, :])
    xm1, xp1 = _shift_rows(x, prev_row, next_row)
    xs = x + mu_ref[...] * (0.5 * (xm1 + xp1) - x)

    r = xs[:, 0:w]
    k = xs[:, w:2 * w]
    v = xs[:, 2 * w:3 * w]
    base = 3 * w
    gd = xs[:, base + 4 * RANK_PAD: base + 4 * RANK_PAD + gup_ref.shape[0]]
    g_ref[...] = _mm(_sigmoid(gd), gup_ref[...])
    v_ref[...] = v.astype(BF16)

    kx = k * kk_ref[...]
    kk = kx * lax.rsqrt(jnp.maximum(_seg_sum(kx * kx, RW_HEAD), 1e-12))

    ksum = jnp.zeros_like(k)
    for d in range(2):
        wd = xs[:, base + d * RANK_PAD: base + (d + 1) * RANK_PAD]
        ad = xs[:, base + (2 + d) * RANK_PAD: base + (3 + d) * RANK_PAD]
        wl = w0_ref[d:d + 1, :] + _mm(jnp.tanh(wd), wup_ref[d])
        lw = -jnp.exp(-_softplus(-wl) - 0.5)
        a = _sigmoid(a0_ref[d:d + 1, :] + _mm(ad, aup_ref[d]))
        kd = k * (1.0 + (a - 1.0) * ka_ref[...])
        ksum = ksum + kd
        tri = _tri(rev=(d == 1))
        for c in range(tm // CHUNK):
            rows = slice(c * CHUNK, (c + 1) * CHUNK)
            lwc = lw[rows]
            cum = _mm_hi(tri, lwc)
            p_in = jnp.exp(cum)
            p_inv = jnp.exp(-cum)
            p_ex = jnp.exp(cum - lwc)
            at_ref[d, rows, :] = (-kk[rows] * p_ex).astype(BF16)
            bt_ref[d, rows, :] = (kk[rows] * a[rows] * p_inv).astype(BF16)
            kt_ref[d, rows, :] = (kd[rows] * p_inv).astype(BF16)
            rt_ref[d, rows, :] = (r[rows] * p_in).astype(BF16)
            tot = cum[CHUNK - 1:CHUNK] if d == 0 else cum[0:1]
            pt_ref[d, c, :, :] = jnp.exp(tot)
    bonus_ref[...] = _seg_sum(r * ksum * rk_ref[...], RW_HEAD) * v


def _rwprep(p, seq_len, wts):
    m = p.shape[0]
    mu, k_k, k_a, r_k, w0, w_up, a0, a_up, g_up = wts
    w = k_k.shape[1]
    blk = mu.shape[1]
    tm = _tile(seq_len, 256, CHUNK)
    tps = seq_len // tm
    nb8 = m // 8
    full = lambda a: pl.BlockSpec(a.shape, lambda i: (0,) * a.ndim)
    feat = jax.ShapeDtypeStruct((2, m, w), BF16)
    feat_spec = pl.BlockSpec((2, tm, w), lambda i: (0, i, 0))
    row_spec = pl.BlockSpec((tm, w), lambda i: (i, 0))
    return pl.pallas_call(
        functools.partial(_rwprep_kernel, tiles_per_seq=tps),
        grid=(m // tm,),
        in_specs=[pl.BlockSpec((tm, blk), lambda i: (i, 0)),
                  pl.BlockSpec((8, blk), lambda i: (jnp.maximum(i * (tm // 8) - 1, 0), 0)),
                  pl.BlockSpec((8, blk), lambda i: (jnp.minimum((i + 1) * (tm // 8), nb8 - 1), 0)),
                  full(mu), full(k_k), full(k_a), full(r_k), full(w0), full(w_up), full(a0),
                  full(a_up), full(g_up)],
        out_specs=[feat_spec, feat_spec, feat_spec, feat_spec, row_spec,
                   pl.BlockSpec((2, tm // CHUNK, 1, w), lambda i: (0, i, 0, 0)),
                   row_spec, row_spec],
        out_shape=[feat, feat, feat, feat, jax.ShapeDtypeStruct((m, w), BF16),
                   jax.ShapeDtypeStruct((2, m // CHUNK, 1, w), F32),
                   jax.ShapeDtypeStruct((m, w), F32), jax.ShapeDtypeStruct((m, w), F32)],
        compiler_params=_params(("parallel",)),
        name="rwprep",
    )(p, p, p, mu, k_k, k_a, r_k, w0, w_up, a0, a_up, g_up)


def _pair_blockdiag(x, m0):
    zero = jnp.zeros_like(x)
    return jnp.concatenate([jnp.where(m0, x, zero), jnp.where(m0, zero, x)], axis=0)


def _tri_inverse(a, eye, blk, m0, sign):
    ts = [eye + sign * jnp.where(blk[0], x, 0.0) for x in a]
    for lvl in range(1, len(blk)):
        xs = [_mm(t, _pair_blockdiag(jnp.where(blk[lvl], x, 0.0), m0)) for t, x in zip(ts, a)]
        ts = [t + sign * _mm(x, _pair_blockdiag(t, m0)) for t, x in zip(ts, xs)]
    return ts


def _pair_masks(head_cols):
    i = lax.broadcasted_iota(jnp.int32, (CHUNK, LANES), 0)
    lane = lax.broadcasted_iota(jnp.int32, (CHUNK, LANES), 1)
    j = lane % head_cols
    strict = (j < i, j > i)
    incl = (j <= i, j >= i)
    eye = (j == i).astype(F32)
    return strict, incl, eye, _level_masks(i, j), lane < head_cols


def _rwscan_kernel(*refs, want_out):
    ins, rest = refs[:12], refs[12:]
    s0_ref = rest[0]
    if want_out:
        o_refs, s_ref, h_ref = rest[1:3], rest[3], rest[4]
    else:
        s_ref, h_ref = rest[1], rest[2]
    c = pl.program_id(1)
    n_pairs = h_ref.shape[1]

    @pl.when(c == 0)
    def _():
        h_ref[...] = s0_ref[0]

    strict, incl, eye, blk, m0 = _pair_masks(RW_HEAD)
    r2 = lax.broadcasted_iota(jnp.int32, (LANES, LANES), 0) // RW_HEAD
    c2 = lax.broadcasted_iota(jnp.int32, (LANES, LANES), 1) // RW_HEAD
    diag2 = r2 == c2

    chains = [(d, p) for d in range(2) for p in range(n_pairs)]
    cols = lambda p: slice(p * LANES, (p + 1) * LANES)
    at = [ins[6 * d + 0][0, :, cols(p)] for d, p in chains]
    bt = [ins[6 * d + 1][0, :, cols(p)] for d, p in chains]
    kt = [ins[6 * d + 2][0, :, cols(p)] for d, p in chains]
    rt = [ins[6 * d + 3][0, :, cols(p)] for d, p in chains]
    v = [ins[6 * d + 4][:, cols(p)] for d, p in chains]
    pt = [ins[6 * d + 5][0, 0, :, cols(p)] for d, p in chains]
    n = len(chains)
    bd = lambda x: _pair_blockdiag(x, m0)

    s4 = [_mm_nt(jnp.concatenate([at[i], rt[i]], axis=0),
                 jnp.concatenate([bd(bt[i]), bd(kt[i])], axis=0)) for i in range(n)]
    a_ab = [jnp.where(strict[chains[i][0]], s4[i][:CHUNK, :LANES], 0.0) for i in range(n)]
    a_ak = [jnp.where(strict[chains[i][0]], s4[i][:CHUNK, LANES:], 0.0) for i in range(n)]
    t = _tri_inverse(a_ab, eye, blk, m0, 1.0)
    av = [_mm(a_ak[i], bd(v[i])) for i in range(n)]
    wu = [_mm(t[i], jnp.concatenate([bd(at[i]), bd(av[i].astype(BF16))], axis=1)) for i in range(n)]

    ht = [h_ref[d, p] for d, p in chains]
    if want_out:
        m1 = [_mm_nt(jnp.concatenate([wu[i][:, :LANES].astype(BF16), rt[i]], axis=0), ht[i])
              for i in range(n)]
        u = [m1[i][:CHUNK] + wu[i][:, LANES:] for i in range(n)]
    else:
        u = [_mm_nt(wu[i][:, :LANES], ht[i]) + wu[i][:, LANES:] for i in range(n)]
    ub = [x.astype(BF16) for x in u]
    if want_out:
        for i, (d, p) in enumerate(chains):
            m_r = jnp.where(jnp.concatenate([incl[d], incl[d]], axis=1), s4[i][CHUNK:], 0.0)
            o = m1[i][CHUNK:] + _mm(m_r, jnp.concatenate([bd(ub[i]), bd(v[i])], axis=0))
            o_refs[d][:, cols(p)] = o
    for i, (d, p) in enumerate(chains):
        upd = _mm_tn(jnp.concatenate([ub[i], v[i]], axis=0),
                     jnp.concatenate([bt[i], kt[i]], axis=0))
        h_ref[d, p] = (ht[i] + jnp.where(diag2, upd, 0.0)) * pt[i]

    @pl.when(c == pl.num_programs(1) - 1)
    def _():
        s_ref[0] = h_ref[...]


def _rwscan(at, bt, kt, rt, v, pt, s0, batch, want_out):
    m, w = v.shape
    n_chunks = m // batch // CHUNK
    n_pairs = w // LANES
    rows = (lambda b, c: b * n_chunks + c, lambda b, c: b * n_chunks + n_chunks - 1 - c)

    in_specs, args = [], []
    for d in range(2):
        feat_spec = pl.BlockSpec((1, CHUNK, w), lambda b, c, d=d: (d, rows[d](b, c), 0))
        in_specs += [feat_spec] * 4
        in_specs += [pl.BlockSpec((CHUNK, w), lambda b, c, d=d: (rows[d](b, c), 0)),
                     pl.BlockSpec((1, 1, 1, w), lambda b, c, d=d: (d, rows[d](b, c), 0, 0))]
        args += [at, bt, kt, rt, v, pt]
    state_spec = pl.BlockSpec((1, 2, n_pairs, LANES, LANES), lambda b, c: (b, 0, 0, 0, 0))
    out_specs = [state_spec]
    out_shape = [jax.ShapeDtypeStruct(s0.shape, F32)]
    if want_out:
        out_specs = [pl.BlockSpec((CHUNK, w), lambda b, c, d=d: (rows[d](b, c), 0))
                     for d in range(2)] + out_specs
        out_shape = [jax.ShapeDtypeStruct((m, w), F32)] * 2 + out_shape
    return pl.pallas_call(
        functools.partial(_rwscan_kernel, want_out=want_out),
        grid=(batch, n_chunks),
        in_specs=in_specs + [state_spec],
        out_specs=out_specs,
        out_shape=out_shape,
        scratch_shapes=[pltpu.VMEM((2, n_pairs, LANES, LANES), F32)],
        compiler_params=_params(("parallel", "arbitrary")),
        name="rwscan_out" if want_out else "rwscan_state",
    )(*args, s0)


def _rwread_kernel(of_ref, ob_ref, bonus_ref, g_ref, gng_ref, gnb_ref, y_ref):
    o = of_ref[...] + ob_ref[...]
    inv = 1.0 / RW_HEAD
    mean = _seg_sum(o, RW_HEAD) * inv
    cen = o - mean
    var = _seg_sum(cen * cen, RW_HEAD) * inv
    on = cen * lax.rsqrt(var + RW_GN_EPS) * gng_ref[...] + gnb_ref[...]
    y_ref[...] = ((on + bonus_ref[...]) * g_ref[...]).astype(BF16)


def _rwread(o_f, o_b, bonus, g, gn_g, gn_b):
    m, w = o_f.shape
    tm = _tile(m, 512)
    row_spec = pl.BlockSpec((tm, w), lambda i: (i, 0))
    vec_spec = pl.BlockSpec((1, w), lambda i: (0, 0))
    return pl.pallas_call(
        _rwread_kernel,
        grid=(m // tm,),
        in_specs=[row_spec, row_spec, row_spec, row_spec, vec_spec, vec_spec],
        out_specs=row_spec,
        out_shape=jax.ShapeDtypeStruct((m, w), BF16),
        compiler_params=_params(("parallel",)),
        name="rwread",
    )(o_f, o_b, bonus, g, gn_g, gn_b)


def _gdprep_kernel(pc_ref, pp_ref, pn_ref, ab_ref, cw_ref, alog_ref, dtb_ref,
                   q_ref, k_ref, v_ref, gcum_ref, beta_ref, *, tiles_per_seq):
    i = pl.program_id(0)
    tm = pc_ref.shape[0]
    w = q_ref.shape[1]
    n_heads = w // GD_HEAD
    first = (i % tiles_per_seq) == 0
    last = (i % tiles_per_seq) == tiles_per_seq - 1
    x = pc_ref[:, 0:3 * w]
    prev_row = jnp.where(first, 0.0, pp_ref[7:8, 0:3 * w])
    next_row = jnp.where(last, 0.0, pn_ref[0:1, 0:3 * w])
    xm1, xp1 = _shift_rows(x, prev_row, next_row)
    y = xm1 * cw_ref[0:1, :] + x * cw_ref[1:2, :] + xp1 * cw_ref[2:3, :]
    y = y * _sigmoid(y)
    for h in range(n_heads):
        for part, ref, scale in ((0, q_ref, GD_HEAD ** -0.5), (1, k_ref, 1.0)):
            cols = slice(part * w + h * GD_HEAD, part * w + (h + 1) * GD_HEAD)
            t = y[:, cols]
            ss = jnp.sum(t * t, axis=-1, keepdims=True)
            ref[:, h * GD_HEAD:(h + 1) * GD_HEAD] = (
                t * (lax.rsqrt(jnp.maximum(ss, 1e-12)) * scale)).astype(BF16)
    v_ref[...] = y[:, 2 * w:3 * w].astype(BF16)

    ab = ab_ref[:, 6 * LANES:7 * LANES]
    a = ab[:, 0:2 * n_heads]
    b = ab[:, 2 * n_heads:4 * n_heads]
    glog = -jnp.exp(alog_ref[...]) * _softplus(a + dtb_ref[...])
    beta_ref[...] = _sigmoid(b)
    for c in range(tm // CHUNK):
        rows = slice(c * CHUNK, (c + 1) * CHUNK)
        gc = glog[rows]
        fwd = _mm_hi(_tri(False), gc)
        bwd = _mm_hi(_tri(True), gc)
        col = lax.broadcasted_iota(jnp.int32, gc.shape, 1)
        gcum_ref[rows, :] = jnp.where(col < n_heads, fwd, bwd)


def _gdprep(p, seq_len, conv_w, a_log, dt_bias, qkvz_block, ab_block, width):
    m = p.shape[0]
    blk = 4 * width
    tm = _tile(seq_len, 256, CHUNK)
    tps = seq_len // tm
    nb8 = m // 8
    n2h = a_log.shape[1]
    full = lambda a: pl.BlockSpec(a.shape, lambda i: (0,) * a.ndim)
    row_spec = pl.BlockSpec((tm, width), lambda i: (i, 0))
    small_spec = pl.BlockSpec((tm, n2h), lambda i: (i, 0))
    return pl.pallas_call(
        functools.partial(_gdprep_kernel, tiles_per_seq=tps),
        grid=(m // tm,),
        in_specs=[pl.BlockSpec((tm, blk), lambda i: (i, qkvz_block)),
                  pl.BlockSpec((8, blk), lambda i: (jnp.maximum(i * (tm // 8) - 1, 0), qkvz_block)),
                  pl.BlockSpec((8, blk),
                               lambda i: (jnp.minimum((i + 1) * (tm // 8), nb8 - 1), qkvz_block)),
                  pl.BlockSpec((tm, 8 * LANES), lambda i: (i, ab_block)),
                  full(conv_w), full(a_log), full(dt_bias)],
        out_specs=[row_spec, row_spec, row_spec, small_spec, small_spec],
        out_shape=[jax.ShapeDtypeStruct((m, width), BF16)] * 3
        + [jax.ShapeDtypeStruct((m, n2h), F32)] * 2,
        compiler_params=_params(("parallel",)),
        name="gdprep",
    )(p, p, p, p, conv_w, a_log, dt_bias)


def _gdscan_kernel(*refs, want_out):
    ins, rest = refs[:12], refs[12:]
    s0_ref = rest[0]
    if want_out:
        o_refs, s_ref, st_ref = rest[1:3], rest[3], rest[4]
    else:
        s_ref, st_ref = rest[1], rest[2]
    c = pl.program_id(1)
    n_pairs = st_ref.shape[1] // 2

    @pl.when(c == 0)
    def _():
        st_ref[...] = s0_ref[0]

    strict, incl, eye, blk, m0 = _pair_masks(CHUNK)
    bd = lambda x: _pair_blockdiag(x, m0)
    chains = [(d, p) for d in range(2) for p in range(n_pairs)]
    n = len(chains)
    hcols = lambda h: slice(h * GD_HEAD, (h + 1) * GD_HEAD)

    def head_vals(d, p, e):
        h = 2 * p + e
        q_ref, k_ref, v_ref, g_ref, _, beta_ref = ins[6 * d:6 * d + 6]
        return (q_ref[:, hcols(h)].astype(F32), k_ref[:, hcols(h)].astype(F32),
                v_ref[:, hcols(h)].astype(F32), g_ref[0, :, h:h + 1], beta_ref[0, :, h:h + 1])

    hv = [[head_vals(d, p, e) for e in range(2)] for d, p in chains]
    kb = [[hv[i][e][1] * hv[i][e][4] for e in range(2)] for i in range(n)]
    eg = [[jnp.exp(hv[i][e][3]) for e in range(2)] for i in range(n)]
    decay, s2 = [], []
    for i, (d, p) in enumerate(chains):
        gcol = jnp.where(m0, hv[i][0][3], hv[i][1][3])
        grow = ins[6 * d + 4][0, 0, p:p + 1, :]
        decay.append(jnp.where(incl[d], jnp.exp(jnp.where(incl[d], gcol - grow, 0.0)), 0.0))
        k0, k1 = hv[i][0][1], hv[i][1][1]
        zero = jnp.zeros_like(k0)
        lhs = jnp.concatenate([jnp.concatenate([kb[i][0], kb[i][1]], axis=1),
                               jnp.concatenate([hv[i][0][0], hv[i][1][0]], axis=1)], axis=0)
        rhs = jnp.concatenate([jnp.concatenate([k0, zero], axis=1),
                               jnp.concatenate([zero, k1], axis=1)], axis=0)
        s2.append(_mm_nt(lhs, rhs))
    a = [jnp.where(strict[chains[i][0]], s2[i][:CHUNK] * decay[i], 0.0) for i in range(n)]
    t = _tri_inverse(a, eye, blk, m0, -1.0)
    sol = [_mm(bd(t[i]), jnp.concatenate(
        [jnp.concatenate([hv[i][e][2] * hv[i][e][4], kb[i][e] * eg[i][e]], axis=1)
         for e in range(2)], axis=0)) for i in range(n)]

    st = [[st_ref[d, 2 * p + e] for e in range(2)] for d, p in chains]
    ws = [[_mm(jnp.concatenate([sol[i][e * CHUNK:(e + 1) * CHUNK, GD_HEAD:],
                                hv[i][e][0] * eg[i][e]], axis=0), st[i][e])
           for e in range(2)] for i in range(n)]
    v_new = [[sol[i][e * CHUNK:(e + 1) * CHUNK, :GD_HEAD] - ws[i][e][:CHUNK] for e in range(2)]
             for i in range(n)]
    if want_out:
        for i, (d, p) in enumerate(chains):
            intra = _mm(bd(s2[i][CHUNK:] * decay[i]),
                        jnp.concatenate([v_new[i][0], v_new[i][1]], axis=0))
            for e in range(2):
                o_refs[d][:, hcols(2 * p + e)] = (ws[i][e][CHUNK:]
                                                  + intra[e * CHUNK:(e + 1) * CHUNK])
    for i, (d, p) in enumerate(chains):
        for e in range(2):
            gcol = hv[i][e][3]
            g_last = jnp.min(gcol, axis=0, keepdims=True)
            k_dec = hv[i][e][1] * jnp.exp(g_last - gcol)
            st_ref[d, 2 * p + e] = st[i][e] * jnp.exp(g_last) + _mm_tn(k_dec, v_new[i][e])

    @pl.when(c == pl.num_programs(1) - 1)
    def _():
        s_ref[0] = st_ref[...]


def _gdscan(q, k, v, g, gt, beta, s0, batch, want_out):
    m, w = q.shape
    n_chunks = m // batch // CHUNK
    n_heads = w // GD_HEAD
    rows = (lambda b, c: b * n_chunks + c, lambda b, c: b * n_chunks + n_chunks - 1 - c)

    in_specs, args = [], []
    for d in range(2):
        row_spec = pl.BlockSpec((CHUNK, w), lambda b, c, d=d: (rows[d](b, c), 0))
        col_spec = pl.BlockSpec((1, CHUNK, n_heads), lambda b, c, d=d: (d, rows[d](b, c), 0))
        in_specs += [row_spec, row_spec, row_spec, col_spec,
                     pl.BlockSpec((1, 1, n_heads // 2, LANES),
                                  lambda b, c, d=d: (d, rows[d](b, c), 0, 0)),
                     col_spec]
        args += [q, k, v, g, gt, beta]
    state_spec = pl.BlockSpec((1, 2, n_heads, GD_HEAD, GD_HEAD), lambda b, c: (b, 0, 0, 0, 0))
    out_specs = [state_spec]
    out_shape = [jax.ShapeDtypeStruct(s0.shape, F32)]
    if want_out:
        out_specs = [pl.BlockSpec((CHUNK, w), lambda b, c, d=d: (rows[d](b, c), 0))
                     for d in range(2)] + out_specs
        out_shape = [jax.ShapeDtypeStruct((m, w), F32)] * 2 + out_shape
    return pl.pallas_call(
        functools.partial(_gdscan_kernel, want_out=want_out),
        grid=(batch, n_chunks),
        in_specs=in_specs + [state_spec],
        out_specs=out_specs,
        out_shape=out_shape,
        scratch_shapes=[pltpu.VMEM((2, n_heads, GD_HEAD, GD_HEAD), F32)],
        compiler_params=_params(("parallel", "arbitrary")),
        name="gdscan_out" if want_out else "gdscan_state",
    )(*args, s0)


def _gdread_kernel(of_ref, ob_ref, z_ref, ng_ref, y_ref):
    o = of_ref[...] + ob_ref[...]
    z = z_ref[...]
    gate = z * _sigmoid(z)
    for h in range(o.shape[1] // GD_HEAD):
        cols = slice(h * GD_HEAD, (h + 1) * GD_HEAD)
        oh = o[:, cols]
        ms = jnp.mean(oh * oh, axis=-1, keepdims=True)
        y_ref[:, cols] = (oh * lax.rsqrt(ms + NORM_EPS) * ng_ref[...] * gate[:, cols]).astype(BF16)


def _gdread(o_f, o_b, p, z_block, norm_g):
    m, w = o_f.shape
    tm = _tile(m, 512)
    return pl.pallas_call(
        _gdread_kernel,
        grid=(m // tm,),
        in_specs=[pl.BlockSpec((tm, w), lambda i: (i, 0)),
                  pl.BlockSpec((tm, w), lambda i: (i, 0)),
                  pl.BlockSpec((tm, w), lambda i: (i, z_block)),
                  pl.BlockSpec((1, GD_HEAD), lambda i: (0, 0))],
        out_specs=pl.BlockSpec((tm, w), lambda i: (i, 0)),
        out_shape=jax.ShapeDtypeStruct((m, w), BF16),
        compiler_params=_params(("parallel",)),
        name="gdread",
    )(o_f, o_b, p, norm_g)


def _merge1_kernel(ya_ref, yb_ref, wa_ref, wb_ref, ga_ref, gb_ref, o_ref):
    a = jnp.dot(ya_ref[...], wa_ref[...], preferred_element_type=F32)
    b = jnp.dot(yb_ref[...], wb_ref[...], preferred_element_type=F32)
    o_ref[...] = (_sigmoid(ga_ref[...]) * a + _sigmoid(gb_ref[...]) * b).astype(BF16)


def _merge1(ya, yb, wa, wb, p, gate_col0):
    m, ka = ya.shape
    kb = yb.shape[1]
    d = wa.shape[1]
    tm = _tile(m, 1024)
    tn = _tile(d, 1024, LANES)
    ga0 = gate_col0 // tn
    gb0 = (gate_col0 + d) // tn
    return pl.pallas_call(
        _merge1_kernel,
        grid=(m // tm, d // tn),
        in_specs=[pl.BlockSpec((tm, ka), lambda i, j: (i, 0)),
                  pl.BlockSpec((tm, kb), lambda i, j: (i, 0)),
                  pl.BlockSpec((ka, tn), lambda i, j: (0, j)),
                  pl.BlockSpec((kb, tn), lambda i, j: (0, j)),
                  pl.BlockSpec((tm, tn), lambda i, j: (i, ga0 + j)),
                  pl.BlockSpec((tm, tn), lambda i, j: (i, gb0 + j))],
        out_specs=pl.BlockSpec((tm, tn), lambda i, j: (i, j)),
        out_shape=jax.ShapeDtypeStruct((m, d), BF16),
        compiler_params=_params(("parallel", "arbitrary")),
        name="merge1",
    )(ya, yb, wa, wb, p, p)


def _merge2_kernel(mg_ref, wo_ref, x_ref, mod_ref, o_ref, *, gate_row):
    y = jnp.dot(mg_ref[...], wo_ref[...], preferred_element_type=F32)
    o_ref[...] = x_ref[...] + mod_ref[0, gate_row:gate_row + 1, :] * y


def _merge2(merged, wo, x2, mod, rows_per_mod, gate_row):
    m, d = x2.shape
    tm = _tile(rows_per_mod, 1024)
    tn = _tile(d, 1024, LANES)
    per = rows_per_mod // tm
    return pl.pallas_call(
        functools.partial(_merge2_kernel, gate_row=gate_row),
        grid=(m // tm, d // tn),
        in_specs=[pl.BlockSpec((tm, d), lambda i, j: (i, 0)),
                  pl.BlockSpec((d, tn), lambda i, j: (0, j)),
                  pl.BlockSpec((tm, tn), lambda i, j: (i, j)),
                  pl.BlockSpec((1, 8, tn), lambda i, j: (i // per, 0, j))],
        out_specs=pl.BlockSpec((tm, tn), lambda i, j: (i, j)),
        out_shape=jax.ShapeDtypeStruct((m, d), F32),
        compiler_params=_params(("parallel", "arbitrary")),
        name="merge2",
    )(merged, wo, x2, mod)


def _convglu_kernel(gate_ref, val_ref, cw_ref, o_ref):
    g = gate_ref[...]
    n = g.shape[0]
    t = lax.broadcasted_iota(jnp.int32, g.shape, 0)
    col = t % GRID_W
    row = t // GRID_W
    n_rows = n // GRID_W
    left = jnp.where(col > 0, pltpu.roll(g, 1, 0), 0.0)
    right = jnp.where(col < GRID_W - 1, pltpu.roll(g, n - 1, 0), 0.0)
    acc = jnp.zeros_like(g)
    for dr in (-1, 0, 1):
        line = (left * cw_ref[3 * (dr + 1):3 * (dr + 1) + 1, :]
                + g * cw_ref[3 * (dr + 1) + 1:3 * (dr + 1) + 2, :]
                + right * cw_ref[3 * (dr + 1) + 2:3 * (dr + 1) + 3, :])
        if dr == -1:
            line = jnp.where(row > 0, pltpu.roll(line, GRID_W, 0), 0.0)
        elif dr == 1:
            line = jnp.where(row < n_rows - 1, pltpu.roll(line, n - GRID_W, 0), 0.0)
        acc = acc + line
    gelu = 0.5 * acc * (1.0 + lax.erf(acc * (2.0 ** -0.5)))
    o_ref[...] = (gelu * val_ref[...]).astype(BF16)


def _convglu(gv, conv_w, batch):
    m = gv.shape[0]
    dff = conv_w.shape[1]
    seq = m // batch
    tc = _tile(dff, 512, LANES)
    nblk = dff // tc
    return pl.pallas_call(
        _convglu_kernel,
        grid=(batch, nblk),
        in_specs=[pl.BlockSpec((seq, tc), lambda b, j: (b, j)),
                  pl.BlockSpec((seq, tc), lambda b, j: (b, nblk + j)),
                  pl.BlockSpec((conv_w.shape[0], tc), lambda b, j: (0, j))],
        out_specs=pl.BlockSpec((seq, tc), lambda b, j: (b, j)),
        out_shape=jax.ShapeDtypeStruct((m, dff), BF16),
        compiler_params=_params(("parallel", "arbitrary")),
        name="convglu",
    )(gv, gv, conv_w)


def _ffn_out_kernel(act_ref, w2_ref, x_ref, mod_ref, g_ref, o_ref, acc_ref, *, gate_row):
    kstep = pl.program_id(1)

    @pl.when(kstep == 0)
    def _():
        acc_ref[...] = jnp.zeros_like(acc_ref)

    acc_ref[...] += jnp.dot(act_ref[...], w2_ref[...], preferred_element_type=F32)

    @pl.when(kstep == pl.num_programs(1) - 1)
    def _():
        y = x_ref[...] + mod_ref[0, gate_row:gate_row + 1, :] * acc_ref[...]
        ms = jnp.mean(y * y, axis=-1, keepdims=True)
        o_ref[...] = y * lax.rsqrt(ms + NORM_EPS) * g_ref[...]


def _ffn_out(act, w2, x1, mod, final_g, rows_per_mod, gate_row):
    m, d = x1.shape
    dff = act.shape[1]
    tm = _tile(rows_per_mod, 512)
    tk = _tile(dff, 1408, LANES)
    per = rows_per_mod // tm
    return pl.pallas_call(
        functools.partial(_ffn_out_kernel, gate_row=gate_row),
        grid=(m // tm, dff // tk),
        in_specs=[pl.BlockSpec((tm, tk), lambda i, k: (i, k)),
                  pl.BlockSpec((tk, d), lambda i, k: (k, 0)),
                  pl.BlockSpec((tm, d), lambda i, k: (i, 0)),
                  pl.BlockSpec((1, 8, d), lambda i, k: (i // per, 0, 0)),
                  pl.BlockSpec((1, d), lambda i, k: (0, 0))],
        out_specs=pl.BlockSpec((tm, d), lambda i, k: (i, 0)),
        out_shape=jax.ShapeDtypeStruct((m, d), F32),
        scratch_shapes=[pltpu.VMEM((tm, d), F32)],
        compiler_params=_params(("parallel", "arbitrary")),
        name="ffn_out",
    )(act, w2, x1, mod, final_g)


def _pad_cols(a, width):
    return jnp.pad(a, [(0, 0)] * (a.ndim - 1) + [(0, width - a.shape[-1])])


def _pad_rank(a):
    return jnp.pad(a, [(0, 0)] * (a.ndim - 2) + [(0, RANK_PAD - a.shape[-2]), (0, 0)])


def kernel(x, c, ctx, c_ctx, w_ada, b_ada, norm1_g, norm2_g, w_in, rw_mu, rw_k_k, rw_k_a, rw_r_k, rw_w0, rw_w_up, rw_a0, rw_a_up, rw_g_up, rw_gn_g, rw_gn_b, gd_conv_w, gd_a_log, gd_dt_bias, gd_norm_g, w_a_out, w_b_out, w_o, ffn_w1, ffn_conv_w, ffn_w2, final_norm_g):
    batch, seq, d = x.shape
    ctx_len = ctx.shape[1]
    assert w_ada.shape[0] == 1, "single layer only"
    rw_w = rw_k_k.shape[1]
    gd_w = w_b_out.shape[1]
    dec_rank = rw_w_up.shape[2]
    icl_rank = rw_a_up.shape[2]
    gate_rank = rw_g_up.shape[1]
    gd_heads = gd_a_log.shape[2]
    assert max(dec_rank, icl_rank) <= RANK_PAD and 4 * gd_heads <= 2 * LANES
    assert seq % CHUNK == 0 and ctx_len % CHUNK == 0 and seq % GRID_W == 0
    assert 3 * rw_w == 3 * gd_w and gate_rank <= 2 * LANES

    low_w = 8 * LANES
    blk0 = 3 * rw_w + low_w
    assert blk0 == 4 * gd_w
    wi = w_in[0]
    o_rw = 3 * rw_w
    o_gd = o_rw + 2 * dec_rank + 2 * icl_rank + gate_rank
    o_ab = o_gd + 4 * gd_w
    o_gate = o_ab + 4 * gd_heads

    def pack_cols(a):
        pieces = [a[..., :o_rw]]
        off = o_rw
        for r in (dec_rank, dec_rank, icl_rank, icl_rank):
            pieces.append(_pad_cols(a[..., off:off + r], RANK_PAD))
            off += r
        pieces.append(_pad_cols(a[..., off:off + gate_rank], 2 * LANES))
        return pieces

    w_pack = jnp.concatenate(
        pack_cols(wi) + [_pad_cols(wi[:, o_ab:o_gate], 2 * LANES), wi[:, o_gd:o_ab], wi[:, o_gate:]],
        axis=1).astype(BF16)
    n_ctx_cols = 2 * blk0
    gate_col0 = 2 * blk0
    mu_pack = jnp.concatenate(pack_cols(rw_mu) + [jnp.zeros((1, 2 * LANES), F32)], axis=1)

    cc = jnp.concatenate([c, c_ctx[None, :], jnp.zeros((16 - batch - 1, d), F32)], axis=0)
    mods = _mod(cc, w_ada[0], b_ada)
    mod_lat = _pad_rows8(mods[:batch].reshape(batch, 6, d))
    mod_ctx = _pad_rows8(mods[batch:batch + 1].reshape(1, 6, d))

    x2 = x.reshape(batch * seq, d)
    ctx2 = ctx.reshape(batch * ctx_len, d)
    p_lat = _normproj(x2, mod_lat, norm1_g, w_pack, seq, w_pack.shape[1], 0, 1, "inproj_lat")
    p_ctx = _normproj(ctx2, mod_ctx, norm1_g, w_pack, batch * ctx_len, n_ctx_cols, 0, 1, "inproj_ctx")

    rw_wts = (mu_pack, rw_k_k, rw_k_a, rw_r_k, rw_w0[0], _pad_rank(rw_w_up[0]).astype(BF16),
              rw_a0[0], _pad_rank(rw_a_up[0]).astype(BF16),
              jnp.pad(rw_g_up[0], ((0, 2 * LANES - gate_rank), (0, 0))).astype(BF16))
    n_pairs = rw_w // LANES
    s0 = jnp.zeros((batch, 2, n_pairs, LANES, LANES), F32)
    f_ctx = _rwprep(p_ctx, ctx_len, rw_wts)
    f_lat = _rwprep(p_lat, seq, rw_wts)
    (s_ctx,) = _rwscan(*f_ctx[:6], s0, batch, want_out=False)
    o_rw_f, o_rw_b, _ = _rwscan(*f_lat[:6], s_ctx, batch, want_out=True)
    ya = _rwread(o_rw_f, o_rw_b, f_lat[6], f_lat[7], rw_gn_g, rw_gn_b)

    s0g = jnp.zeros((batch, 2, gd_heads, GD_HEAD, GD_HEAD), F32)
    a_log2 = gd_a_log[0].reshape(1, 2 * gd_heads)
    dtb2 = gd_dt_bias[0].reshape(1, 2 * gd_heads)

    def gd_feats(p, seq_len):
        q, k, v, gcum, beta = _gdprep(p, seq_len, gd_conv_w[0], a_log2, dtb2, 1, 3, gd_w)
        m = p.shape[0]
        g3 = gcum.reshape(m, 2, gd_heads).transpose(1, 0, 2)
        gt = g3.reshape(2, m // CHUNK, CHUNK, gd_heads).transpose(0, 1, 3, 2)
        gt = gt.reshape(2, m // CHUNK, gd_heads // 2, 2 * CHUNK)
        b3 = beta.reshape(m, 2, gd_heads).transpose(1, 0, 2)
        return q, k, v, g3, gt, b3

    (sg_ctx,) = _gdscan(*gd_feats(p_ctx, ctx_len), s0g, batch, want_out=False)
    o_gd_f, o_gd_b, _ = _gdscan(*gd_feats(p_lat, seq), sg_ctx, batch, want_out=True)
    yb = _gdread(o_gd_f, o_gd_b, p_lat, 2 * blk0 // gd_w - 1, gd_norm_g)

    merged = _merge1(ya, yb, w_a_out[0].astype(BF16), w_b_out[0].astype(BF16), p_lat, gate_col0)
    x1 = _merge2(merged, w_o[0].astype(BF16), x2, mod_lat, seq, 2)

    gv = _normproj(x1, mod_lat, norm2_g, ffn_w1[0].astype(BF16), seq, ffn_w1.shape[2], 3, 4, "ffn_in")
    act = _convglu(gv, ffn_conv_w[0].reshape(-1, ffn_conv_w.shape[-1]), batch)
    out = _ffn_out(act, ffn_w2[0].astype(BF16), x1, mod_lat, final_norm_g[None, :], seq, 5)
    return out.reshape(batch, seq, d)


def _pad_rows8(a):
    return jnp.pad(a, ((0, 0), (0, 8 - a.shape[1]), (0, 0)))
```

```python
import functools

import jax
import jax.numpy as jnp
from jax import lax
from jax.experimental import pallas as pl
from jax.experimental.pallas import tpu as pltpu

F32 = jnp.float32
BF16 = jnp.bfloat16
HIGHEST = lax.Precision.HIGHEST

NORM_EPS = 1e-6
RW_GN_EPS = 64e-5
RW_HEAD = 64
GD_HEAD = 128
LANES = 128
CHUNK = 64
GRID_W = 64
RANK_PAD = 128
VMEM_LIMIT = 56 * 1024 * 1024


def _params(sem):
    return pltpu.CompilerParams(dimension_semantics=sem, vmem_limit_bytes=VMEM_LIMIT)


def _tile(n, pref, mult=8):
    if n <= pref:
        return n
    t = (pref // mult) * mult
    while t >= mult:
        if n % t == 0:
            return t
        t -= mult
    return n


def _mm(a, b):
    return jnp.dot(a.astype(BF16), b.astype(BF16), preferred_element_type=F32)


def _mm_nt(a, b):
    return lax.dot_general(a.astype(BF16), b.astype(BF16), (((1,), (1,)), ((), ())),
                           preferred_element_type=F32)


def _mm_tn(a, b):
    return lax.dot_general(a.astype(BF16), b.astype(BF16), (((0,), (0,)), ((), ())),
                           preferred_element_type=F32)


def _mm_hi(a, b):
    return jnp.dot(a, b, precision=HIGHEST, preferred_element_type=F32)


def _split3(x):
    x1 = x.astype(BF16)
    r1 = x - x1.astype(F32)
    x2 = r1.astype(BF16)
    x3 = (r1 - x2.astype(F32)).astype(BF16)
    return x1, x2, x3


def _mm_sel_left(c, x):
    cb = c.astype(BF16)
    return jnp.dot(jnp.concatenate([cb, cb, cb], axis=1), jnp.concatenate(_split3(x), axis=0),
                   preferred_element_type=F32)


def _mm_sel_right(x, c):
    cb = c.astype(BF16)
    return jnp.dot(jnp.concatenate(_split3(x), axis=1), jnp.concatenate([cb, cb, cb], axis=0),
                   preferred_element_type=F32)


def _softplus(x):
    return jnp.maximum(x, 0.0) + jnp.log(1.0 + jnp.exp(-jnp.abs(x)))


def _sigmoid(x):
    return 1.0 / (1.0 + jnp.exp(-x))


def _seg_ones(width):
    i = lax.broadcasted_iota(jnp.int32, (LANES, LANES), 0) // width
    j = lax.broadcasted_iota(jnp.int32, (LANES, LANES), 1) // width
    return (i == j).astype(F32)


def _seg_sum(x, width):
    e = _seg_ones(width)
    n = x.shape[-1] // LANES
    parts = [_mm_sel_right(x[:, g * LANES:(g + 1) * LANES], e) for g in range(n)]
    return parts[0] if n == 1 else jnp.concatenate(parts, axis=-1)


def _tri(rev):
    i = lax.broadcasted_iota(jnp.int32, (CHUNK, CHUNK), 0)
    j = lax.broadcasted_iota(jnp.int32, (CHUNK, CHUNK), 1)
    return ((j >= i) if rev else (j <= i)).astype(F32)


def _level_masks(i, j):
    masks = [(i // 2) == (j // 2)]
    s = 2
    while s < CHUNK:
        masks.append(((i // (2 * s)) == (j // (2 * s))) & ((i // s) != (j // s)))
        s *= 2
    return masks


def _shift_rows(x, prev_row, next_row):
    n = x.shape[0]
    row = lax.broadcasted_iota(jnp.int32, x.shape, 0)
    xm1 = jnp.where(row == 0, prev_row, pltpu.roll(x, 1, 0))
    xp1 = jnp.where(row == n - 1, next_row, pltpu.roll(x, n - 1, 0))
    return xm1, xp1


def _mod_kernel(c_ref, w_ref, b_ref, o_ref):
    c = c_ref[...]
    s = c * _sigmoid(c)
    o_ref[...] = _mm_hi(s, w_ref[...]) + b_ref[...]


def _mod(cc, w_ada, b_ada):
    rows, d = cc.shape
    n = w_ada.shape[1]
    tn = _tile(n, 1024, LANES)
    return pl.pallas_call(
        _mod_kernel,
        grid=(n // tn,),
        in_specs=[pl.BlockSpec((rows, d), lambda j: (0, 0)),
                  pl.BlockSpec((d, tn), lambda j: (0, j)),
                  pl.BlockSpec((1, tn), lambda j: (0, j))],
        out_specs=pl.BlockSpec((rows, tn), lambda j: (0, j)),
        out_shape=jax.ShapeDtypeStruct((rows, n), F32),
        compiler_params=_params(("arbitrary",)),
        name="mod",
    )(cc, w_ada, b_ada)


def _normproj_kernel(x_ref, mod_ref, g_ref, w_ref, o_ref, h_ref, *, sh_row, sc_row):
    @pl.when(pl.program_id(1) == 0)
    def _():
        x = x_ref[...]
        ms = jnp.mean(x * x, axis=-1, keepdims=True)
        y = x * lax.rsqrt(ms + NORM_EPS) * g_ref[...]
        sh = mod_ref[0, sh_row:sh_row + 1, :]
        sc = mod_ref[0, sc_row:sc_row + 1, :]
        h_ref[...] = (y * (1.0 + sc) + sh).astype(BF16)

    o_ref[...] = jnp.dot(h_ref[...], w_ref[...], preferred_element_type=F32)


def _normproj(x2, mod, gain, w, rows_per_mod, n_cols, sh_row, sc_row, name):
    m, d = x2.shape
    tm = _tile(rows_per_mod, 1024)
    tn = _tile(n_cols, 1024, LANES)
    per = rows_per_mod // tm
    return pl.pallas_call(
        functools.partial(_normproj_kernel, sh_row=sh_row, sc_row=sc_row),
        grid=(m // tm, n_cols // tn),
        in_specs=[pl.BlockSpec((tm, d), lambda i, j: (i, 0)),
                  pl.BlockSpec((1, 8, d), lambda i, j: (i // per, 0, 0)),
                  pl.BlockSpec((1, d), lambda i, j: (0, 0)),
                  pl.BlockSpec((d, tn), lambda i, j: (0, j))],
        out_specs=pl.BlockSpec((tm, tn), lambda i, j: (i, j)),
        out_shape=jax.ShapeDtypeStruct((m, n_cols), F32),
        scratch_shapes=[pltpu.VMEM((tm, d), BF16)],
        compiler_params=_params(("parallel", "arbitrary")),
        name=name,
    )(x2, mod, gain, w)


def _rwprep_kernel(pc_ref, pp_ref, pn_ref, mu_ref, kk_ref, ka_ref, rk_ref, w0_ref, wup_ref,
                   a0_ref, aup_ref, gup_ref,
                   at_ref, bt_ref, kt_ref, rt_ref, v_ref, pt_ref, bonus_ref, g_ref, *, tiles_per_seq):
    i = pl.program_id(0)
    tm = pc_ref.shape[0]
    w = RW_HEAD * (kk_ref.shape[1] // RW_HEAD)
    first = (i % tiles_per_seq) == 0
    last = (i % tiles_per_seq) == tiles_per_seq - 1
    x = pc_ref[...]
    prev_row = jnp.where(first, 0.0, pp_ref[7:8, :])
    next_row = jnp.where(last, 0.0, pn_ref[0:1, :])
    xm1, xp1 = _shift_rows(x, prev_row, next_row)
    xs = x + mu_ref[...] * (0.5 * (xm1 + xp1) - x)

    r = xs[:, 0:w]
    k = xs[:, w:2 * w]
    v = xs[:, 2 * w:3 * w]
    base = 3 * w
    gd = xs[:, base + 4 * RANK_PAD: base + 4 * RANK_PAD + gup_ref.shape[0]]
    g_ref[...] = _mm(_sigmoid(gd), gup_ref[...])
    v_ref[...] = v.astype(BF16)

    kx = k * kk_ref[...]
    kk = kx * lax.rsqrt(jnp.maximum(_seg_sum(kx * kx, RW_HEAD), 1e-12))

    ksum = jnp.zeros_like(k)
    for d in range(2):
        wd = xs[:, base + d * RANK_PAD: base + (d + 1) * RANK_PAD]
        ad = xs[:, base + (2 + d) * RANK_PAD: base + (3 + d) * RANK_PAD]
        wl = w0_ref[d:d + 1, :] + _mm(jnp.tanh(wd), wup_ref[d])
        lw = -jnp.exp(-_softplus(-wl) - 0.5)
        a = _sigmoid(a0_ref[d:d + 1, :] + _mm(ad, aup_ref[d]))
        kd = k * (1.0 + (a - 1.0) * ka_ref[...])
        ksum = ksum + kd
        tri = _tri(rev=(d == 1))
        for c in range(tm // CHUNK):
            rows = slice(c * CHUNK, (c + 1) * CHUNK)
            lwc = lw[rows]
            cum = _mm_sel_left(tri, lwc)
            p_in = jnp.exp(cum)
            p_inv = jnp.exp(-cum)
            p_ex = jnp.exp(cum - lwc)
            at_ref[d, rows, :] = (-kk[rows] * p_ex).astype(BF16)
            bt_ref[d, rows, :] = (kk[rows] * a[rows] * p_inv).astype(BF16)
            kt_ref[d, rows, :] = (kd[rows] * p_inv).astype(BF16)
            rt_ref[d, rows, :] = (r[rows] * p_in).astype(BF16)
            tot = cum[CHUNK - 1:CHUNK] if d == 0 else cum[0:1]
            pt_ref[d, c, :, :] = jnp.exp(tot)
    bonus_ref[...] = _seg_sum(r * ksum * rk_ref[...], RW_HEAD) * v


def _rwprep(p, seq_len, wts):
    m = p.shape[0]
    mu, k_k, k_a, r_k, w0, w_up, a0, a_up, g_up = wts
    w = k_k.shape[1]
    blk = mu.shape[1]
    tm = _tile(seq_len, 256, CHUNK)
    tps = seq_len // tm
    nb8 = m // 8
    full = lambda a: pl.BlockSpec(a.shape, lambda i: (0,) * a.ndim)
    feat = jax.ShapeDtypeStruct((2, m, w), BF16)
    feat_spec = pl.BlockSpec((2, tm, w), lambda i: (0, i, 0))
    row_spec = pl.BlockSpec((tm, w), lambda i: (i, 0))
    return pl.pallas_call(
        functools.partial(_rwprep_kernel, tiles_per_seq=tps),
        grid=(m // tm,),
        in_specs=[pl.BlockSpec((tm, blk), lambda i: (i, 0)),
                  pl.BlockSpec((8, blk), lambda i: (jnp.maximum(i * (tm // 8) - 1, 0), 0)),
                  pl.BlockSpec((8, blk), lambda i: (jnp.minimum((i + 1) * (tm // 8), nb8 - 1), 0)),
                  full(mu), full(k_k), full(k_a), full(r_k), full(w0), full(w_up), full(a0),
                  full(a_up), full(g_up)],
        out_specs=[feat_spec, feat_spec, feat_spec, feat_spec, row_spec,
                   pl.BlockSpec((2, tm // CHUNK, 1, w), lambda i: (0, i, 0, 0)),
                   row_spec, row_spec],
        out_shape=[feat, feat, feat, feat, jax.ShapeDtypeStruct((m, w), BF16),
                   jax.ShapeDtypeStruct((2, m // CHUNK, 1, w), F32),
                   jax.ShapeDtypeStruct((m, w), F32), jax.ShapeDtypeStruct((m, w), F32)],
        compiler_params=_params(("parallel",)),
        name="rwprep",
    )(p, p, p, mu, k_k, k_a, r_k, w0, w_up, a0, a_up, g_up)


def _pair_blockdiag(x, m0):
    zero = jnp.zeros_like(x)
    return jnp.concatenate([jnp.where(m0, x, zero), jnp.where(m0, zero, x)], axis=0)


def _tri_inverse(a, eye, blk, m0, sign):
    ts = [eye + sign * jnp.where(blk[0], x, 0.0) for x in a]
    for lvl in range(1, len(blk)):
        xs = [_mm(t, _pair_blockdiag(jnp.where(blk[lvl], x, 0.0), m0)) for t, x in zip(ts, a)]
        ts = [t + sign * _mm(x, _pair_blockdiag(t, m0)) for t, x in zip(ts, xs)]
    return ts


def _pair_masks(head_cols):
    i = lax.broadcasted_iota(jnp.int32, (CHUNK, LANES), 0)
    lane = lax.broadcasted_iota(jnp.int32, (CHUNK, LANES), 1)
    j = lane % head_cols
    strict = (j < i, j > i)
    incl = (j <= i, j >= i)
    eye = (j == i).astype(F32)
    return strict, incl, eye, _level_masks(i, j), lane < head_cols


def _rwscan_kernel(*refs, want_out):
    ins, rest = refs[:12], refs[12:]
    s0_ref = rest[0]
    if want_out:
        o_refs, s_ref, h_ref = rest[1:3], rest[3], rest[4]
    else:
        s_ref, h_ref = rest[1], rest[2]
    c = pl.program_id(1)
    n_pairs = h_ref.shape[1]

    @pl.when(c == 0)
    def _():
        h_ref[...] = s0_ref[0]

    strict, incl, eye, blk, m0 = _pair_masks(RW_HEAD)
    r2 = lax.broadcasted_iota(jnp.int32, (LANES, LANES), 0) // RW_HEAD
    c2 = lax.broadcasted_iota(jnp.int32, (LANES, LANES), 1) // RW_HEAD
    diag2 = r2 == c2

    chains = [(d, p) for d in range(2) for p in range(n_pairs)]
    cols = lambda p: slice(p * LANES, (p + 1) * LANES)
    at = [ins[6 * d + 0][0, :, cols(p)] for d, p in chains]
    bt = [ins[6 * d + 1][0, :, cols(p)] for d, p in chains]
    kt = [ins[6 * d + 2][0, :, cols(p)] for d, p in chains]
    rt = [ins[6 * d + 3][0, :, cols(p)] for d, p in chains]
    v = [ins[6 * d + 4][:, cols(p)] for d, p in chains]
    pt = [ins[6 * d + 5][0, 0, :, cols(p)] for d, p in chains]
    n = len(chains)
    bd = lambda x: _pair_blockdiag(x, m0)

    s4 = [_mm_nt(jnp.concatenate([at[i], rt[i]], axis=0),
                 jnp.concatenate([bd(bt[i]), bd(kt[i])], axis=0)) for i in range(n)]
    a_ab = [jnp.where(strict[chains[i][0]], s4[i][:CHUNK, :LANES], 0.0) for i in range(n)]
    a_ak = [jnp.where(strict[chains[i][0]], s4[i][:CHUNK, LANES:], 0.0) for i in range(n)]
    t = _tri_inverse(a_ab, eye, blk, m0, 1.0)
    av = [_mm(a_ak[i], bd(v[i])) for i in range(n)]
    wu = [_mm(t[i], jnp.concatenate([bd(at[i]), bd(av[i].astype(BF16))], axis=1)) for i in range(n)]

    ht = [h_ref[d, p] for d, p in chains]
    if want_out:
        m1 = [_mm_nt(jnp.concatenate([wu[i][:, :LANES].astype(BF16), rt[i]], axis=0), ht[i])
              for i in range(n)]
        u = [m1[i][:CHUNK] + wu[i][:, LANES:] for i in range(n)]
    else:
        u = [_mm_nt(wu[i][:, :LANES], ht[i]) + wu[i][:, LANES:] for i in range(n)]
    ub = [x.astype(BF16) for x in u]
    if want_out:
        for i, (d, p) in enumerate(chains):
            m_r = jnp.where(jnp.concatenate([incl[d], incl[d]], axis=1), s4[i][CHUNK:], 0.0)
            o = m1[i][CHUNK:] + _mm(m_r, jnp.concatenate([bd(ub[i]), bd(v[i])], axis=0))
            o_refs[d][:, cols(p)] = o
    for i, (d, p) in enumerate(chains):
        upd = _mm_tn(jnp.concatenate([ub[i], v[i]], axis=0),
                     jnp.concatenate([bt[i], kt[i]], axis=0))
        h_ref[d, p] = (ht[i] + jnp.where(diag2, upd, 0.0)) * pt[i]

    @pl.when(c == pl.num_programs(1) - 1)
    def _():
        s_ref[0] = h_ref[...]


def _rwscan(at, bt, kt, rt, v, pt, s0, batch, want_out):
    m, w = v.shape
    n_chunks = m // batch // CHUNK
    n_pairs = w // LANES
    rows = (lambda b, c: b * n_chunks + c, lambda b, c: b * n_chunks + n_chunks - 1 - c)

    in_specs, args = [], []
    for d in range(2):
        feat_spec = pl.BlockSpec((1, CHUNK, w), lambda b, c, d=d: (d, rows[d](b, c), 0))
        in_specs += [feat_spec] * 4
        in_specs += [pl.BlockSpec((CHUNK, w), lambda b, c, d=d: (rows[d](b, c), 0)),
                     pl.BlockSpec((1, 1, 1, w), lambda b, c, d=d: (d, rows[d](b, c), 0, 0))]
        args += [at, bt, kt, rt, v, pt]
    state_spec = pl.BlockSpec((1, 2, n_pairs, LANES, LANES), lambda b, c: (b, 0, 0, 0, 0))
    out_specs = [state_spec]
    out_shape = [jax.ShapeDtypeStruct(s0.shape, F32)]
    if want_out:
        out_specs = [pl.BlockSpec((CHUNK, w), lambda b, c, d=d: (rows[d](b, c), 0))
                     for d in range(2)] + out_specs
        out_shape = [jax.ShapeDtypeStruct((m, w), F32)] * 2 + out_shape
    return pl.pallas_call(
        functools.partial(_rwscan_kernel, want_out=want_out),
        grid=(batch, n_chunks),
        in_specs=in_specs + [state_spec],
        out_specs=out_specs,
        out_shape=out_shape,
        scratch_shapes=[pltpu.VMEM((2, n_pairs, LANES, LANES), F32)],
        compiler_params=_params(("parallel", "arbitrary")),
        name="rwscan_out" if want_out else "rwscan_state",
    )(*args, s0)


def _rwread_kernel(of_ref, ob_ref, bonus_ref, g_ref, gng_ref, gnb_ref, y_ref):
    o = of_ref[...] + ob_ref[...]
    inv = 1.0 / RW_HEAD
    mean = _seg_sum(o, RW_HEAD) * inv
    cen = o - mean
    var = _seg_sum(cen * cen, RW_HEAD) * inv
    on = cen * lax.rsqrt(var + RW_GN_EPS) * gng_ref[...] + gnb_ref[...]
    y_ref[...] = ((on + bonus_ref[...]) * g_ref[...]).astype(BF16)


def _rwread(o_f, o_b, bonus, g, gn_g, gn_b):
    m, w = o_f.shape
    tm = _tile(m, 512)
    row_spec = pl.BlockSpec((tm, w), lambda i: (i, 0))
    vec_spec = pl.BlockSpec((1, w), lambda i: (0, 0))
    return pl.pallas_call(
        _rwread_kernel,
        grid=(m // tm,),
        in_specs=[row_spec, row_spec, row_spec, row_spec, vec_spec, vec_spec],
        out_specs=row_spec,
        out_shape=jax.ShapeDtypeStruct((m, w), BF16),
        compiler_params=_params(("parallel",)),
        name="rwread",
    )(o_f, o_b, bonus, g, gn_g, gn_b)


def _gdprep_kernel(pc_ref, pp_ref, pn_ref, ab_ref, cw_ref, alog_ref, dtb_ref,
                   q_ref, k_ref, v_ref, gcum_ref, beta_ref, *, tiles_per_seq):
    i = pl.program_id(0)
    tm = pc_ref.shape[0]
    w = q_ref.shape[1]
    n_heads = w // GD_HEAD
    first = (i % tiles_per_seq) == 0
    last = (i % tiles_per_seq) == tiles_per_seq - 1
    x = pc_ref[:, 0:3 * w]
    prev_row = jnp.where(first, 0.0, pp_ref[7:8, 0:3 * w])
    next_row = jnp.where(last, 0.0, pn_ref[0:1, 0:3 * w])
    xm1, xp1 = _shift_rows(x, prev_row, next_row)
    y = xm1 * cw_ref[0:1, :] + x * cw_ref[1:2, :] + xp1 * cw_ref[2:3, :]
    y = y * _sigmoid(y)
    for h in range(n_heads):
        for part, ref, scale in ((0, q_ref, GD_HEAD ** -0.5), (1, k_ref, 1.0)):
            cols = slice(part * w + h * GD_HEAD, part * w + (h + 1) * GD_HEAD)
            t = y[:, cols]
            ss = jnp.sum(t * t, axis=-1, keepdims=True)
            ref[:, h * GD_HEAD:(h + 1) * GD_HEAD] = (
                t * (lax.rsqrt(jnp.maximum(ss, 1e-12)) * scale)).astype(BF16)
    v_ref[...] = y[:, 2 * w:3 * w].astype(BF16)

    ab = ab_ref[:, 6 * LANES:7 * LANES]
    a = ab[:, 0:2 * n_heads]
    b = ab[:, 2 * n_heads:4 * n_heads]
    glog = -jnp.exp(alog_ref[...]) * _softplus(a + dtb_ref[...])
    beta_ref[...] = _sigmoid(b)
    for c in range(tm // CHUNK):
        rows = slice(c * CHUNK, (c + 1) * CHUNK)
        gc = glog[rows]
        fwd = _mm_sel_left(_tri(False), gc)
        bwd = _mm_sel_left(_tri(True), gc)
        col = lax.broadcasted_iota(jnp.int32, gc.shape, 1)
        gcum_ref[rows, :] = jnp.where(col < n_heads, fwd, bwd)


def _gdprep(p, seq_len, conv_w, a_log, dt_bias, qkvz_block, ab_block, width):
    m = p.shape[0]
    blk = 4 * width
    tm = _tile(seq_len, 256, CHUNK)
    tps = seq_len // tm
    nb8 = m // 8
    n2h = a_log.shape[1]
    full = lambda a: pl.BlockSpec(a.shape, lambda i: (0,) * a.ndim)
    row_spec = pl.BlockSpec((tm, width), lambda i: (i, 0))
    small_spec = pl.BlockSpec((tm, n2h), lambda i: (i, 0))
    return pl.pallas_call(
        functools.partial(_gdprep_kernel, tiles_per_seq=tps),
        grid=(m // tm,),
        in_specs=[pl.BlockSpec((tm, blk), lambda i: (i, qkvz_block)),
                  pl.BlockSpec((8, blk), lambda i: (jnp.maximum(i * (tm // 8) - 1, 0), qkvz_block)),
                  pl.BlockSpec((8, blk),
                               lambda i: (jnp.minimum((i + 1) * (tm // 8), nb8 - 1), qkvz_block)),
                  pl.BlockSpec((tm, 8 * LANES), lambda i: (i, ab_block)),
                  full(conv_w), full(a_log), full(dt_bias)],
        out_specs=[row_spec, row_spec, row_spec, small_spec, small_spec],
        out_shape=[jax.ShapeDtypeStruct((m, width), BF16)] * 3
        + [jax.ShapeDtypeStruct((m, n2h), F32)] * 2,
        compiler_params=_params(("parallel",)),
        name="gdprep",
    )(p, p, p, p, conv_w, a_log, dt_bias)


def _gdscan_kernel(*refs, want_out):
    ins, rest = refs[:12], refs[12:]
    s0_ref = rest[0]
    if want_out:
        o_refs, s_ref, st_ref = rest[1:3], rest[3], rest[4]
    else:
        s_ref, st_ref = rest[1], rest[2]
    c = pl.program_id(1)
    n_pairs = st_ref.shape[1] // 2

    @pl.when(c == 0)
    def _():
        st_ref[...] = s0_ref[0]

    strict, incl, eye, blk, m0 = _pair_masks(CHUNK)
    bd = lambda x: _pair_blockdiag(x, m0)
    chains = [(d, p) for d in range(2) for p in range(n_pairs)]
    n = len(chains)
    hcols = lambda h: slice(h * GD_HEAD, (h + 1) * GD_HEAD)

    def head_vals(d, p, e):
        h = 2 * p + e
        q_ref, k_ref, v_ref, g_ref, _, beta_ref = ins[6 * d:6 * d + 6]
        return (q_ref[:, hcols(h)].astype(F32), k_ref[:, hcols(h)].astype(F32),
                v_ref[:, hcols(h)].astype(F32), g_ref[0, :, h:h + 1], beta_ref[0, :, h:h + 1])

    hv = [[head_vals(d, p, e) for e in range(2)] for d, p in chains]
    kb = [[hv[i][e][1] * hv[i][e][4] for e in range(2)] for i in range(n)]
    eg = [[jnp.exp(hv[i][e][3]) for e in range(2)] for i in range(n)]
    decay, s2 = [], []
    for i, (d, p) in enumerate(chains):
        gcol = jnp.where(m0, hv[i][0][3], hv[i][1][3])
        grow = ins[6 * d + 4][0, 0, p:p + 1, :]
        decay.append(jnp.where(incl[d], jnp.exp(jnp.where(incl[d], gcol - grow, 0.0)), 0.0))
        k0, k1 = hv[i][0][1], hv[i][1][1]
        zero = jnp.zeros_like(k0)
        lhs = jnp.concatenate([jnp.concatenate([kb[i][0], kb[i][1]], axis=1),
                               jnp.concatenate([hv[i][0][0], hv[i][1][0]], axis=1)], axis=0)
        rhs = jnp.concatenate([jnp.concatenate([k0, zero], axis=1),
                               jnp.concatenate([zero, k1], axis=1)], axis=0)
        s2.append(_mm_nt(lhs, rhs))
    a = [jnp.where(strict[chains[i][0]], s2[i][:CHUNK] * decay[i], 0.0) for i in range(n)]
    t = _tri_inverse(a, eye, blk, m0, -1.0)
    sol = [_mm(bd(t[i]), jnp.concatenate(
        [jnp.concatenate([hv[i][e][2] * hv[i][e][4], kb[i][e] * eg[i][e]], axis=1)
         for e in range(2)], axis=0)) for i in range(n)]

    st = [[st_ref[d, 2 * p + e] for e in range(2)] for d, p in chains]
    ws = [[_mm(jnp.concatenate([sol[i][e * CHUNK:(e + 1) * CHUNK, GD_HEAD:],
                                hv[i][e][0] * eg[i][e]], axis=0), st[i][e])
           for e in range(2)] for i in range(n)]
    v_new = [[sol[i][e * CHUNK:(e + 1) * CHUNK, :GD_HEAD] - ws[i][e][:CHUNK] for e in range(2)]
             for i in range(n)]
    if want_out:
        for i, (d, p) in enumerate(chains):
            intra = _mm(bd(s2[i][CHUNK:] * decay[i]),
                        jnp.concatenate([v_new[i][0], v_new[i][1]], axis=0))
            for e in range(2):
                o_refs[d][:, hcols(2 * p + e)] = (ws[i][e][CHUNK:]
                                                  + intra[e * CHUNK:(e + 1) * CHUNK])
    for i, (d, p) in enumerate(chains):
        for e in range(2):
            gcol = hv[i][e][3]
            g_last = jnp.min(gcol, axis=0, keepdims=True)
            k_dec = hv[i][e][1] * jnp.exp(g_last - gcol)
            st_ref[d, 2 * p + e] = st[i][e] * jnp.exp(g_last) + _mm_tn(k_dec, v_new[i][e])

    @pl.when(c == pl.num_programs(1) - 1)
    def _():
        s_ref[0] = st_ref[...]


def _gdscan(q, k, v, g, gt, beta, s0, batch, want_out):
    m, w = q.shape
    n_chunks = m // batch // CHUNK
    n_heads = w // GD_HEAD
    rows = (lambda b, c: b * n_chunks + c, lambda b, c: b * n_chunks + n_chunks - 1 - c)

    in_specs, args = [], []
    for d in range(2):
        row_spec = pl.BlockSpec((CHUNK, w), lambda b, c, d=d: (rows[d](b, c), 0))
        col_spec = pl.BlockSpec((1, CHUNK, n_heads), lambda b, c, d=d: (d, rows[d](b, c), 0))
        in_specs += [row_spec, row_spec, row_spec, col_spec,
                     pl.BlockSpec((1, 1, n_heads // 2, LANES),
                                  lambda b, c, d=d: (d, rows[d](b, c), 0, 0)),
                     col_spec]
        args += [q, k, v, g, gt, beta]
    state_spec = pl.BlockSpec((1, 2, n_heads, GD_HEAD, GD_HEAD), lambda b, c: (b, 0, 0, 0, 0))
    out_specs = [state_spec]
    out_shape = [jax.ShapeDtypeStruct(s0.shape, F32)]
    if want_out:
        out_specs = [pl.BlockSpec((CHUNK, w), lambda b, c, d=d: (rows[d](b, c), 0))
                     for d in range(2)] + out_specs
        out_shape = [jax.ShapeDtypeStruct((m, w), F32)] * 2 + out_shape
    return pl.pallas_call(
        functools.partial(_gdscan_kernel, want_out=want_out),
        grid=(batch, n_chunks),
        in_specs=in_specs + [state_spec],
        out_specs=out_specs,
        out_shape=out_shape,
        scratch_shapes=[pltpu.VMEM((2, n_heads, GD_HEAD, GD_HEAD), F32)],
        compiler_params=_params(("parallel", "arbitrary")),
        name="gdscan_out" if want_out else "gdscan_state",
    )(*args, s0)


def _gdread_kernel(of_ref, ob_ref, z_ref, ng_ref, y_ref):
    o = of_ref[...] + ob_ref[...]
    z = z_ref[...]
    gate = z * _sigmoid(z)
    for h in range(o.shape[1] // GD_HEAD):
        cols = slice(h * GD_HEAD, (h + 1) * GD_HEAD)
        oh = o[:, cols]
        ms = jnp.mean(oh * oh, axis=-1, keepdims=True)
        y_ref[:, cols] = (oh * lax.rsqrt(ms + NORM_EPS) * ng_ref[...] * gate[:, cols]).astype(BF16)


def _gdread(o_f, o_b, p, z_block, norm_g):
    m, w = o_f.shape
    tm = _tile(m, 512)
    return pl.pallas_call(
        _gdread_kernel,
        grid=(m // tm,),
        in_specs=[pl.BlockSpec((tm, w), lambda i: (i, 0)),
                  pl.BlockSpec((tm, w), lambda i: (i, 0)),
                  pl.BlockSpec((tm, w), lambda i: (i, z_block)),
                  pl.BlockSpec((1, GD_HEAD), lambda i: (0, 0))],
        out_specs=pl.BlockSpec((tm, w), lambda i: (i, 0)),
        out_shape=jax.ShapeDtypeStruct((m, w), BF16),
        compiler_params=_params(("parallel",)),
        name="gdread",
    )(o_f, o_b, p, norm_g)


def _merge1_kernel(ya_ref, yb_ref, wa_ref, wb_ref, ga_ref, gb_ref, o_ref):
    a = jnp.dot(ya_ref[...], wa_ref[...], preferred_element_type=F32)
    b = jnp.dot(yb_ref[...], wb_ref[...], preferred_element_type=F32)
    o_ref[...] = (_sigmoid(ga_ref[...]) * a + _sigmoid(gb_ref[...]) * b).astype(BF16)


def _merge1(ya, yb, wa, wb, p, gate_col0):
    m, ka = ya.shape
    kb = yb.shape[1]
    d = wa.shape[1]
    tm = _tile(m, 1024)
    tn = _tile(d, 1024, LANES)
    ga0 = gate_col0 // tn
    gb0 = (gate_col0 + d) // tn
    return pl.pallas_call(
        _merge1_kernel,
        grid=(m // tm, d // tn),
        in_specs=[pl.BlockSpec((tm, ka), lambda i, j: (i, 0)),
                  pl.BlockSpec((tm, kb), lambda i, j: (i, 0)),
                  pl.BlockSpec((ka, tn), lambda i, j: (0, j)),
                  pl.BlockSpec((kb, tn), lambda i, j: (0, j)),
                  pl.BlockSpec((tm, tn), lambda i, j: (i, ga0 + j)),
                  pl.BlockSpec((tm, tn), lambda i, j: (i, gb0 + j))],
        out_specs=pl.BlockSpec((tm, tn), lambda i, j: (i, j)),
        out_shape=jax.ShapeDtypeStruct((m, d), BF16),
        compiler_params=_params(("parallel", "arbitrary")),
        name="merge1",
    )(ya, yb, wa, wb, p, p)


def _merge2_kernel(mg_ref, wo_ref, x_ref, mod_ref, o_ref, *, gate_row):
    y = jnp.dot(mg_ref[...], wo_ref[...], preferred_element_type=F32)
    o_ref[...] = x_ref[...] + mod_ref[0, gate_row:gate_row + 1, :] * y


def _merge2(merged, wo, x2, mod, rows_per_mod, gate_row):
    m, d = x2.shape
    tm = _tile(rows_per_mod, 1024)
    tn = _tile(d, 1024, LANES)
    per = rows_per_mod // tm
    return pl.pallas_call(
        functools.partial(_merge2_kernel, gate_row=gate_row),
        grid=(m // tm, d // tn),
        in_specs=[pl.BlockSpec((tm, d), lambda i, j: (i, 0)),
                  pl.BlockSpec((d, tn), lambda i, j: (0, j)),
                  pl.BlockSpec((tm, tn), lambda i, j: (i, j)),
                  pl.BlockSpec((1, 8, tn), lambda i, j: (i // per, 0, j))],
        out_specs=pl.BlockSpec((tm, tn), lambda i, j: (i, j)),
        out_shape=jax.ShapeDtypeStruct((m, d), F32),
        compiler_params=_params(("parallel", "arbitrary")),
        name="merge2",
    )(merged, wo, x2, mod)


def _conv3x3_gelu(g, cw_ref):
    n = g.shape[0]
    col = lax.broadcasted_iota(jnp.int32, g.shape, 0) % GRID_W
    left = jnp.where(col > 0, pltpu.roll(g, 1, 0), 0.0)
    right = jnp.where(col < GRID_W - 1, pltpu.roll(g, n - 1, 0), 0.0)
    lines = [left * cw_ref[3 * kh:3 * kh + 1, :] + g * cw_ref[3 * kh + 1:3 * kh + 2, :]
             + right * cw_ref[3 * kh + 2:3 * kh + 3, :] for kh in range(3)]
    pad = jnp.zeros((GRID_W, g.shape[1]), F32)
    acc = (lines[1] + jnp.concatenate([pad, lines[0][:n - GRID_W]], axis=0)
           + jnp.concatenate([lines[2][GRID_W:], pad], axis=0))
    return 0.5 * acc * (1.0 + lax.erf(acc * (2.0 ** -0.5)))


def _ffn_act_kernel(x_ref, mod_ref, g_ref, w1g_ref, w1v_ref, cw_ref, o_ref, h_ref, gate_ref, val_ref,
                    *, sh_row, sc_row, prologue_rows):
    j = pl.program_id(1)

    @pl.when(j == 0)
    def _():
        sh = mod_ref[0, sh_row:sh_row + 1, :]
        sc = mod_ref[0, sc_row:sc_row + 1, :]

        def body(r, carry):
            rows = pl.ds(pl.multiple_of(r * prologue_rows, prologue_rows), prologue_rows)
            x = x_ref[rows, :]
            ms = jnp.mean(x * x, axis=-1, keepdims=True)
            y = x * lax.rsqrt(ms + NORM_EPS) * g_ref[...]
            h_ref[rows, :] = (y * (1.0 + sc) + sh).astype(BF16)
            return carry

        lax.fori_loop(0, x_ref.shape[0] // prologue_rows, body, 0)
        gate_ref[1] = jnp.zeros(gate_ref.shape[1:], F32)
        val_ref[1] = jnp.zeros(val_ref.shape[1:], F32)

    slot = j % 2
    o_ref[...] = (_conv3x3_gelu(gate_ref[1 - slot], cw_ref) * val_ref[1 - slot]).astype(BF16)
    hh = h_ref[...]
    gate_ref[slot] = jnp.dot(hh, w1g_ref[...], preferred_element_type=F32)
    val_ref[slot] = jnp.dot(hh, w1v_ref[...], preferred_element_type=F32)


def _ffn_act(x1, mod, gain, w1, conv_w, batch, sh_row, sc_row):
    m, d = x1.shape
    dff = conv_w.shape[1]
    seq = m // batch
    tf = _tile(dff, 256, LANES)
    nblk = dff // tf
    cur = lambda j: jnp.minimum(j, nblk - 1)
    prev = lambda j: jnp.maximum(j - 1, 0)
    return pl.pallas_call(
        functools.partial(_ffn_act_kernel, sh_row=sh_row, sc_row=sc_row,
                          prologue_rows=_tile(seq, 256)),
        grid=(batch, nblk + 1),
        in_specs=[pl.BlockSpec((seq, d), lambda b, j: (b, 0), pipeline_mode=pl.Buffered(1)),
                  pl.BlockSpec((1, 8, d), lambda b, j: (b, 0, 0)),
                  pl.BlockSpec((1, d), lambda b, j: (0, 0)),
                  pl.BlockSpec((d, tf), lambda b, j: (0, cur(j))),
                  pl.BlockSpec((d, tf), lambda b, j: (0, nblk + cur(j))),
                  pl.BlockSpec((conv_w.shape[0], tf), lambda b, j: (0, prev(j)))],
        out_specs=pl.BlockSpec((seq, tf), lambda b, j: (b, prev(j))),
        out_shape=jax.ShapeDtypeStruct((m, dff), BF16),
        scratch_shapes=[pltpu.VMEM((seq, d), BF16),
                        pltpu.VMEM((2, seq, tf), F32),
                        pltpu.VMEM((2, seq, tf), F32)],
        compiler_params=_params(("parallel", "arbitrary")),
        name="ffn_act",
    )(x1, mod, gain, w1, w1, conv_w)


def _ffn_out_kernel(act_ref, w2_ref, x_ref, mod_ref, g_ref, o_ref, acc_ref, *, gate_row):
    kstep = pl.program_id(1)

    @pl.when(kstep == 0)
    def _():
        acc_ref[...] = jnp.zeros_like(acc_ref)

    acc_ref[...] += jnp.dot(act_ref[...], w2_ref[...], preferred_element_type=F32)

    @pl.when(kstep == pl.num_programs(1) - 1)
    def _():
        y = x_ref[...] + mod_ref[0, gate_row:gate_row + 1, :] * acc_ref[...]
        ms = jnp.mean(y * y, axis=-1, keepdims=True)
        o_ref[...] = y * lax.rsqrt(ms + NORM_EPS) * g_ref[...]


def _ffn_out(act, w2, x1, mod, final_g, rows_per_mod, gate_row):
    m, d = x1.shape
    dff = act.shape[1]
    tm = _tile(rows_per_mod, 512)
    tk = _tile(dff, 1408, LANES)
    per = rows_per_mod // tm
    return pl.pallas_call(
        functools.partial(_ffn_out_kernel, gate_row=gate_row),
        grid=(m // tm, dff // tk),
        in_specs=[pl.BlockSpec((tm, tk), lambda i, k: (i, k)),
                  pl.BlockSpec((tk, d), lambda i, k: (k, 0)),
                  pl.BlockSpec((tm, d), lambda i, k: (i, 0)),
                  pl.BlockSpec((1, 8, d), lambda i, k: (i // per, 0, 0)),
                  pl.BlockSpec((1, d), lambda i, k: (0, 0))],
        out_specs=pl.BlockSpec((tm, d), lambda i, k: (i, 0)),
        out_shape=jax.ShapeDtypeStruct((m, d), F32),
        scratch_shapes=[pltpu.VMEM((tm, d), F32)],
        compiler_params=_params(("parallel", "arbitrary")),
        name="ffn_out",
    )(act, w2, x1, mod, final_g)


def _pad_cols(a, width):
    return jnp.pad(a, [(0, 0)] * (a.ndim - 1) + [(0, width - a.shape[-1])])


def _pad_rank(a):
    return jnp.pad(a, [(0, 0)] * (a.ndim - 2) + [(0, RANK_PAD - a.shape[-2]), (0, 0)])


def kernel(x, c, ctx, c_ctx, w_ada, b_ada, norm1_g, norm2_g, w_in, rw_mu, rw_k_k, rw_k_a, rw_r_k, rw_w0, rw_w_up, rw_a0, rw_a_up, rw_g_up, rw_gn_g, rw_gn_b, gd_conv_w, gd_a_log, gd_dt_bias, gd_norm_g, w_a_out, w_b_out, w_o, ffn_w1, ffn_conv_w, ffn_w2, final_norm_g):
    batch, seq, d = x.shape
    ctx_len = ctx.shape[1]
    assert w_ada.shape[0] == 1, "single layer only"
    rw_w = rw_k_k.shape[1]
    gd_w = w_b_out.shape[1]
    dec_rank = rw_w_up.shape[2]
    icl_rank = rw_a_up.shape[2]
    gate_rank = rw_g_up.shape[1]
    gd_heads = gd_a_log.shape[2]
    assert max(dec_rank, icl_rank) <= RANK_PAD and 4 * gd_heads <= 2 * LANES
    assert seq % CHUNK == 0 and ctx_len % CHUNK == 0 and seq % GRID_W == 0
    assert 3 * rw_w == 3 * gd_w and gate_rank <= 2 * LANES

    low_w = 8 * LANES
    blk0 = 3 * rw_w + low_w
    assert blk0 == 4 * gd_w
    wi = w_in[0]
    o_rw = 3 * rw_w
    o_gd = o_rw + 2 * dec_rank + 2 * icl_rank + gate_rank
    o_ab = o_gd + 4 * gd_w
    o_gate = o_ab + 4 * gd_heads

    def pack_cols(a):
        pieces = [a[..., :o_rw]]
        off = o_rw
        for r in (dec_rank, dec_rank, icl_rank, icl_rank):
            pieces.append(_pad_cols(a[..., off:off + r], RANK_PAD))
            off += r
        pieces.append(_pad_cols(a[..., off:off + gate_rank], 2 * LANES))
        return pieces

    w_pack = jnp.concatenate(
        pack_cols(wi) + [_pad_cols(wi[:, o_ab:o_gate], 2 * LANES), wi[:, o_gd:o_ab], wi[:, o_gate:]],
        axis=1).astype(BF16)
    n_ctx_cols = 2 * blk0
    gate_col0 = 2 * blk0
    mu_pack = jnp.concatenate(pack_cols(rw_mu) + [jnp.zeros((1, 2 * LANES), F32)], axis=1)

    cc = jnp.concatenate([c, c_ctx[None, :], jnp.zeros((16 - batch - 1, d), F32)], axis=0)
    mods = _mod(cc, w_ada[0], b_ada)
    mod_lat = _pad_rows8(mods[:batch].reshape(batch, 6, d))
    mod_ctx = _pad_rows8(mods[batch:batch + 1].reshape(1, 6, d))

    x2 = x.reshape(batch * seq, d)
    ctx2 = ctx.reshape(batch * ctx_len, d)
    p_lat = _normproj(x2, mod_lat, norm1_g, w_pack, seq, w_pack.shape[1], 0, 1, "inproj_lat")
    p_ctx = _normproj(ctx2, mod_ctx, norm1_g, w_pack, batch * ctx_len, n_ctx_cols, 0, 1, "inproj_ctx")

    rw_wts = (mu_pack, rw_k_k, rw_k_a, rw_r_k, rw_w0[0], _pad_rank(rw_w_up[0]).astype(BF16),
              rw_a0[0], _pad_rank(rw_a_up[0]).astype(BF16),
              jnp.pad(rw_g_up[0], ((0, 2 * LANES - gate_rank), (0, 0))).astype(BF16))
    n_pairs = rw_w // LANES
    s0 = jnp.zeros((batch, 2, n_pairs, LANES, LANES), F32)
    f_ctx = _rwprep(p_ctx, ctx_len, rw_wts)
    f_lat = _rwprep(p_lat, seq, rw_wts)
    (s_ctx,) = _rwscan(*f_ctx[:6], s0, batch, want_out=False)
    o_rw_f, o_rw_b, _ = _rwscan(*f_lat[:6], s_ctx, batch, want_out=True)
    ya = _rwread(o_rw_f, o_rw_b, f_lat[6], f_lat[7], rw_gn_g, rw_gn_b)

    s0g = jnp.zeros((batch, 2, gd_heads, GD_HEAD, GD_HEAD), F32)
    a_log2 = gd_a_log[0].reshape(1, 2 * gd_heads)
    dtb2 = gd_dt_bias[0].reshape(1, 2 * gd_heads)

    def gd_feats(p, seq_len):
        q, k, v, gcum, beta = _gdprep(p, seq_len, gd_conv_w[0], a_log2, dtb2, 1, 3, gd_w)
        m = p.shape[0]
        g3 = gcum.reshape(m, 2, gd_heads).transpose(1, 0, 2)
        gt = g3.reshape(2, m // CHUNK, CHUNK, gd_heads).transpose(0, 1, 3, 2)
        gt = gt.reshape(2, m // CHUNK, gd_heads // 2, 2 * CHUNK)
        b3 = beta.reshape(m, 2, gd_heads).transpose(1, 0, 2)
        return q, k, v, g3, gt, b3

    (sg_ctx,) = _gdscan(*gd_feats(p_ctx, ctx_len), s0g, batch, want_out=False)
    o_gd_f, o_gd_b, _ = _gdscan(*gd_feats(p_lat, seq), sg_ctx, batch, want_out=True)
    yb = _gdread(o_gd_f, o_gd_b, p_lat, 2 * blk0 // gd_w - 1, gd_norm_g)

    merged = _merge1(ya, yb, w_a_out[0].astype(BF16), w_b_out[0].astype(BF16), p_lat, gate_col0)
    x1 = _merge2(merged, w_o[0].astype(BF16), x2, mod_lat, seq, 2)

    act = _ffn_act(x1, mod_lat, norm2_g, ffn_w1[0].astype(BF16),
                   ffn_conv_w[0].reshape(-1, ffn_conv_w.shape[-1]), batch, 3, 4)
    out = _ffn_out(act, ffn_w2[0].astype(BF16), x1, mod_lat, final_norm_g[None, :], seq, 5)
    return out.reshape(batch, seq, d)


def _pad_rows8(a):
    return jnp.pad(a, ((0, 0), (0, 8 - a.shape[1]), (0, 0)))
```

```python
import functools
import math

import jax
import jax.numpy as jnp
from jax import lax
from jax.experimental import pallas as pl
from jax.experimental.pallas import tpu as pltpu

F32 = jnp.float32
BF16 = jnp.bfloat16
HIGHEST = lax.Precision.HIGHEST

NORM_EPS = 1e-6
RW_GN_EPS = 64e-5
RW_HEAD = 64
GD_HEAD = 128
LANES = 128
CHUNK = 64
GRID_W = 64
RANK_PAD = 128
VMEM_LIMIT = 56 * 1024 * 1024


def _params(sem):
    return pltpu.CompilerParams(dimension_semantics=sem, vmem_limit_bytes=VMEM_LIMIT)


def _tile(n, pref, mult=8):
    if n <= pref:
        return n
    t = (pref // mult) * mult
    while t >= mult:
        if n % t == 0:
            return t
        t -= mult
    return n


def _mm(a, b):
    return jnp.dot(a.astype(BF16), b.astype(BF16), preferred_element_type=F32)


def _mm_nt(a, b):
    return lax.dot_general(a.astype(BF16), b.astype(BF16), (((1,), (1,)), ((), ())),
                           preferred_element_type=F32)


def _mm_tn(a, b):
    return lax.dot_general(a.astype(BF16), b.astype(BF16), (((0,), (0,)), ((), ())),
                           preferred_element_type=F32)


def _mm_hi(a, b):
    return jnp.dot(a, b, precision=HIGHEST, preferred_element_type=F32)


def _split3(x):
    x1 = x.astype(BF16)
    r1 = x - x1.astype(F32)
    x2 = r1.astype(BF16)
    x3 = (r1 - x2.astype(F32)).astype(BF16)
    return x1, x2, x3


def _mm_sel_left(c, x):
    cb = c.astype(BF16)
    return jnp.dot(jnp.concatenate([cb, cb, cb], axis=1), jnp.concatenate(_split3(x), axis=0),
                   preferred_element_type=F32)


def _mm_sel_right(x, c):
    cb = c.astype(BF16)
    return jnp.dot(jnp.concatenate(_split3(x), axis=1), jnp.concatenate([cb, cb, cb], axis=0),
                   preferred_element_type=F32)


def _softplus(x):
    return jnp.maximum(x, 0.0) + jnp.log(1.0 + jnp.exp(-jnp.abs(x)))


def _sigmoid(x):
    return 1.0 / (1.0 + jnp.exp(-x))


def _seg_ones(width):
    i = lax.broadcasted_iota(jnp.int32, (LANES, LANES), 0) // width
    j = lax.broadcasted_iota(jnp.int32, (LANES, LANES), 1) // width
    return (i == j).astype(F32)


def _seg_sum(x, width):
    e = _seg_ones(width)
    n = x.shape[-1] // LANES
    parts = [_mm_sel_right(x[:, g * LANES:(g + 1) * LANES], e) for g in range(n)]
    return parts[0] if n == 1 else jnp.concatenate(parts, axis=-1)


def _tri(rev):
    i = lax.broadcasted_iota(jnp.int32, (CHUNK, CHUNK), 0)
    j = lax.broadcasted_iota(jnp.int32, (CHUNK, CHUNK), 1)
    return ((j >= i) if rev else (j <= i)).astype(F32)


def _level_masks(i, j):
    masks = [(i // 2) == (j // 2)]
    s = 2
    while s < CHUNK:
        masks.append(((i // (2 * s)) == (j // (2 * s))) & ((i // s) != (j // s)))
        s *= 2
    return masks


def _shift_rows(x, prev_row, next_row):
    n = x.shape[0]
    row = lax.broadcasted_iota(jnp.int32, x.shape, 0)
    xm1 = jnp.where(row == 0, prev_row, pltpu.roll(x, 1, 0))
    xp1 = jnp.where(row == n - 1, next_row, pltpu.roll(x, n - 1, 0))
    return xm1, xp1


def _mod_kernel(c_ref, w_ref, b_ref, o_ref):
    c = c_ref[...]
    s = c * _sigmoid(c)
    o_ref[...] = _mm_hi(s, w_ref[...]) + b_ref[...]


def _mod(cc, w_ada, b_ada):
    rows, d = cc.shape
    n = w_ada.shape[1]
    tn = _tile(n, 1024, LANES)
    return pl.pallas_call(
        _mod_kernel,
        grid=(n // tn,),
        in_specs=[pl.BlockSpec((rows, d), lambda j: (0, 0)),
                  pl.BlockSpec((d, tn), lambda j: (0, j)),
                  pl.BlockSpec((1, tn), lambda j: (0, j))],
        out_specs=pl.BlockSpec((rows, tn), lambda j: (0, j)),
        out_shape=jax.ShapeDtypeStruct((rows, n), F32),
        compiler_params=_params(("arbitrary",)),
        name="mod",
    )(cc, w_ada, b_ada)


def _normproj_kernel(x_ref, mod_ref, g_ref, w_ref, o_ref, h_ref, *, sh_row, sc_row):
    @pl.when(pl.program_id(1) == 0)
    def _():
        x = x_ref[...]
        ms = jnp.mean(x * x, axis=-1, keepdims=True)
        y = x * lax.rsqrt(ms + NORM_EPS) * g_ref[...]
        sh = mod_ref[0, sh_row:sh_row + 1, :]
        sc = mod_ref[0, sc_row:sc_row + 1, :]
        h_ref[...] = (y * (1.0 + sc) + sh).astype(BF16)

    o_ref[...] = jnp.dot(h_ref[...], w_ref[...], preferred_element_type=F32)


def _normproj(x2, mod, gain, w, rows_per_mod, n_cols, sh_row, sc_row, name):
    m, d = x2.shape
    tm = _tile(rows_per_mod, 1024)
    tn = _tile(n_cols, 1024, LANES)
    per = rows_per_mod // tm
    return pl.pallas_call(
        functools.partial(_normproj_kernel, sh_row=sh_row, sc_row=sc_row),
        grid=(m // tm, n_cols // tn),
        in_specs=[pl.BlockSpec((tm, d), lambda i, j: (i, 0)),
                  pl.BlockSpec((1, 8, d), lambda i, j: (i // per, 0, 0)),
                  pl.BlockSpec((1, d), lambda i, j: (0, 0)),
                  pl.BlockSpec((d, tn), lambda i, j: (0, j))],
        out_specs=pl.BlockSpec((tm, tn), lambda i, j: (i, j)),
        out_shape=jax.ShapeDtypeStruct((m, n_cols), F32),
        scratch_shapes=[pltpu.VMEM((tm, d), BF16)],
        compiler_params=_params(("parallel", "arbitrary")),
        name=name,
    )(x2, mod, gain, w)


def _rwprep_kernel(pc_ref, pp_ref, pn_ref, mu_ref, kk_ref, ka_ref, rk_ref, w0_ref, wup_ref,
                   a0_ref, aup_ref, gup_ref,
                   at_ref, bt_ref, kt_ref, rt_ref, v_ref, pt_ref, bonus_ref, g_ref, *, tiles_per_seq):
    i = pl.program_id(0)
    tm = pc_ref.shape[0]
    w = RW_HEAD * (kk_ref.shape[1] // RW_HEAD)
    first = (i % tiles_per_seq) == 0
    last = (i % tiles_per_seq) == tiles_per_seq - 1
    base = 3 * w
    used = base + 4 * RANK_PAD + gup_ref.shape[0]
    x = pc_ref[:, 0:used]
    prev_row = jnp.where(first, 0.0, pp_ref[7:8, 0:used])
    next_row = jnp.where(last, 0.0, pn_ref[0:1, 0:used])
    xm1, xp1 = _shift_rows(x, prev_row, next_row)
    xs = x + mu_ref[:, 0:used] * (0.5 * (xm1 + xp1) - x)

    r = xs[:, 0:w]
    k = xs[:, w:2 * w]
    v = xs[:, 2 * w:3 * w]
    gd = xs[:, base + 4 * RANK_PAD:used]
    g_ref[...] = _mm(_sigmoid(gd), gup_ref[...])
    v_ref[...] = v.astype(BF16)

    kx = k * kk_ref[...]
    kk = kx * lax.rsqrt(jnp.maximum(_seg_sum(kx * kx, RW_HEAD), 1e-12))

    ksum = jnp.zeros_like(k)
    for d in range(2):
        wd = xs[:, base + d * RANK_PAD: base + (d + 1) * RANK_PAD]
        ad = xs[:, base + (2 + d) * RANK_PAD: base + (3 + d) * RANK_PAD]
        wl = w0_ref[d:d + 1, :] + _mm(jnp.tanh(wd), wup_ref[d])
        lw = -math.exp(-0.5) * _sigmoid(wl)
        a = _sigmoid(a0_ref[d:d + 1, :] + _mm(ad, aup_ref[d]))
        kd = k * (1.0 + (a - 1.0) * ka_ref[...])
        ksum = ksum + kd
        tri = _tri(rev=(d == 1))
        for c in range(tm // CHUNK):
            rows = slice(c * CHUNK, (c + 1) * CHUNK)
            lwc = lw[rows]
            cum = _mm_sel_left(tri, lwc)
            p_in = jnp.exp(cum)
            p_inv = jnp.exp(-cum)
            p_ex = jnp.exp(cum - lwc)
            at_ref[d, rows, :] = (-kk[rows] * p_ex).astype(BF16)
            bt_ref[d, rows, :] = (kk[rows] * a[rows] * p_inv).astype(BF16)
            kt_ref[d, rows, :] = (kd[rows] * p_inv).astype(BF16)
            rt_ref[d, rows, :] = (r[rows] * p_in).astype(BF16)
            tot = cum[CHUNK - 1:CHUNK] if d == 0 else cum[0:1]
            pt_ref[d, c, :, :] = jnp.exp(tot)
    bonus_ref[...] = _seg_sum(r * ksum * rk_ref[...], RW_HEAD) * v


def _rwprep(p, seq_len, wts):
    m = p.shape[0]
    mu, k_k, k_a, r_k, w0, w_up, a0, a_up, g_up = wts
    w = k_k.shape[1]
    blk = mu.shape[1]
    tm = _tile(seq_len, 256, CHUNK)
    tps = seq_len // tm
    nb8 = m // 8
    full = lambda a: pl.BlockSpec(a.shape, lambda i: (0,) * a.ndim)
    feat = jax.ShapeDtypeStruct((2, m, w), BF16)
    feat_spec = pl.BlockSpec((2, tm, w), lambda i: (0, i, 0))
    row_spec = pl.BlockSpec((tm, w), lambda i: (i, 0))
    return pl.pallas_call(
        functools.partial(_rwprep_kernel, tiles_per_seq=tps),
        grid=(m // tm,),
        in_specs=[pl.BlockSpec((tm, blk), lambda i: (i, 0)),
                  pl.BlockSpec((8, blk), lambda i: (jnp.maximum(i * (tm // 8) - 1, 0), 0)),
                  pl.BlockSpec((8, blk), lambda i: (jnp.minimum((i + 1) * (tm // 8), nb8 - 1), 0)),
                  full(mu), full(k_k), full(k_a), full(r_k), full(w0), full(w_up), full(a0),
                  full(a_up), full(g_up)],
        out_specs=[feat_spec, feat_spec, feat_spec, feat_spec, row_spec,
                   pl.BlockSpec((2, tm // CHUNK, 1, w), lambda i: (0, i, 0, 0)),
                   row_spec, row_spec],
        out_shape=[feat, feat, feat, feat, jax.ShapeDtypeStruct((m, w), BF16),
                   jax.ShapeDtypeStruct((2, m // CHUNK, 1, w), F32),
                   jax.ShapeDtypeStruct((m, w), F32), jax.ShapeDtypeStruct((m, w), F32)],
        compiler_params=_params(("parallel",)),
        name="rwprep",
    )(p, p, p, mu, k_k, k_a, r_k, w0, w_up, a0, a_up, g_up)


def _pair_blockdiag(x, m0):
    zero = jnp.zeros_like(x)
    return jnp.concatenate([jnp.where(m0, x, zero), jnp.where(m0, zero, x)], axis=0)


def _tri_inverse(a, eye, blk, m0, sign):
    ts = [eye + sign * jnp.where(blk[0], x, 0.0) for x in a]
    for lvl in range(1, len(blk)):
        xs = [_mm(t, _pair_blockdiag(jnp.where(blk[lvl], x, 0.0), m0)) for t, x in zip(ts, a)]
        ts = [t + sign * _mm(x, _pair_blockdiag(t, m0)) for t, x in zip(ts, xs)]
    return ts


def _pair_masks(head_cols):
    i = lax.broadcasted_iota(jnp.int32, (CHUNK, LANES), 0)
    lane = lax.broadcasted_iota(jnp.int32, (CHUNK, LANES), 1)
    j = lane % head_cols
    strict = (j < i, j > i)
    incl = (j <= i, j >= i)
    eye = (j == i).astype(F32)
    return strict, incl, eye, _level_masks(i, j), lane < head_cols


def _rwscan_kernel(*refs, want_out):
    ins, rest = refs[:12], refs[12:]
    s0_ref = rest[0]
    if want_out:
        o_refs, s_ref, h_ref = rest[1:3], rest[3], rest[4]
    else:
        s_ref, h_ref = rest[1], rest[2]
    c = pl.program_id(1)
    n_pairs = h_ref.shape[1]

    @pl.when(c == 0)
    def _():
        h_ref[...] = s0_ref[0]

    strict, incl, eye, blk, m0 = _pair_masks(RW_HEAD)
    r2 = lax.broadcasted_iota(jnp.int32, (LANES, LANES), 0) // RW_HEAD
    c2 = lax.broadcasted_iota(jnp.int32, (LANES, LANES), 1) // RW_HEAD
    diag2 = r2 == c2

    chains = [(d, p) for d in range(2) for p in range(n_pairs)]
    cols = lambda p: slice(p * LANES, (p + 1) * LANES)
    at = [ins[6 * d + 0][0, :, cols(p)] for d, p in chains]
    bt = [ins[6 * d + 1][0, :, cols(p)] for d, p in chains]
    kt = [ins[6 * d + 2][0, :, cols(p)] for d, p in chains]
    rt = [ins[6 * d + 3][0, :, cols(p)] for d, p in chains]
    v = [ins[6 * d + 4][:, cols(p)] for d, p in chains]
    pt = [ins[6 * d + 5][0, 0, :, cols(p)] for d, p in chains]
    n = len(chains)
    bd = lambda x: _pair_blockdiag(x, m0)

    s4 = [_mm_nt(jnp.concatenate([at[i], rt[i]], axis=0),
                 jnp.concatenate([bd(bt[i]), bd(kt[i])], axis=0)) for i in range(n)]
    a_ab = [jnp.where(strict[chains[i][0]], s4[i][:CHUNK, :LANES], 0.0) for i in range(n)]
    a_ak = [jnp.where(strict[chains[i][0]], s4[i][:CHUNK, LANES:], 0.0) for i in range(n)]
    t = _tri_inverse(a_ab, eye, blk, m0, 1.0)
    av = [_mm(a_ak[i], bd(v[i])) for i in range(n)]
    wu = [_mm(t[i], jnp.concatenate([bd(at[i]), bd(av[i].astype(BF16))], axis=1)) for i in range(n)]

    ht = [h_ref[d, p] for d, p in chains]
    if want_out:
        m1 = [_mm_nt(jnp.concatenate([wu[i][:, :LANES].astype(BF16), rt[i]], axis=0), ht[i])
              for i in range(n)]
        u = [m1[i][:CHUNK] + wu[i][:, LANES:] for i in range(n)]
    else:
        u = [_mm_nt(wu[i][:, :LANES], ht[i]) + wu[i][:, LANES:] for i in range(n)]
    ub = [x.astype(BF16) for x in u]
    if want_out:
        for i, (d, p) in enumerate(chains):
            m_r = jnp.where(jnp.concatenate([incl[d], incl[d]], axis=1), s4[i][CHUNK:], 0.0)
            o = m1[i][CHUNK:] + _mm(m_r, jnp.concatenate([bd(ub[i]), bd(v[i])], axis=0))
            o_refs[d][:, cols(p)] = o
    for i, (d, p) in enumerate(chains):
        upd = _mm_tn(jnp.concatenate([ub[i], v[i]], axis=0),
                     jnp.concatenate([bt[i], kt[i]], axis=0))
        h_ref[d, p] = (ht[i] + jnp.where(diag2, upd, 0.0)) * pt[i]

    @pl.when(c == pl.num_programs(1) - 1)
    def _():
        s_ref[0] = h_ref[...]


def _rwscan(at, bt, kt, rt, v, pt, s0, batch, want_out):
    m, w = v.shape
    n_chunks = m // batch // CHUNK
    n_pairs = w // LANES
    rows = (lambda b, c: b * n_chunks + c, lambda b, c: b * n_chunks + n_chunks - 1 - c)

    in_specs, args = [], []
    for d in range(2):
        feat_spec = pl.BlockSpec((1, CHUNK, w), lambda b, c, d=d: (d, rows[d](b, c), 0))
        in_specs += [feat_spec] * 4
        in_specs += [pl.BlockSpec((CHUNK, w), lambda b, c, d=d: (rows[d](b, c), 0)),
                     pl.BlockSpec((1, 1, 1, w), lambda b, c, d=d: (d, rows[d](b, c), 0, 0))]
        args += [at, bt, kt, rt, v, pt]
    state_spec = pl.BlockSpec((1, 2, n_pairs, LANES, LANES), lambda b, c: (b, 0, 0, 0, 0))
    out_specs = [state_spec]
    out_shape = [jax.ShapeDtypeStruct(s0.shape, F32)]
    if want_out:
        out_specs = [pl.BlockSpec((CHUNK, w), lambda b, c, d=d: (rows[d](b, c), 0))
                     for d in range(2)] + out_specs
        out_shape = [jax.ShapeDtypeStruct((m, w), F32)] * 2 + out_shape
    return pl.pallas_call(
        functools.partial(_rwscan_kernel, want_out=want_out),
        grid=(batch, n_chunks),
        in_specs=in_specs + [state_spec],
        out_specs=out_specs,
        out_shape=out_shape,
        scratch_shapes=[pltpu.VMEM((2, n_pairs, LANES, LANES), F32)],
        compiler_params=_params(("parallel", "arbitrary")),
        name="rwscan_out" if want_out else "rwscan_state",
    )(*args, s0)


def _rwread_kernel(of_ref, ob_ref, bonus_ref, g_ref, gng_ref, gnb_ref, y_ref):
    o = of_ref[...] + ob_ref[...]
    inv = 1.0 / RW_HEAD
    mean = _seg_sum(o, RW_HEAD) * inv
    cen = o - mean
    var = _seg_sum(cen * cen, RW_HEAD) * inv
    on = cen * lax.rsqrt(var + RW_GN_EPS) * gng_ref[...] + gnb_ref[...]
    y_ref[...] = ((on + bonus_ref[...]) * g_ref[...]).astype(BF16)


def _rwread(o_f, o_b, bonus, g, gn_g, gn_b):
    m, w = o_f.shape
    tm = _tile(m, 512)
    row_spec = pl.BlockSpec((tm, w), lambda i: (i, 0))
    vec_spec = pl.BlockSpec((1, w), lambda i: (0, 0))
    return pl.pallas_call(
        _rwread_kernel,
        grid=(m // tm,),
        in_specs=[row_spec, row_spec, row_spec, row_spec, vec_spec, vec_spec],
        out_specs=row_spec,
        out_shape=jax.ShapeDtypeStruct((m, w), BF16),
        compiler_params=_params(("parallel",)),
        name="rwread",
    )(o_f, o_b, bonus, g, gn_g, gn_b)


def _gdprep_kernel(pc_ref, pp_ref, pn_ref, ab_ref, cw_ref, alog_ref, dtb_ref,
                   q_ref, k_ref, v_ref, gcum_ref, beta_ref, *, tiles_per_seq):
    i = pl.program_id(0)
    tm = pc_ref.shape[0]
    w = q_ref.shape[1]
    n_heads = w // GD_HEAD
    first = (i % tiles_per_seq) == 0
    last = (i % tiles_per_seq) == tiles_per_seq - 1
    x = pc_ref[:, 0:3 * w]
    prev_row = jnp.where(first, 0.0, pp_ref[7:8, 0:3 * w])
    next_row = jnp.where(last, 0.0, pn_ref[0:1, 0:3 * w])
    xm1, xp1 = _shift_rows(x, prev_row, next_row)
    y = xm1 * cw_ref[0:1, :] + x * cw_ref[1:2, :] + xp1 * cw_ref[2:3, :]
    y = y * _sigmoid(y)
    for h in range(n_heads):
        for part, ref, scale in ((0, q_ref, GD_HEAD ** -0.5), (1, k_ref, 1.0)):
            cols = slice(part * w + h * GD_HEAD, part * w + (h + 1) * GD_HEAD)
            t = y[:, cols]
            ss = jnp.sum(t * t, axis=-1, keepdims=True)
            ref[:, h * GD_HEAD:(h + 1) * GD_HEAD] = (
                t * (lax.rsqrt(jnp.maximum(ss, 1e-12)) * scale)).astype(BF16)
    v_ref[...] = y[:, 2 * w:3 * w].astype(BF16)

    ab = ab_ref[:, 6 * LANES:7 * LANES]
    a = ab[:, 0:2 * n_heads]
    b = ab[:, 2 * n_heads:4 * n_heads]
    glog = -jnp.exp(alog_ref[...]) * _softplus(a + dtb_ref[...])
    beta_ref[...] = _sigmoid(b)
    for c in range(tm // CHUNK):
        rows = slice(c * CHUNK, (c + 1) * CHUNK)
        gc = glog[rows]
        fwd = _mm_sel_left(_tri(False), gc)
        bwd = _mm_sel_left(_tri(True), gc)
        col = lax.broadcasted_iota(jnp.int32, gc.shape, 1)
        gcum_ref[rows, :] = jnp.where(col < n_heads, fwd, bwd)


def _gdprep(p, seq_len, conv_w, a_log, dt_bias, qkvz_block, ab_block, width):
    m = p.shape[0]
    blk = 4 * width
    tm = _tile(seq_len, 256, CHUNK)
    tps = seq_len // tm
    nb8 = m // 8
    n2h = a_log.shape[1]
    full = lambda a: pl.BlockSpec(a.shape, lambda i: (0,) * a.ndim)
    row_spec = pl.BlockSpec((tm, width), lambda i: (i, 0))
    small_spec = pl.BlockSpec((tm, n2h), lambda i: (i, 0))
    return pl.pallas_call(
        functools.partial(_gdprep_kernel, tiles_per_seq=tps),
        grid=(m // tm,),
        in_specs=[pl.BlockSpec((tm, blk), lambda i: (i, qkvz_block)),
                  pl.BlockSpec((8, blk), lambda i: (jnp.maximum(i * (tm // 8) - 1, 0), qkvz_block)),
                  pl.BlockSpec((8, blk),
                               lambda i: (jnp.minimum((i + 1) * (tm // 8), nb8 - 1), qkvz_block)),
                  pl.BlockSpec((tm, 8 * LANES), lambda i: (i, ab_block)),
                  full(conv_w), full(a_log), full(dt_bias)],
        out_specs=[row_spec, row_spec, row_spec, small_spec, small_spec],
        out_shape=[jax.ShapeDtypeStruct((m, width), BF16)] * 3
        + [jax.ShapeDtypeStruct((m, n2h), F32)] * 2,
        compiler_params=_params(("parallel",)),
        name="gdprep",
    )(p, p, p, p, conv_w, a_log, dt_bias)


def _gdscan_kernel(*refs, want_out):
    ins, rest = refs[:12], refs[12:]
    s0_ref = rest[0]
    if want_out:
        o_refs, s_ref, st_ref = rest[1:3], rest[3], rest[4]
    else:
        s_ref, st_ref = rest[1], rest[2]
    c = pl.program_id(1)
    n_pairs = st_ref.shape[1] // 2

    @pl.when(c == 0)
    def _():
        st_ref[...] = s0_ref[0]

    strict, incl, eye, blk, m0 = _pair_masks(CHUNK)
    bd = lambda x: _pair_blockdiag(x, m0)
    chains = [(d, p) for d in range(2) for p in range(n_pairs)]
    n = len(chains)
    hcols = lambda h: slice(h * GD_HEAD, (h + 1) * GD_HEAD)

    def head_vals(d, p, e):
        h = 2 * p + e
        q_ref, k_ref, v_ref, g_ref, _, beta_ref = ins[6 * d:6 * d + 6]
        return (q_ref[:, hcols(h)].astype(F32), k_ref[:, hcols(h)].astype(F32),
                v_ref[:, hcols(h)].astype(F32), g_ref[0, :, h:h + 1], beta_ref[0, :, h:h + 1])

    hv = [[head_vals(d, p, e) for e in range(2)] for d, p in chains]
    kb = [[hv[i][e][1] * hv[i][e][4] for e in range(2)] for i in range(n)]
    eg = [[jnp.exp(hv[i][e][3]) for e in range(2)] for i in range(n)]
    decay, s2 = [], []
    for i, (d, p) in enumerate(chains):
        gcol = jnp.where(m0, hv[i][0][3], hv[i][1][3])
        grow = ins[6 * d + 4][0, 0, p:p + 1, :]
        decay.append(jnp.where(incl[d], jnp.exp(jnp.where(incl[d], gcol - grow, 0.0)), 0.0))
        k0, k1 = hv[i][0][1], hv[i][1][1]
        zero = jnp.zeros_like(k0)
        lhs = jnp.concatenate([jnp.concatenate([kb[i][0], kb[i][1]], axis=1),
                               jnp.concatenate([hv[i][0][0], hv[i][1][0]], axis=1)], axis=0)
        rhs = jnp.concatenate([jnp.concatenate([k0, zero], axis=1),
                               jnp.concatenate([zero, k1], axis=1)], axis=0)
        s2.append(_mm_nt(lhs, rhs))
    a = [jnp.where(strict[chains[i][0]], s2[i][:CHUNK] * decay[i], 0.0) for i in range(n)]
    t = _tri_inverse(a, eye, blk, m0, -1.0)
    sol = [_mm(bd(t[i]), jnp.concatenate(
        [jnp.concatenate([hv[i][e][2] * hv[i][e][4], kb[i][e] * eg[i][e]], axis=1)
         for e in range(2)], axis=0)) for i in range(n)]

    st = [[st_ref[d, 2 * p + e] for e in range(2)] for d, p in chains]
    ws = [[_mm(jnp.concatenate([sol[i][e * CHUNK:(e + 1) * CHUNK, GD_HEAD:],
                                hv[i][e][0] * eg[i][e]], axis=0), st[i][e])
           for e in range(2)] for i in range(n)]
    v_new = [[sol[i][e * CHUNK:(e + 1) * CHUNK, :GD_HEAD] - ws[i][e][:CHUNK] for e in range(2)]
             for i in range(n)]
    if want_out:
        for i, (d, p) in enumerate(chains):
            intra = _mm(bd(s2[i][CHUNK:] * decay[i]),
                        jnp.concatenate([v_new[i][0], v_new[i][1]], axis=0))
            for e in range(2):
                o_refs[d][:, hcols(2 * p + e)] = (ws[i][e][CHUNK:]
                                                  + intra[e * CHUNK:(e + 1) * CHUNK])
    for i, (d, p) in enumerate(chains):
        for e in range(2):
            gcol = hv[i][e][3]
            g_last = jnp.min(gcol, axis=0, keepdims=True)
            k_dec = hv[i][e][1] * jnp.exp(g_last - gcol)
            st_ref[d, 2 * p + e] = st[i][e] * jnp.exp(g_last) + _mm_tn(k_dec, v_new[i][e])

    @pl.when(c == pl.num_programs(1) - 1)
    def _():
        s_ref[0] = st_ref[...]


def _gdscan(q, k, v, g, gt, beta, s0, batch, want_out):
    m, w = q.shape
    n_chunks = m // batch // CHUNK
    n_heads = w // GD_HEAD
    rows = (lambda b, c: b * n_chunks + c, lambda b, c: b * n_chunks + n_chunks - 1 - c)

    in_specs, args = [], []
    for d in range(2):
        row_spec = pl.BlockSpec((CHUNK, w), lambda b, c, d=d: (rows[d](b, c), 0))
        col_spec = pl.BlockSpec((1, CHUNK, n_heads), lambda b, c, d=d: (d, rows[d](b, c), 0))
        in_specs += [row_spec, row_spec, row_spec, col_spec,
                     pl.BlockSpec((1, 1, n_heads // 2, LANES),
                                  lambda b, c, d=d: (d, rows[d](b, c), 0, 0)),
                     col_spec]
        args += [q, k, v, g, gt, beta]
    state_spec = pl.BlockSpec((1, 2, n_heads, GD_HEAD, GD_HEAD), lambda b, c: (b, 0, 0, 0, 0))
    out_specs = [state_spec]
    out_shape = [jax.ShapeDtypeStruct(s0.shape, F32)]
    if want_out:
        out_specs = [pl.BlockSpec((CHUNK, w), lambda b, c, d=d: (rows[d](b, c), 0))
                     for d in range(2)] + out_specs
        out_shape = [jax.ShapeDtypeStruct((m, w), F32)] * 2 + out_shape
    return pl.pallas_call(
        functools.partial(_gdscan_kernel, want_out=want_out),
        grid=(batch, n_chunks),
        in_specs=in_specs + [state_spec],
        out_specs=out_specs,
        out_shape=out_shape,
        scratch_shapes=[pltpu.VMEM((2, n_heads, GD_HEAD, GD_HEAD), F32)],
        compiler_params=_params(("parallel", "arbitrary")),
        name="gdscan_out" if want_out else "gdscan_state",
    )(*args, s0)


def _gdread_kernel(of_ref, ob_ref, z_ref, ng_ref, y_ref):
    o = of_ref[...] + ob_ref[...]
    z = z_ref[...]
    gate = z * _sigmoid(z)
    for h in range(o.shape[1] // GD_HEAD):
        cols = slice(h * GD_HEAD, (h + 1) * GD_HEAD)
        oh = o[:, cols]
        ms = jnp.mean(oh * oh, axis=-1, keepdims=True)
        y_ref[:, cols] = (oh * lax.rsqrt(ms + NORM_EPS) * ng_ref[...] * gate[:, cols]).astype(BF16)


def _gdread(o_f, o_b, p, z_block, norm_g):
    m, w = o_f.shape
    tm = _tile(m, 512)
    return pl.pallas_call(
        _gdread_kernel,
        grid=(m // tm,),
        in_specs=[pl.BlockSpec((tm, w), lambda i: (i, 0)),
                  pl.BlockSpec((tm, w), lambda i: (i, 0)),
                  pl.BlockSpec((tm, w), lambda i: (i, z_block)),
                  pl.BlockSpec((1, GD_HEAD), lambda i: (0, 0))],
        out_specs=pl.BlockSpec((tm, w), lambda i: (i, 0)),
        out_shape=jax.ShapeDtypeStruct((m, w), BF16),
        compiler_params=_params(("parallel",)),
        name="gdread",
    )(o_f, o_b, p, norm_g)


def _merge1_kernel(ya_ref, yb_ref, wa_ref, wb_ref, ga_ref, gb_ref, o_ref):
    a = jnp.dot(ya_ref[...], wa_ref[...], preferred_element_type=F32)
    b = jnp.dot(yb_ref[...], wb_ref[...], preferred_element_type=F32)
    o_ref[...] = (_sigmoid(ga_ref[...]) * a + _sigmoid(gb_ref[...]) * b).astype(BF16)


def _merge1(ya, yb, wa, wb, p, gate_col0):
    m, ka = ya.shape
    kb = yb.shape[1]
    d = wa.shape[1]
    tm = _tile(m, 1024)
    tn = _tile(d, 1024, LANES)
    ga0 = gate_col0 // tn
    gb0 = (gate_col0 + d) // tn
    return pl.pallas_call(
        _merge1_kernel,
        grid=(m // tm, d // tn),
        in_specs=[pl.BlockSpec((tm, ka), lambda i, j: (i, 0)),
                  pl.BlockSpec((tm, kb), lambda i, j: (i, 0)),
                  pl.BlockSpec((ka, tn), lambda i, j: (0, j)),
                  pl.BlockSpec((kb, tn), lambda i, j: (0, j)),
                  pl.BlockSpec((tm, tn), lambda i, j: (i, ga0 + j)),
                  pl.BlockSpec((tm, tn), lambda i, j: (i, gb0 + j))],
        out_specs=pl.BlockSpec((tm, tn), lambda i, j: (i, j)),
        out_shape=jax.ShapeDtypeStruct((m, d), BF16),
        compiler_params=_params(("parallel", "arbitrary")),
        name="merge1",
    )(ya, yb, wa, wb, p, p)


def _merge2_kernel(mg_ref, wo_ref, x_ref, mod_ref, o_ref, *, gate_row):
    y = jnp.dot(mg_ref[...], wo_ref[...], preferred_element_type=F32)
    o_ref[...] = x_ref[...] + mod_ref[0, gate_row:gate_row + 1, :] * y


def _merge2(merged, wo, x2, mod, rows_per_mod, gate_row):
    m, d = x2.shape
    tm = _tile(rows_per_mod, 1024)
    tn = _tile(d, 1024, LANES)
    per = rows_per_mod // tm
    return pl.pallas_call(
        functools.partial(_merge2_kernel, gate_row=gate_row),
        grid=(m // tm, d // tn),
        in_specs=[pl.BlockSpec((tm, d), lambda i, j: (i, 0)),
                  pl.BlockSpec((d, tn), lambda i, j: (0, j)),
                  pl.BlockSpec((tm, tn), lambda i, j: (i, j)),
                  pl.BlockSpec((1, 8, tn), lambda i, j: (i // per, 0, j))],
        out_specs=pl.BlockSpec((tm, tn), lambda i, j: (i, j)),
        out_shape=jax.ShapeDtypeStruct((m, d), F32),
        compiler_params=_params(("parallel", "arbitrary")),
        name="merge2",
    )(merged, wo, x2, mod)


def _convglu_lines(gate_ref, val_ref, cw_ref, o_ref, slot):
    n, tf = o_ref.shape
    n_lines = n // GRID_W
    col = lax.broadcasted_iota(jnp.int32, (GRID_W, LANES), 0)
    first_col = col == 0
    last_col = col == GRID_W - 1
    groups = [slice(c0, c0 + LANES) for c0 in range(0, tf, LANES)]
    w = [[cw_ref[k:k + 1, lanes] for k in range(9)] for lanes in groups]
    acc = [{} for _ in groups]

    def feed(line):
        for gi, lanes in enumerate(groups):
            if line < n_lines:
                g = gate_ref[slot, line * GRID_W:(line + 1) * GRID_W, lanes]
                left = jnp.where(first_col, 0.0, pltpu.roll(g, 1, 0))
                right = jnp.where(last_col, 0.0, pltpu.roll(g, GRID_W - 1, 0))
                for kh in range(3):
                    dst = line + 1 - kh
                    if 0 <= dst < n_lines:
                        t = (left * w[gi][3 * kh] + g * w[gi][3 * kh + 1]
                             + right * w[gi][3 * kh + 2])
                        acc[gi][dst] = acc[gi][dst] + t if dst in acc[gi] else t
            done = line - 1
            if done >= 0:
                a = acc[gi].pop(done)
                rows = slice(done * GRID_W, (done + 1) * GRID_W)
                gelu = 0.5 * a * (1.0 + lax.erf(a * (2.0 ** -0.5)))
                o_ref[rows, lanes] = (gelu * val_ref[slot, rows, lanes]).astype(BF16)

    return feed, n_lines + 1


def _ffn_act_kernel(x_ref, mod_ref, g_ref, w1g_ref, w1v_ref, cw_ref, o_ref, h_ref, gate_ref, val_ref,
                    *, sh_row, sc_row, prologue_rows, dot_rows):
    j = pl.program_id(1)

    @pl.when(j == 0)
    def _():
        sh = mod_ref[0, sh_row:sh_row + 1, :]
        sc = mod_ref[0, sc_row:sc_row + 1, :]

        def body(r, carry):
            rows = pl.ds(pl.multiple_of(r * prologue_rows, prologue_rows), prologue_rows)
            x = x_ref[rows, :]
            ms = jnp.mean(x * x, axis=-1, keepdims=True)
            y = x * lax.rsqrt(ms + NORM_EPS) * g_ref[...]
            h_ref[rows, :] = (y * (1.0 + sc) + sh).astype(BF16)
            return carry

        lax.fori_loop(0, x_ref.shape[0] // prologue_rows, body, 0)
        gate_ref[1] = jnp.zeros(gate_ref.shape[1:], F32)
        val_ref[1] = jnp.zeros(val_ref.shape[1:], F32)

    slot = j % 2
    feed, n_feeds = _convglu_lines(gate_ref, val_ref, cw_ref, o_ref, 1 - slot)
    for line in range(n_feeds):
        feed(line)
    w1g = w1g_ref[...].astype(BF16)
    w1v = w1v_ref[...].astype(BF16)
    for r0 in range(0, h_ref.shape[0], dot_rows):
        rows = slice(r0, r0 + dot_rows)
        hh = h_ref[rows, :]
        gate_ref[slot, rows, :] = jnp.dot(hh, w1g, preferred_element_type=F32)
        val_ref[slot, rows, :] = jnp.dot(hh, w1v, preferred_element_type=F32)


def _ffn_act(x1, mod, gain, w1, conv_w, batch, sh_row, sc_row):
    m, d = x1.shape
    dff = conv_w.shape[1]
    seq = m // batch
    tf = _tile(dff, 256, LANES)
    nblk = dff // tf
    cur = lambda j: jnp.minimum(j, nblk - 1)
    prev = lambda j: jnp.maximum(j - 1, 0)
    return pl.pallas_call(
        functools.partial(_ffn_act_kernel, sh_row=sh_row, sc_row=sc_row,
                          prologue_rows=_tile(seq, 256), dot_rows=_tile(seq, 256)),
        grid=(batch, nblk + 1),
        in_specs=[pl.BlockSpec((seq, d), lambda b, j: (b, 0), pipeline_mode=pl.Buffered(1)),
                  pl.BlockSpec((1, 8, d), lambda b, j: (b, 0, 0)),
                  pl.BlockSpec((1, d), lambda b, j: (0, 0)),
                  pl.BlockSpec((d, tf), lambda b, j: (0, cur(j))),
                  pl.BlockSpec((d, tf), lambda b, j: (0, nblk + cur(j))),
                  pl.BlockSpec((conv_w.shape[0], tf), lambda b, j: (0, prev(j)))],
        out_specs=pl.BlockSpec((seq, tf), lambda b, j: (b, prev(j))),
        out_shape=jax.ShapeDtypeStruct((m, dff), BF16),
        scratch_shapes=[pltpu.VMEM((seq, d), BF16),
                        pltpu.VMEM((2, seq, tf), F32),
                        pltpu.VMEM((2, seq, tf), F32)],
        compiler_params=_params(("parallel", "arbitrary")),
        name="ffn_act",
    )(x1, mod, gain, w1, w1, conv_w)


def _ffn_out_kernel(act_ref, w2_ref, x_ref, mod_ref, g_ref, o_ref, acc_ref, *, gate_row):
    kstep = pl.program_id(1)

    @pl.when(kstep == 0)
    def _():
        acc_ref[...] = jnp.zeros_like(acc_ref)

    acc_ref[...] += jnp.dot(act_ref[...], w2_ref[...], preferred_element_type=F32)

    @pl.when(kstep == pl.num_programs(1) - 1)
    def _():
        y = x_ref[...] + mod_ref[0, gate_row:gate_row + 1, :] * acc_ref[...]
        ms = jnp.mean(y * y, axis=-1, keepdims=True)
        o_ref[...] = y * lax.rsqrt(ms + NORM_EPS) * g_ref[...]


def _ffn_out(act, w2, x1, mod, final_g, rows_per_mod, gate_row):
    m, d = x1.shape
    dff = act.shape[1]
    tm = _tile(rows_per_mod, 512)
    tk = _tile(dff, 1408, LANES)
    per = rows_per_mod // tm
    return pl.pallas_call(
        functools.partial(_ffn_out_kernel, gate_row=gate_row),
        grid=(m // tm, dff // tk),
        in_specs=[pl.BlockSpec((tm, tk), lambda i, k: (i, k)),
                  pl.BlockSpec((tk, d), lambda i, k: (k, 0)),
                  pl.BlockSpec((tm, d), lambda i, k: (i, 0)),
                  pl.BlockSpec((1, 8, d), lambda i, k: (i // per, 0, 0)),
                  pl.BlockSpec((1, d), lambda i, k: (0, 0))],
        out_specs=pl.BlockSpec((tm, d), lambda i, k: (i, 0)),
        out_shape=jax.ShapeDtypeStruct((m, d), F32),
        scratch_shapes=[pltpu.VMEM((tm, d), F32)],
        compiler_params=_params(("parallel", "arbitrary")),
        name="ffn_out",
    )(act, w2, x1, mod, final_g)


def _pad_cols(a, width):
    return jnp.pad(a, [(0, 0)] * (a.ndim - 1) + [(0, width - a.shape[-1])])


def _pad_rank(a):
    return jnp.pad(a, [(0, 0)] * (a.ndim - 2) + [(0, RANK_PAD - a.shape[-2]), (0, 0)])


def kernel(x, c, ctx, c_ctx, w_ada, b_ada, norm1_g, norm2_g, w_in, rw_mu, rw_k_k, rw_k_a, rw_r_k, rw_w0, rw_w_up, rw_a0, rw_a_up, rw_g_up, rw_gn_g, rw_gn_b, gd_conv_w, gd_a_log, gd_dt_bias, gd_norm_g, w_a_out, w_b_out, w_o, ffn_w1, ffn_conv_w, ffn_w2, final_norm_g):
    batch, seq, d = x.shape
    ctx_len = ctx.shape[1]
    assert w_ada.shape[0] == 1, "single layer only"
    rw_w = rw_k_k.shape[1]
    gd_w = w_b_out.shape[1]
    dec_rank = rw_w_up.shape[2]
    icl_rank = rw_a_up.shape[2]
    gate_rank = rw_g_up.shape[1]
    gd_heads = gd_a_log.shape[2]
    assert max(dec_rank, icl_rank) <= RANK_PAD and 4 * gd_heads <= 2 * LANES
    assert seq % CHUNK == 0 and ctx_len % CHUNK == 0 and seq % GRID_W == 0
    assert 3 * rw_w == 3 * gd_w and gate_rank <= 2 * LANES

    low_w = 8 * LANES
    blk0 = 3 * rw_w + low_w
    assert blk0 == 4 * gd_w
    wi = w_in[0]
    o_rw = 3 * rw_w
    o_gd = o_rw + 2 * dec_rank + 2 * icl_rank + gate_rank
    o_ab = o_gd + 4 * gd_w
    o_gate = o_ab + 4 * gd_heads

    def pack_cols(a):
        pieces = [a[..., :o_rw]]
        off = o_rw
        for r in (dec_rank, dec_rank, icl_rank, icl_rank):
            pieces.append(_pad_cols(a[..., off:off + r], RANK_PAD))
            off += r
        pieces.append(_pad_cols(a[..., off:off + gate_rank], 2 * LANES))
        return pieces

    w_pack = jnp.concatenate(
        pack_cols(wi) + [_pad_cols(wi[:, o_ab:o_gate], 2 * LANES), wi[:, o_gd:o_ab], wi[:, o_gate:]],
        axis=1).astype(BF16)
    n_ctx_cols = 2 * blk0
    gate_col0 = 2 * blk0
    mu_pack = jnp.concatenate(pack_cols(rw_mu) + [jnp.zeros((1, 2 * LANES), F32)], axis=1)

    cc = jnp.concatenate([c, c_ctx[None, :], jnp.zeros((16 - batch - 1, d), F32)], axis=0)
    mods = _mod(cc, w_ada[0], b_ada)
    mod_lat = _pad_rows8(mods[:batch].reshape(batch, 6, d))
    mod_ctx = _pad_rows8(mods[batch:batch + 1].reshape(1, 6, d))

    x2 = x.reshape(batch * seq, d)
    ctx2 = ctx.reshape(batch * ctx_len, d)
    p_lat = _normproj(x2, mod_lat, norm1_g, w_pack, seq, w_pack.shape[1], 0, 1, "inproj_lat")
    p_ctx = _normproj(ctx2, mod_ctx, norm1_g, w_pack, batch * ctx_len, n_ctx_cols, 0, 1, "inproj_ctx")

    rw_wts = (mu_pack, rw_k_k, rw_k_a, rw_r_k, rw_w0[0], _pad_rank(rw_w_up[0]).astype(BF16),
              rw_a0[0], _pad_rank(rw_a_up[0]).astype(BF16),
              jnp.pad(rw_g_up[0], ((0, 2 * LANES - gate_rank), (0, 0))).astype(BF16))
    n_pairs = rw_w // LANES
    s0 = jnp.zeros((batch, 2, n_pairs, LANES, LANES), F32)
    f_ctx = _rwprep(p_ctx, ctx_len, rw_wts)
    f_lat = _rwprep(p_lat, seq, rw_wts)
    (s_ctx,) = _rwscan(*f_ctx[:6], s0, batch, want_out=False)
    o_rw_f, o_rw_b, _ = _rwscan(*f_lat[:6], s_ctx, batch, want_out=True)
    ya = _rwread(o_rw_f, o_rw_b, f_lat[6], f_lat[7], rw_gn_g, rw_gn_b)

    s0g = jnp.zeros((batch, 2, gd_heads, GD_HEAD, GD_HEAD), F32)
    a_log2 = gd_a_log[0].reshape(1, 2 * gd_heads)
    dtb2 = gd_dt_bias[0].reshape(1, 2 * gd_heads)

    def gd_feats(p, seq_len):
        q, k, v, gcum, beta = _gdprep(p, seq_len, gd_conv_w[0], a_log2, dtb2, 1, 3, gd_w)
        m = p.shape[0]
        g3 = gcum.reshape(m, 2, gd_heads).transpose(1, 0, 2)
        gt = g3.reshape(2, m // CHUNK, CHUNK, gd_heads).transpose(0, 1, 3, 2)
        gt = gt.reshape(2, m // CHUNK, gd_heads // 2, 2 * CHUNK)
        b3 = beta.reshape(m, 2, gd_heads).transpose(1, 0, 2)
        return q, k, v, g3, gt, b3

    (sg_ctx,) = _gdscan(*gd_feats(p_ctx, ctx_len), s0g, batch, want_out=False)
    o_gd_f, o_gd_b, _ = _gdscan(*gd_feats(p_lat, seq), sg_ctx, batch, want_out=True)
    yb = _gdread(o_gd_f, o_gd_b, p_lat, 2 * blk0 // gd_w - 1, gd_norm_g)

    merged = _merge1(ya, yb, w_a_out[0].astype(BF16), w_b_out[0].astype(BF16), p_lat, gate_col0)
    x1 = _merge2(merged, w_o[0].astype(BF16), x2, mod_lat, seq, 2)

    act = _ffn_act(x1, mod_lat, norm2_g, ffn_w1[0],
                   ffn_conv_w[0].reshape(-1, ffn_conv_w.shape[-1]), batch, 3, 4)
    out = _ffn_out(act, ffn_w2[0].astype(BF16), x1, mod_lat, final_norm_g[None, :], seq, 5)
    return out.reshape(batch, seq, d)


def _pad_rows8(a):
    return jnp.pad(a, ((0, 0), (0, 8 - a.shape[1]), (0, 0)))
```

```python
import functools
import math

import jax
import jax.numpy as jnp
from jax import lax
from jax.experimental import pallas as pl
from jax.experimental.pallas import tpu as pltpu

F32 = jnp.float32
BF16 = jnp.bfloat16
HIGHEST = lax.Precision.HIGHEST

NORM_EPS = 1e-6
RW_GN_EPS = 64e-5
RW_HEAD = 64
GD_HEAD = 128
LANES = 128
CHUNK = 64
GRID_W = 64
RANK_PAD = 128
VMEM_LIMIT = 56 * 1024 * 1024


def _params(sem):
    return pltpu.CompilerParams(dimension_semantics=sem, vmem_limit_bytes=VMEM_LIMIT)


def _tile(n, pref, mult=8):
    if n <= pref:
        return n
    t = (pref // mult) * mult
    while t >= mult:
        if n % t == 0:
            return t
        t -= mult
    return n


def _mm(a, b):
    return jnp.dot(a.astype(BF16), b.astype(BF16), preferred_element_type=F32)


def _mm_nt(a, b):
    return lax.dot_general(a.astype(BF16), b.astype(BF16), (((1,), (1,)), ((), ())),
                           preferred_element_type=F32)


def _mm_tn(a, b):
    return lax.dot_general(a.astype(BF16), b.astype(BF16), (((0,), (0,)), ((), ())),
                           preferred_element_type=F32)


def _mm_hi(a, b):
    return jnp.dot(a, b, precision=HIGHEST, preferred_element_type=F32)


def _split3(x):
    x1 = x.astype(BF16)
    r1 = x - x1.astype(F32)
    x2 = r1.astype(BF16)
    x3 = (r1 - x2.astype(F32)).astype(BF16)
    return x1, x2, x3


def _mm_sel_left(c, x):
    cb = c.astype(BF16)
    return jnp.dot(jnp.concatenate([cb, cb, cb], axis=1), jnp.concatenate(_split3(x), axis=0),
                   preferred_element_type=F32)


def _mm_sel_right(x, c):
    cb = c.astype(BF16)
    return jnp.dot(jnp.concatenate(_split3(x), axis=1), jnp.concatenate([cb, cb, cb], axis=0),
                   preferred_element_type=F32)


def _softplus(x):
    return jnp.maximum(x, 0.0) + jnp.log(1.0 + jnp.exp(-jnp.abs(x)))


def _sigmoid(x):
    return 1.0 / (1.0 + jnp.exp(-x))


def _seg_ones(width):
    i = lax.broadcasted_iota(jnp.int32, (LANES, LANES), 0) // width
    j = lax.broadcasted_iota(jnp.int32, (LANES, LANES), 1) // width
    return (i == j).astype(F32)


def _seg_sum(x, width):
    e = _seg_ones(width)
    n = x.shape[-1] // LANES
    parts = [_mm_sel_right(x[:, g * LANES:(g + 1) * LANES], e) for g in range(n)]
    return parts[0] if n == 1 else jnp.concatenate(parts, axis=-1)


def _tri(rev):
    i = lax.broadcasted_iota(jnp.int32, (CHUNK, CHUNK), 0)
    j = lax.broadcasted_iota(jnp.int32, (CHUNK, CHUNK), 1)
    return ((j >= i) if rev else (j <= i)).astype(F32)


def _level_masks(i, j):
    masks = [(i // 2) == (j // 2)]
    s = 2
    while s < CHUNK:
        masks.append(((i // (2 * s)) == (j // (2 * s))) & ((i // s) != (j // s)))
        s *= 2
    return masks


def _shift_rows(x, prev_row, next_row):
    n = x.shape[0]
    row = lax.broadcasted_iota(jnp.int32, x.shape, 0)
    xm1 = jnp.where(row == 0, prev_row, pltpu.roll(x, 1, 0))
    xp1 = jnp.where(row == n - 1, next_row, pltpu.roll(x, n - 1, 0))
    return xm1, xp1


def _mod_kernel(c_ref, w_ref, b_ref, o_ref):
    c = c_ref[...]
    s = c * _sigmoid(c)
    o_ref[...] = _mm_hi(s, w_ref[...]) + b_ref[...]


def _mod(cc, w_ada, b_ada):
    rows, d = cc.shape
    n = w_ada.shape[1]
    tn = _tile(n, 1024, LANES)
    return pl.pallas_call(
        _mod_kernel,
        grid=(n // tn,),
        in_specs=[pl.BlockSpec((rows, d), lambda j: (0, 0)),
                  pl.BlockSpec((d, tn), lambda j: (0, j)),
                  pl.BlockSpec((1, tn), lambda j: (0, j))],
        out_specs=pl.BlockSpec((rows, tn), lambda j: (0, j)),
        out_shape=jax.ShapeDtypeStruct((rows, n), F32),
        compiler_params=_params(("arbitrary",)),
        name="mod",
    )(cc, w_ada, b_ada)


def _normproj_kernel(x_ref, mod_ref, g_ref, w_ref, o_ref, h_ref, *, sh_row, sc_row):
    @pl.when(pl.program_id(1) == 0)
    def _():
        x = x_ref[...]
        ms = jnp.mean(x * x, axis=-1, keepdims=True)
        y = x * lax.rsqrt(ms + NORM_EPS) * g_ref[...]
        sh = mod_ref[0, sh_row:sh_row + 1, :]
        sc = mod_ref[0, sc_row:sc_row + 1, :]
        h_ref[...] = (y * (1.0 + sc) + sh).astype(BF16)

    o_ref[...] = jnp.dot(h_ref[...], w_ref[...], preferred_element_type=F32)


def _normproj(x2, mod, gain, w, rows_per_mod, n_cols, sh_row, sc_row, name):
    m, d = x2.shape
    tm = _tile(rows_per_mod, 1024)
    tn = _tile(n_cols, 1024, LANES)
    per = rows_per_mod // tm
    return pl.pallas_call(
        functools.partial(_normproj_kernel, sh_row=sh_row, sc_row=sc_row),
        grid=(m // tm, n_cols // tn),
        in_specs=[pl.BlockSpec((tm, d), lambda i, j: (i, 0)),
                  pl.BlockSpec((1, 8, d), lambda i, j: (i // per, 0, 0)),
                  pl.BlockSpec((1, d), lambda i, j: (0, 0)),
                  pl.BlockSpec((d, tn), lambda i, j: (0, j))],
        out_specs=pl.BlockSpec((tm, tn), lambda i, j: (i, j)),
        out_shape=jax.ShapeDtypeStruct((m, n_cols), F32),
        scratch_shapes=[pltpu.VMEM((tm, d), BF16)],
        compiler_params=_params(("parallel", "arbitrary")),
        name=name,
    )(x2, mod, gain, w)


def _rwprep_kernel(pc_ref, pp_ref, pn_ref, mu_ref, kk_ref, ka_ref, rk_ref, w0_ref, wup_ref,
                   a0_ref, aup_ref, gup_ref,
                   at_ref, bt_ref, kt_ref, rt_ref, v_ref, pt_ref, bonus_ref, g_ref, *, tiles_per_seq):
    i = pl.program_id(0)
    tm = pc_ref.shape[0]
    w = RW_HEAD * (kk_ref.shape[1] // RW_HEAD)
    first = (i % tiles_per_seq) == 0
    last = (i % tiles_per_seq) == tiles_per_seq - 1
    base = 3 * w
    used = base + 4 * RANK_PAD + gup_ref.shape[0]
    x = pc_ref[:, 0:used]
    prev_row = jnp.where(first, 0.0, pp_ref[7:8, 0:used])
    next_row = jnp.where(last, 0.0, pn_ref[0:1, 0:used])
    xm1, xp1 = _shift_rows(x, prev_row, next_row)
    xs = x + mu_ref[:, 0:used] * (0.5 * (xm1 + xp1) - x)

    r = xs[:, 0:w]
    k = xs[:, w:2 * w]
    v = xs[:, 2 * w:3 * w]
    gd = xs[:, base + 4 * RANK_PAD:used]
    g_ref[...] = _mm(_sigmoid(gd), gup_ref[...])
    v_ref[...] = v.astype(BF16)

    kx = k * kk_ref[...]
    kk = kx * lax.rsqrt(jnp.maximum(_seg_sum(kx * kx, RW_HEAD), 1e-12))

    ksum = jnp.zeros_like(k)
    for d in range(2):
        wd = xs[:, base + d * RANK_PAD: base + (d + 1) * RANK_PAD]
        ad = xs[:, base + (2 + d) * RANK_PAD: base + (3 + d) * RANK_PAD]
        wl = w0_ref[d:d + 1, :] + _mm(jnp.tanh(wd), wup_ref[d])
        lw = -math.exp(-0.5) * _sigmoid(wl)
        a = _sigmoid(a0_ref[d:d + 1, :] + _mm(ad, aup_ref[d]))
        kd = k * (1.0 + (a - 1.0) * ka_ref[...])
        ksum = ksum + kd
        tri = _tri(rev=(d == 1))
        for c in range(tm // CHUNK):
            rows = slice(c * CHUNK, (c + 1) * CHUNK)
            lwc = lw[rows]
            cum = _mm_sel_left(tri, lwc)
            p_in = jnp.exp(cum)
            p_inv = jnp.exp(-cum)
            p_ex = jnp.exp(cum - lwc)
            at_ref[d, rows, :] = (-kk[rows] * p_ex).astype(BF16)
            bt_ref[d, rows, :] = (kk[rows] * a[rows] * p_inv).astype(BF16)
            kt_ref[d, rows, :] = (kd[rows] * p_inv).astype(BF16)
            rt_ref[d, rows, :] = (r[rows] * p_in).astype(BF16)
            tot = cum[CHUNK - 1:CHUNK] if d == 0 else cum[0:1]
            pt_ref[d, c, :, :] = jnp.exp(tot)
    bonus_ref[...] = _seg_sum(r * ksum * rk_ref[...], RW_HEAD) * v


def _rwprep(p, seq_len, wts):
    m = p.shape[0]
    mu, k_k, k_a, r_k, w0, w_up, a0, a_up, g_up = wts
    w = k_k.shape[1]
    blk = mu.shape[1]
    tm = _tile(seq_len, 256, CHUNK)
    tps = seq_len // tm
    nb8 = m // 8
    full = lambda a: pl.BlockSpec(a.shape, lambda i: (0,) * a.ndim)
    feat = jax.ShapeDtypeStruct((2, m, w), BF16)
    feat_spec = pl.BlockSpec((2, tm, w), lambda i: (0, i, 0))
    row_spec = pl.BlockSpec((tm, w), lambda i: (i, 0))
    return pl.pallas_call(
        functools.partial(_rwprep_kernel, tiles_per_seq=tps),
        grid=(m // tm,),
        in_specs=[pl.BlockSpec((tm, blk), lambda i: (i, 0)),
                  pl.BlockSpec((8, blk), lambda i: (jnp.maximum(i * (tm // 8) - 1, 0), 0)),
                  pl.BlockSpec((8, blk), lambda i: (jnp.minimum((i + 1) * (tm // 8), nb8 - 1), 0)),
                  full(mu), full(k_k), full(k_a), full(r_k), full(w0), full(w_up), full(a0),
                  full(a_up), full(g_up)],
        out_specs=[feat_spec, feat_spec, feat_spec, feat_spec, row_spec,
                   pl.BlockSpec((2, tm // CHUNK, 1, w), lambda i: (0, i, 0, 0)),
                   row_spec, row_spec],
        out_shape=[feat, feat, feat, feat, jax.ShapeDtypeStruct((m, w), BF16),
                   jax.ShapeDtypeStruct((2, m // CHUNK, 1, w), F32),
                   jax.ShapeDtypeStruct((m, w), F32), jax.ShapeDtypeStruct((m, w), F32)],
        compiler_params=_params(("parallel",)),
        name="rwprep",
    )(p, p, p, mu, k_k, k_a, r_k, w0, w_up, a0, a_up, g_up)


def _pair_blockdiag(x, m0):
    zero = jnp.zeros_like(x)
    return jnp.concatenate([jnp.where(m0, x, zero), jnp.where(m0, zero, x)], axis=0)


def _tri_inverse(a, eye, blk, m0, sign):
    ts = [eye + sign * jnp.where(blk[0], x, 0.0) for x in a]
    for lvl in range(1, len(blk)):
        xs = [_mm(t, _pair_blockdiag(jnp.where(blk[lvl], x, 0.0), m0)) for t, x in zip(ts, a)]
        ts = [t + sign * _mm(x, _pair_blockdiag(t, m0)) for t, x in zip(ts, xs)]
    return ts


def _pair_masks(head_cols):
    i = lax.broadcasted_iota(jnp.int32, (CHUNK, LANES), 0)
    lane = lax.broadcasted_iota(jnp.int32, (CHUNK, LANES), 1)
    j = lane % head_cols
    strict = (j < i, j > i)
    incl = (j <= i, j >= i)
    eye = (j == i).astype(F32)
    return strict, incl, eye, _level_masks(i, j), lane < head_cols


def _rwscan_kernel(*refs, want_out):
    ins, rest = refs[:12], refs[12:]
    s0_ref = rest[0]
    if want_out:
        o_refs, s_ref, h_ref = rest[1:3], rest[3], rest[4]
    else:
        s_ref, h_ref = rest[1], rest[2]
    c = pl.program_id(1)
    n_pairs = h_ref.shape[1]

    @pl.when(c == 0)
    def _():
        h_ref[...] = s0_ref[0]

    strict, incl, eye, blk, m0 = _pair_masks(RW_HEAD)
    r2 = lax.broadcasted_iota(jnp.int32, (LANES, LANES), 0) // RW_HEAD
    c2 = lax.broadcasted_iota(jnp.int32, (LANES, LANES), 1) // RW_HEAD
    diag2 = r2 == c2

    chains = [(d, p) for d in range(2) for p in range(n_pairs)]
    cols = lambda p: slice(p * LANES, (p + 1) * LANES)
    at = [ins[6 * d + 0][0, :, cols(p)] for d, p in chains]
    bt = [ins[6 * d + 1][0, :, cols(p)] for d, p in chains]
    kt = [ins[6 * d + 2][0, :, cols(p)] for d, p in chains]
    rt = [ins[6 * d + 3][0, :, cols(p)] for d, p in chains]
    v = [ins[6 * d + 4][:, cols(p)] for d, p in chains]
    pt = [ins[6 * d + 5][0, 0, :, cols(p)] for d, p in chains]
    n = len(chains)
    bd = lambda x: _pair_blockdiag(x, m0)

    s4 = [_mm_nt(jnp.concatenate([at[i], rt[i]], axis=0),
                 jnp.concatenate([bd(bt[i]), bd(kt[i])], axis=0)) for i in range(n)]
    a_ab = [jnp.where(strict[chains[i][0]], s4[i][:CHUNK, :LANES], 0.0) for i in range(n)]
    a_ak = [jnp.where(strict[chains[i][0]], s4[i][:CHUNK, LANES:], 0.0) for i in range(n)]
    t = _tri_inverse(a_ab, eye, blk, m0, 1.0)
    av = [_mm(a_ak[i], bd(v[i])) for i in range(n)]
    wu = [_mm(t[i], jnp.concatenate([bd(at[i]), bd(av[i].astype(BF16))], axis=1)) for i in range(n)]

    ht = [h_ref[d, p] for d, p in chains]
    if want_out:
        m1 = [_mm_nt(jnp.concatenate([wu[i][:, :LANES].astype(BF16), rt[i]], axis=0), ht[i])
              for i in range(n)]
        u = [m1[i][:CHUNK] + wu[i][:, LANES:] for i in range(n)]
    else:
        u = [_mm_nt(wu[i][:, :LANES], ht[i]) + wu[i][:, LANES:] for i in range(n)]
    ub = [x.astype(BF16) for x in u]
    if want_out:
        for i, (d, p) in enumerate(chains):
            m_r = jnp.where(jnp.concatenate([incl[d], incl[d]], axis=1), s4[i][CHUNK:], 0.0)
            o = m1[i][CHUNK:] + _mm(m_r, jnp.concatenate([bd(ub[i]), bd(v[i])], axis=0))
            o_refs[d][:, cols(p)] = o
    for i, (d, p) in enumerate(chains):
        upd = _mm_tn(jnp.concatenate([ub[i], v[i]], axis=0),
                     jnp.concatenate([bt[i], kt[i]], axis=0))
        h_ref[d, p] = (ht[i] + jnp.where(diag2, upd, 0.0)) * pt[i]

    @pl.when(c == pl.num_programs(1) - 1)
    def _():
        s_ref[0] = h_ref[...]


def _rwscan(at, bt, kt, rt, v, pt, s0, batch, want_out):
    m, w = v.shape
    n_chunks = m // batch // CHUNK
    n_pairs = w // LANES
    rows = (lambda b, c: b * n_chunks + c, lambda b, c: b * n_chunks + n_chunks - 1 - c)

    in_specs, args = [], []
    for d in range(2):
        feat_spec = pl.BlockSpec((1, CHUNK, w), lambda b, c, d=d: (d, rows[d](b, c), 0))
        in_specs += [feat_spec] * 4
        in_specs += [pl.BlockSpec((CHUNK, w), lambda b, c, d=d: (rows[d](b, c), 0)),
                     pl.BlockSpec((1, 1, 1, w), lambda b, c, d=d: (d, rows[d](b, c), 0, 0))]
        args += [at, bt, kt, rt, v, pt]
    state_spec = pl.BlockSpec((1, 2, n_pairs, LANES, LANES), lambda b, c: (b, 0, 0, 0, 0))
    out_specs = [state_spec]
    out_shape = [jax.ShapeDtypeStruct(s0.shape, F32)]
    if want_out:
        out_specs = [pl.BlockSpec((CHUNK, w), lambda b, c, d=d: (rows[d](b, c), 0))
                     for d in range(2)] + out_specs
        out_shape = [jax.ShapeDtypeStruct((m, w), F32)] * 2 + out_shape
    return pl.pallas_call(
        functools.partial(_rwscan_kernel, want_out=want_out),
        grid=(batch, n_chunks),
        in_specs=in_specs + [state_spec],
        out_specs=out_specs,
        out_shape=out_shape,
        scratch_shapes=[pltpu.VMEM((2, n_pairs, LANES, LANES), F32)],
        compiler_params=_params(("parallel", "arbitrary")),
        name="rwscan_out" if want_out else "rwscan_state",
    )(*args, s0)


def _rwread_kernel(of_ref, ob_ref, bonus_ref, g_ref, gng_ref, gnb_ref, y_ref):
    o = of_ref[...] + ob_ref[...]
    inv = 1.0 / RW_HEAD
    mean = _seg_sum(o, RW_HEAD) * inv
    cen = o - mean
    var = _seg_sum(cen * cen, RW_HEAD) * inv
    on = cen * lax.rsqrt(var + RW_GN_EPS) * gng_ref[...] + gnb_ref[...]
    y_ref[...] = ((on + bonus_ref[...]) * g_ref[...]).astype(BF16)


def _rwread(o_f, o_b, bonus, g, gn_g, gn_b):
    m, w = o_f.shape
    tm = _tile(m, 512)
    row_spec = pl.BlockSpec((tm, w), lambda i: (i, 0))
    vec_spec = pl.BlockSpec((1, w), lambda i: (0, 0))
    return pl.pallas_call(
        _rwread_kernel,
        grid=(m // tm,),
        in_specs=[row_spec, row_spec, row_spec, row_spec, vec_spec, vec_spec],
        out_specs=row_spec,
        out_shape=jax.ShapeDtypeStruct((m, w), BF16),
        compiler_params=_params(("parallel",)),
        name="rwread",
    )(o_f, o_b, bonus, g, gn_g, gn_b)


def _gdprep_kernel(pc_ref, pp_ref, pn_ref, ab_ref, cw_ref, alog_ref, dtb_ref,
                   q_ref, k_ref, v_ref, gcum_ref, beta_ref, *, tiles_per_seq):
    i = pl.program_id(0)
    tm = pc_ref.shape[0]
    w = q_ref.shape[1]
    n_heads = w // GD_HEAD
    first = (i % tiles_per_seq) == 0
    last = (i % tiles_per_seq) == tiles_per_seq - 1
    x = pc_ref[:, 0:3 * w]
    prev_row = jnp.where(first, 0.0, pp_ref[7:8, 0:3 * w])
    next_row = jnp.where(last, 0.0, pn_ref[0:1, 0:3 * w])
    xm1, xp1 = _shift_rows(x, prev_row, next_row)
    y = xm1 * cw_ref[0:1, :] + x * cw_ref[1:2, :] + xp1 * cw_ref[2:3, :]
    y = y * _sigmoid(y)
    for h in range(n_heads):
        for part, ref, scale in ((0, q_ref, GD_HEAD ** -0.5), (1, k_ref, 1.0)):
            cols = slice(part * w + h * GD_HEAD, part * w + (h + 1) * GD_HEAD)
            t = y[:, cols]
            ss = jnp.sum(t * t, axis=-1, keepdims=True)
            ref[:, h * GD_HEAD:(h + 1) * GD_HEAD] = (
                t * (lax.rsqrt(jnp.maximum(ss, 1e-12)) * scale)).astype(BF16)
    v_ref[...] = y[:, 2 * w:3 * w].astype(BF16)

    ab = ab_ref[:, 6 * LANES:7 * LANES]
    a = ab[:, 0:2 * n_heads]
    b = ab[:, 2 * n_heads:4 * n_heads]
    glog = -jnp.exp(alog_ref[...]) * _softplus(a + dtb_ref[...])
    beta_ref[...] = _sigmoid(b)
    for c in range(tm // CHUNK):
        rows = slice(c * CHUNK, (c + 1) * CHUNK)
        gc = glog[rows]
        fwd = _mm_sel_left(_tri(False), gc)
        bwd = _mm_sel_left(_tri(True), gc)
        col = lax.broadcasted_iota(jnp.int32, gc.shape, 1)
        gcum_ref[rows, :] = jnp.where(col < n_heads, fwd, bwd)


def _gdprep(p, seq_len, conv_w, a_log, dt_bias, qkvz_block, ab_block, width):
    m = p.shape[0]
    blk = 4 * width
    tm = _tile(seq_len, 256, CHUNK)
    tps = seq_len // tm
    nb8 = m // 8
    n2h = a_log.shape[1]
    full = lambda a: pl.BlockSpec(a.shape, lambda i: (0,) * a.ndim)
    row_spec = pl.BlockSpec((tm, width), lambda i: (i, 0))
    small_spec = pl.BlockSpec((tm, n2h), lambda i: (i, 0))
    return pl.pallas_call(
        functools.partial(_gdprep_kernel, tiles_per_seq=tps),
        grid=(m // tm,),
        in_specs=[pl.BlockSpec((tm, blk), lambda i: (i, qkvz_block)),
                  pl.BlockSpec((8, blk), lambda i: (jnp.maximum(i * (tm // 8) - 1, 0), qkvz_block)),
                  pl.BlockSpec((8, blk),
                               lambda i: (jnp.minimum((i + 1) * (tm // 8), nb8 - 1), qkvz_block)),
                  pl.BlockSpec((tm, 8 * LANES), lambda i: (i, ab_block)),
                  full(conv_w), full(a_log), full(dt_bias)],
        out_specs=[row_spec, row_spec, row_spec, small_spec, small_spec],
        out_shape=[jax.ShapeDtypeStruct((m, width), BF16)] * 3
        + [jax.ShapeDtypeStruct((m, n2h), F32)] * 2,
        compiler_params=_params(("parallel",)),
        name="gdprep",
    )(p, p, p, p, conv_w, a_log, dt_bias)


def _gdscan_kernel(*refs, want_out):
    ins, rest = refs[:12], refs[12:]
    s0_ref = rest[0]
    if want_out:
        o_refs, s_ref, st_ref = rest[1:3], rest[3], rest[4]
    else:
        s_ref, st_ref = rest[1], rest[2]
    c = pl.program_id(1)
    n_pairs = st_ref.shape[1] // 2

    @pl.when(c == 0)
    def _():
        st_ref[...] = s0_ref[0]

    strict, incl, eye, blk, m0 = _pair_masks(CHUNK)
    bd = lambda x: _pair_blockdiag(x, m0)
    chains = [(d, p) for d in range(2) for p in range(n_pairs)]
    n = len(chains)
    hcols = lambda h: slice(h * GD_HEAD, (h + 1) * GD_HEAD)

    def head_vals(d, p, e):
        h = 2 * p + e
        q_ref, k_ref, v_ref, g_ref, _, beta_ref = ins[6 * d:6 * d + 6]
        return (q_ref[:, hcols(h)].astype(F32), k_ref[:, hcols(h)].astype(F32),
                v_ref[:, hcols(h)].astype(F32), g_ref[0, :, h:h + 1], beta_ref[0, :, h:h + 1])

    hv = [[head_vals(d, p, e) for e in range(2)] for d, p in chains]
    kb = [[hv[i][e][1] * hv[i][e][4] for e in range(2)] for i in range(n)]
    eg = [[jnp.exp(hv[i][e][3]) for e in range(2)] for i in range(n)]
    decay, s2 = [], []
    for i, (d, p) in enumerate(chains):
        gcol = jnp.where(m0, hv[i][0][3], hv[i][1][3])
        grow = ins[6 * d + 4][0, 0, p:p + 1, :]
        decay.append(jnp.where(incl[d], jnp.exp(jnp.where(incl[d], gcol - grow, 0.0)), 0.0))
        k0, k1 = hv[i][0][1], hv[i][1][1]
        zero = jnp.zeros_like(k0)
        lhs = jnp.concatenate([jnp.concatenate([kb[i][0], kb[i][1]], axis=1),
                               jnp.concatenate([hv[i][0][0], hv[i][1][0]], axis=1)], axis=0)
        rhs = jnp.concatenate([jnp.concatenate([k0, zero], axis=1),
                               jnp.concatenate([zero, k1], axis=1)], axis=0)
        s2.append(_mm_nt(lhs, rhs))
    a = [jnp.where(strict[chains[i][0]], s2[i][:CHUNK] * decay[i], 0.0) for i in range(n)]
    t = _tri_inverse(a, eye, blk, m0, -1.0)
    sol = [_mm(bd(t[i]), jnp.concatenate(
        [jnp.concatenate([hv[i][e][2] * hv[i][e][4], kb[i][e] * eg[i][e]], axis=1)
         for e in range(2)], axis=0)) for i in range(n)]

    st = [[st_ref[d, 2 * p + e] for e in range(2)] for d, p in chains]
    ws = [[_mm(jnp.concatenate([sol[i][e * CHUNK:(e + 1) * CHUNK, GD_HEAD:],
                                hv[i][e][0] * eg[i][e]], axis=0), st[i][e])
           for e in range(2)] for i in range(n)]
    v_new = [[sol[i][e * CHUNK:(e + 1) * CHUNK, :GD_HEAD] - ws[i][e][:CHUNK] for e in range(2)]
             for i in range(n)]
    if want_out:
        for i, (d, p) in enumerate(chains):
            intra = _mm(bd(s2[i][CHUNK:] * decay[i]),
                        jnp.concatenate([v_new[i][0], v_new[i][1]], axis=0))
            for e in range(2):
                o_refs[d][:, hcols(2 * p + e)] = (ws[i][e][CHUNK:]
                                                  + intra[e * CHUNK:(e + 1) * CHUNK])
    for i, (d, p) in enumerate(chains):
        for e in range(2):
            gcol = hv[i][e][3]
            g_last = jnp.min(gcol, axis=0, keepdims=True)
            k_dec = hv[i][e][1] * jnp.exp(g_last - gcol)
            st_ref[d, 2 * p + e] = st[i][e] * jnp.exp(g_last) + _mm_tn(k_dec, v_new[i][e])

    @pl.when(c == pl.num_programs(1) - 1)
    def _():
        s_ref[0] = st_ref[...]


def _gdscan(q, k, v, g, gt, beta, s0, batch, want_out):
    m, w = q.shape
    n_chunks = m // batch // CHUNK
    n_heads = w // GD_HEAD
    rows = (lambda b, c: b * n_chunks + c, lambda b, c: b * n_chunks + n_chunks - 1 - c)

    in_specs, args = [], []
    for d in range(2):
        row_spec = pl.BlockSpec((CHUNK, w), lambda b, c, d=d: (rows[d](b, c), 0))
        col_spec = pl.BlockSpec((1, CHUNK, n_heads), lambda b, c, d=d: (d, rows[d](b, c), 0))
        in_specs += [row_spec, row_spec, row_spec, col_spec,
                     pl.BlockSpec((1, 1, n_heads // 2, LANES),
                                  lambda b, c, d=d: (d, rows[d](b, c), 0, 0)),
                     col_spec]
        args += [q, k, v, g, gt, beta]
    state_spec = pl.BlockSpec((1, 2, n_heads, GD_HEAD, GD_HEAD), lambda b, c: (b, 0, 0, 0, 0))
    out_specs = [state_spec]
    out_shape = [jax.ShapeDtypeStruct(s0.shape, F32)]
    if want_out:
        out_specs = [pl.BlockSpec((CHUNK, w), lambda b, c, d=d: (rows[d](b, c), 0))
                     for d in range(2)] + out_specs
        out_shape = [jax.ShapeDtypeStruct((m, w), F32)] * 2 + out_shape
    return pl.pallas_call(
        functools.partial(_gdscan_kernel, want_out=want_out),
        grid=(batch, n_chunks),
        in_specs=in_specs + [state_spec],
        out_specs=out_specs,
        out_shape=out_shape,
        scratch_shapes=[pltpu.VMEM((2, n_heads, GD_HEAD, GD_HEAD), F32)],
        compiler_params=_params(("parallel", "arbitrary")),
        name="gdscan_out" if want_out else "gdscan_state",
    )(*args, s0)


def _gdread_kernel(of_ref, ob_ref, z_ref, ng_ref, y_ref):
    o = of_ref[...] + ob_ref[...]
    z = z_ref[...]
    gate = z * _sigmoid(z)
    for h in range(o.shape[1] // GD_HEAD):
        cols = slice(h * GD_HEAD, (h + 1) * GD_HEAD)
        oh = o[:, cols]
        ms = jnp.mean(oh * oh, axis=-1, keepdims=True)
        y_ref[:, cols] = (oh * lax.rsqrt(ms + NORM_EPS) * ng_ref[...] * gate[:, cols]).astype(BF16)


def _gdread(o_f, o_b, p, z_block, norm_g):
    m, w = o_f.shape
    tm = _tile(m, 512)
    return pl.pallas_call(
        _gdread_kernel,
        grid=(m // tm,),
        in_specs=[pl.BlockSpec((tm, w), lambda i: (i, 0)),
                  pl.BlockSpec((tm, w), lambda i: (i, 0)),
                  pl.BlockSpec((tm, w), lambda i: (i, z_block)),
                  pl.BlockSpec((1, GD_HEAD), lambda i: (0, 0))],
        out_specs=pl.BlockSpec((tm, w), lambda i: (i, 0)),
        out_shape=jax.ShapeDtypeStruct((m, w), BF16),
        compiler_params=_params(("parallel",)),
        name="gdread",
    )(o_f, o_b, p, norm_g)


def _merge1_kernel(ya_ref, yb_ref, wa_ref, wb_ref, ga_ref, gb_ref, o_ref):
    a = jnp.dot(ya_ref[...], wa_ref[...], preferred_element_type=F32)
    b = jnp.dot(yb_ref[...], wb_ref[...], preferred_element_type=F32)
    o_ref[...] = (_sigmoid(ga_ref[...]) * a + _sigmoid(gb_ref[...]) * b).astype(BF16)


def _merge1(ya, yb, wa, wb, p, gate_col0):
    m, ka = ya.shape
    kb = yb.shape[1]
    d = wa.shape[1]
    tm = _tile(m, 1024)
    tn = _tile(d, 1024, LANES)
    ga0 = gate_col0 // tn
    gb0 = (gate_col0 + d) // tn
    return pl.pallas_call(
        _merge1_kernel,
        grid=(m // tm, d // tn),
        in_specs=[pl.BlockSpec((tm, ka), lambda i, j: (i, 0)),
                  pl.BlockSpec((tm, kb), lambda i, j: (i, 0)),
                  pl.BlockSpec((ka, tn), lambda i, j: (0, j)),
                  pl.BlockSpec((kb, tn), lambda i, j: (0, j)),
                  pl.BlockSpec((tm, tn), lambda i, j: (i, ga0 + j)),
                  pl.BlockSpec((tm, tn), lambda i, j: (i, gb0 + j))],
        out_specs=pl.BlockSpec((tm, tn), lambda i, j: (i, j)),
        out_shape=jax.ShapeDtypeStruct((m, d), BF16),
        compiler_params=_params(("parallel", "arbitrary")),
        name="merge1",
    )(ya, yb, wa, wb, p, p)


def _merge2_kernel(mg_ref, wo_ref, x_ref, mod_ref, o_ref, *, gate_row):
    y = jnp.dot(mg_ref[...], wo_ref[...], preferred_element_type=F32)
    o_ref[...] = x_ref[...] + mod_ref[0, gate_row:gate_row + 1, :] * y


def _merge2(merged, wo, x2, mod, rows_per_mod, gate_row):
    m, d = x2.shape
    tm = _tile(rows_per_mod, 1024)
    tn = _tile(d, 1024, LANES)
    per = rows_per_mod // tm
    return pl.pallas_call(
        functools.partial(_merge2_kernel, gate_row=gate_row),
        grid=(m // tm, d // tn),
        in_specs=[pl.BlockSpec((tm, d), lambda i, j: (i, 0)),
                  pl.BlockSpec((d, tn), lambda i, j: (0, j)),
                  pl.BlockSpec((tm, tn), lambda i, j: (i, j)),
                  pl.BlockSpec((1, 8, tn), lambda i, j: (i // per, 0, j))],
        out_specs=pl.BlockSpec((tm, tn), lambda i, j: (i, j)),
        out_shape=jax.ShapeDtypeStruct((m, d), F32),
        compiler_params=_params(("parallel", "arbitrary")),
        name="merge2",
    )(merged, wo, x2, mod)


def _convglu_lines(gate_ref, val_ref, cw_ref, o_ref, slot):
    n, tf = o_ref.shape
    n_lines = n // GRID_W
    col = lax.broadcasted_iota(jnp.int32, (GRID_W, LANES), 0)
    first_col = col == 0
    last_col = col == GRID_W - 1
    groups = [slice(c0, c0 + LANES) for c0 in range(0, tf, LANES)]
    w = [[cw_ref[k:k + 1, lanes] for k in range(9)] for lanes in groups]
    acc = [{} for _ in groups]

    def feed(line):
        for gi, lanes in enumerate(groups):
            if line < n_lines:
                g = gate_ref[slot, line * GRID_W:(line + 1) * GRID_W, lanes]
                left = jnp.where(first_col, 0.0, pltpu.roll(g, 1, 0))
                right = jnp.where(last_col, 0.0, pltpu.roll(g, GRID_W - 1, 0))
                for kh in range(3):
                    dst = line + 1 - kh
                    if 0 <= dst < n_lines:
                        t = (left * w[gi][3 * kh] + g * w[gi][3 * kh + 1]
                             + right * w[gi][3 * kh + 2])
                        acc[gi][dst] = acc[gi][dst] + t if dst in acc[gi] else t
            done = line - 1
            if done >= 0:
                a = acc[gi].pop(done)
                rows = slice(done * GRID_W, (done + 1) * GRID_W)
                gelu = 0.5 * a * (1.0 + lax.erf(a * (2.0 ** -0.5)))
                o_ref[rows, lanes] = (gelu * val_ref[slot, rows, lanes]).astype(BF16)

    return feed, n_lines + 1


def _ffn_act_kernel(x_ref, mod_ref, g_ref, w1g_ref, w1v_ref, cw_ref, o_ref, h_ref, gate_ref, val_ref,
                    *, sh_row, sc_row, prologue_rows, dot_rows):
    j = pl.program_id(1)

    @pl.when(j == 0)
    def _():
        sh = mod_ref[0, sh_row:sh_row + 1, :]
        sc = mod_ref[0, sc_row:sc_row + 1, :]

        def body(r, carry):
            rows = pl.ds(pl.multiple_of(r * prologue_rows, prologue_rows), prologue_rows)
            x = x_ref[rows, :]
            ms = jnp.mean(x * x, axis=-1, keepdims=True)
            y = x * lax.rsqrt(ms + NORM_EPS) * g_ref[...]
            h_ref[rows, :] = (y * (1.0 + sc) + sh).astype(BF16)
            return carry

        lax.fori_loop(0, x_ref.shape[0] // prologue_rows, body, 0)
        gate_ref[1] = jnp.zeros(gate_ref.shape[1:], F32)
        val_ref[1] = jnp.zeros(val_ref.shape[1:], F32)

    slot = j % 2
    feed, n_feeds = _convglu_lines(gate_ref, val_ref, cw_ref, o_ref, 1 - slot)
    for line in range(n_feeds):
        feed(line)
    w1g = w1g_ref[...].astype(BF16)
    w1v = w1v_ref[...].astype(BF16)
    for r0 in range(0, h_ref.shape[0], dot_rows):
        rows = slice(r0, r0 + dot_rows)
        hh = h_ref[rows, :]
        gate_ref[slot, rows, :] = jnp.dot(hh, w1g, preferred_element_type=F32)
        val_ref[slot, rows, :] = jnp.dot(hh, w1v, preferred_element_type=F32)


def _ffn_act(x1, mod, gain, w1, conv_w, batch, sh_row, sc_row):
    m, d = x1.shape
    dff = conv_w.shape[1]
    seq = m // batch
    tf = _tile(dff, 256, LANES)
    nblk = dff // tf
    cur = lambda j: jnp.minimum(j, nblk - 1)
    prev = lambda j: jnp.maximum(j - 1, 0)
    return pl.pallas_call(
        functools.partial(_ffn_act_kernel, sh_row=sh_row, sc_row=sc_row,
                          prologue_rows=_tile(seq, 256), dot_rows=_tile(seq, 1024)),
        grid=(batch, nblk + 1),
        in_specs=[pl.BlockSpec((seq, d), lambda b, j: (b, 0), pipeline_mode=pl.Buffered(1)),
                  pl.BlockSpec((1, 8, d), lambda b, j: (b, 0, 0)),
                  pl.BlockSpec((1, d), lambda b, j: (0, 0)),
                  pl.BlockSpec((d, tf), lambda b, j: (0, cur(j))),
                  pl.BlockSpec((d, tf), lambda b, j: (0, nblk + cur(j))),
                  pl.BlockSpec((conv_w.shape[0], tf), lambda b, j: (0, prev(j)))],
        out_specs=pl.BlockSpec((seq, tf), lambda b, j: (b, prev(j))),
        out_shape=jax.ShapeDtypeStruct((m, dff), BF16),
        scratch_shapes=[pltpu.VMEM((seq, d), BF16),
                        pltpu.VMEM((2, seq, tf), F32),
                        pltpu.VMEM((2, seq, tf), F32)],
        compiler_params=_params(("parallel", "arbitrary")),
        name="ffn_act",
    )(x1, mod, gain, w1, w1, conv_w)


def _ffn_out_kernel(act_ref, w2_ref, x_ref, mod_ref, g_ref, o_ref, acc_ref, *, gate_row):
    kstep = pl.program_id(1)

    @pl.when(kstep == 0)
    def _():
        acc_ref[...] = jnp.zeros_like(acc_ref)

    acc_ref[...] += jnp.dot(act_ref[...], w2_ref[...], preferred_element_type=F32)

    @pl.when(kstep == pl.num_programs(1) - 1)
    def _():
        y = x_ref[...] + mod_ref[0, gate_row:gate_row + 1, :] * acc_ref[...]
        ms = jnp.mean(y * y, axis=-1, keepdims=True)
        o_ref[...] = y * lax.rsqrt(ms + NORM_EPS) * g_ref[...]


def _ffn_out(act, w2, x1, mod, final_g, rows_per_mod, gate_row):
    m, d = x1.shape
    dff = act.shape[1]
    tm = _tile(rows_per_mod, 512)
    tk = _tile(dff, 1408, LANES)
    per = rows_per_mod // tm
    return pl.pallas_call(
        functools.partial(_ffn_out_kernel, gate_row=gate_row),
        grid=(m // tm, dff // tk),
        in_specs=[pl.BlockSpec((tm, tk), lambda i, k: (i, k)),
                  pl.BlockSpec((tk, d), lambda i, k: (k, 0)),
                  pl.BlockSpec((tm, d), lambda i, k: (i, 0)),
                  pl.BlockSpec((1, 8, d), lambda i, k: (i // per, 0, 0)),
                  pl.BlockSpec((1, d), lambda i, k: (0, 0))],
        out_specs=pl.BlockSpec((tm, d), lambda i, k: (i, 0)),
        out_shape=jax.ShapeDtypeStruct((m, d), F32),
        scratch_shapes=[pltpu.VMEM((tm, d), F32)],
        compiler_params=_params(("parallel", "arbitrary")),
        name="ffn_out",
    )(act, w2, x1, mod, final_g)


def _pad_cols(a, width):
    return jnp.pad(a, [(0, 0)] * (a.ndim - 1) + [(0, width - a.shape[-1])])


def _pad_rank(a):
    return jnp.pad(a, [(0, 0)] * (a.ndim - 2) + [(0, RANK_PAD - a.shape[-2]), (0, 0)])


def kernel(x, c, ctx, c_ctx, w_ada, b_ada, norm1_g, norm2_g, w_in, rw_mu, rw_k_k, rw_k_a, rw_r_k, rw_w0, rw_w_up, rw_a0, rw_a_up, rw_g_up, rw_gn_g, rw_gn_b, gd_conv_w, gd_a_log, gd_dt_bias, gd_norm_g, w_a_out, w_b_out, w_o, ffn_w1, ffn_conv_w, ffn_w2, final_norm_g):
    batch, seq, d = x.shape
    ctx_len = ctx.shape[1]
    assert w_ada.shape[0] == 1, "single layer only"
    rw_w = rw_k_k.shape[1]
    gd_w = w_b_out.shape[1]
    dec_rank = rw_w_up.shape[2]
    icl_rank = rw_a_up.shape[2]
    gate_rank = rw_g_up.shape[1]
    gd_heads = gd_a_log.shape[2]
    assert max(dec_rank, icl_rank) <= RANK_PAD and 4 * gd_heads <= 2 * LANES
    assert seq % CHUNK == 0 and ctx_len % CHUNK == 0 and seq % GRID_W == 0
    assert 3 * rw_w == 3 * gd_w and gate_rank <= 2 * LANES

    low_w = 8 * LANES
    blk0 = 3 * rw_w + low_w
    assert blk0 == 4 * gd_w
    wi = w_in[0]
    o_rw = 3 * rw_w
    o_gd = o_rw + 2 * dec_rank + 2 * icl_rank + gate_rank
    o_ab = o_gd + 4 * gd_w
    o_gate = o_ab + 4 * gd_heads

    def pack_cols(a):
        pieces = [a[..., :o_rw]]
        off = o_rw
        for r in (dec_rank, dec_rank, icl_rank, icl_rank):
            pieces.append(_pad_cols(a[..., off:off + r], RANK_PAD))
            off += r
        pieces.append(_pad_cols(a[..., off:off + gate_rank], 2 * LANES))
        return pieces

    w_pack = jnp.concatenate(
        pack_cols(wi) + [_pad_cols(wi[:, o_ab:o_gate], 2 * LANES), wi[:, o_gd:o_ab], wi[:, o_gate:]],
        axis=1).astype(BF16)
    n_ctx_cols = 2 * blk0
    gate_col0 = 2 * blk0
    mu_pack = jnp.concatenate(pack_cols(rw_mu) + [jnp.zeros((1, 2 * LANES), F32)], axis=1)

    cc = jnp.concatenate([c, c_ctx[None, :], jnp.zeros((16 - batch - 1, d), F32)], axis=0)
    mods = _mod(cc, w_ada[0], b_ada)
    mod_lat = _pad_rows8(mods[:batch].reshape(batch, 6, d))
    mod_ctx = _pad_rows8(mods[batch:batch + 1].reshape(1, 6, d))

    x2 = x.reshape(batch * seq, d)
    ctx2 = ctx.reshape(batch * ctx_len, d)
    p_lat = _normproj(x2, mod_lat, norm1_g, w_pack, seq, w_pack.shape[1], 0, 1, "inproj_lat")
    p_ctx = _normproj(ctx2, mod_ctx, norm1_g, w_pack, batch * ctx_len, n_ctx_cols, 0, 1, "inproj_ctx")

    rw_wts = (mu_pack, rw_k_k, rw_k_a, rw_r_k, rw_w0[0], _pad_rank(rw_w_up[0]).astype(BF16),
              rw_a0[0], _pad_rank(rw_a_up[0]).astype(BF16),
              jnp.pad(rw_g_up[0], ((0, 2 * LANES - gate_rank), (0, 0))).astype(BF16))
    n_pairs = rw_w // LANES
    s0 = jnp.zeros((batch, 2, n_pairs, LANES, LANES), F32)
    f_ctx = _rwprep(p_ctx, ctx_len, rw_wts)
    f_lat = _rwprep(p_lat, seq, rw_wts)
    (s_ctx,) = _rwscan(*f_ctx[:6], s0, batch, want_out=False)
    o_rw_f, o_rw_b, _ = _rwscan(*f_lat[:6], s_ctx, batch, want_out=True)
    ya = _rwread(o_rw_f, o_rw_b, f_lat[6], f_lat[7], rw_gn_g, rw_gn_b)

    s0g = jnp.zeros((batch, 2, gd_heads, GD_HEAD, GD_HEAD), F32)
    a_log2 = gd_a_log[0].reshape(1, 2 * gd_heads)
    dtb2 = gd_dt_bias[0].reshape(1, 2 * gd_heads)

    def gd_feats(p, seq_len):
        q, k, v, gcum, beta = _gdprep(p, seq_len, gd_conv_w[0], a_log2, dtb2, 1, 3, gd_w)
        m = p.shape[0]
        g3 = gcum.reshape(m, 2, gd_heads).transpose(1, 0, 2)
        gt = g3.reshape(2, m // CHUNK, CHUNK, gd_heads).transpose(0, 1, 3, 2)
        gt = gt.reshape(2, m // CHUNK, gd_heads // 2, 2 * CHUNK)
        b3 = beta.reshape(m, 2, gd_heads).transpose(1, 0, 2)
        return q, k, v, g3, gt, b3

    (sg_ctx,) = _gdscan(*gd_feats(p_ctx, ctx_len), s0g, batch, want_out=False)
    o_gd_f, o_gd_b, _ = _gdscan(*gd_feats(p_lat, seq), sg_ctx, batch, want_out=True)
    yb = _gdread(o_gd_f, o_gd_b, p_lat, 2 * blk0 // gd_w - 1, gd_norm_g)

    merged = _merge1(ya, yb, w_a_out[0].astype(BF16), w_b_out[0].astype(BF16), p_lat, gate_col0)
    x1 = _merge2(merged, w_o[0].astype(BF16), x2, mod_lat, seq, 2)

    act = _ffn_act(x1, mod_lat, norm2_g, ffn_w1[0],
                   ffn_conv_w[0].reshape(-1, ffn_conv_w.shape[-1]), batch, 3, 4)
    out = _ffn_out(act, ffn_w2[0].astype(BF16), x1, mod_lat, final_norm_g[None, :], seq, 5)
    return out.reshape(batch, seq, d)


def _pad_rows8(a):
    return jnp.pad(a, ((0, 0), (0, 8 - a.shape[1]), (0, 0)))
```

```python
import functools
import math

import jax
import jax.numpy as jnp
from jax import lax
from jax.experimental import pallas as pl
from jax.experimental.pallas import tpu as pltpu

F32 = jnp.float32
BF16 = jnp.bfloat16
HIGHEST = lax.Precision.HIGHEST

NORM_EPS = 1e-6
RW_GN_EPS = 64e-5
RW_HEAD = 64
GD_HEAD = 128
LANES = 128
CHUNK = 64
GRID_W = 64
RANK_PAD = 128
VMEM_LIMIT = 56 * 1024 * 1024


def _params(sem):
    return pltpu.CompilerParams(dimension_semantics=sem, vmem_limit_bytes=VMEM_LIMIT)


def _tile(n, pref, mult=8):
    if n <= pref:
        return n
    t = (pref // mult) * mult
    while t >= mult:
        if n % t == 0:
            return t
        t -= mult
    return n


def _mm(a, b):
    return jnp.dot(a.astype(BF16), b.astype(BF16), preferred_element_type=F32)


def _mm_nt(a, b):
    return lax.dot_general(a.astype(BF16), b.astype(BF16), (((1,), (1,)), ((), ())),
                           preferred_element_type=F32)


def _mm_tn(a, b):
    return lax.dot_general(a.astype(BF16), b.astype(BF16), (((0,), (0,)), ((), ())),
                           preferred_element_type=F32)


def _mm_hi(a, b):
    return jnp.dot(a, b, precision=HIGHEST, preferred_element_type=F32)


def _split3(x):
    x1 = x.astype(BF16)
    r1 = x - x1.astype(F32)
    x2 = r1.astype(BF16)
    x3 = (r1 - x2.astype(F32)).astype(BF16)
    return x1, x2, x3


def _mm_sel_left(c, x):
    cb = c.astype(BF16)
    return jnp.dot(jnp.concatenate([cb, cb, cb], axis=1), jnp.concatenate(_split3(x), axis=0),
                   preferred_element_type=F32)


def _mm_sel_right(x, c):
    cb = c.astype(BF16)
    return jnp.dot(jnp.concatenate(_split3(x), axis=1), jnp.concatenate([cb, cb, cb], axis=0),
                   preferred_element_type=F32)


def _softplus(x):
    return jnp.maximum(x, 0.0) + jnp.log(1.0 + jnp.exp(-jnp.abs(x)))


def _sigmoid(x):
    return jax.nn.sigmoid(x)


def _seg_ones(width):
    i = lax.broadcasted_iota(jnp.int32, (LANES, LANES), 0) // width
    j = lax.broadcasted_iota(jnp.int32, (LANES, LANES), 1) // width
    return (i == j).astype(F32)


def _seg_sum(x, width, full_precision=True):
    e = _seg_ones(width)
    n = x.shape[-1] // LANES
    mm = _mm_sel_right if full_precision else _mm
    parts = [mm(x[:, g * LANES:(g + 1) * LANES], e) for g in range(n)]
    return parts[0] if n == 1 else jnp.concatenate(parts, axis=-1)


def _tri(rev):
    i = lax.broadcasted_iota(jnp.int32, (CHUNK, CHUNK), 0)
    j = lax.broadcasted_iota(jnp.int32, (CHUNK, CHUNK), 1)
    return ((j >= i) if rev else (j <= i)).astype(F32)


def _level_masks(i, j):
    masks = [(i // 2) == (j // 2)]
    s = 2
    while s < CHUNK:
        masks.append(((i // (2 * s)) == (j // (2 * s))) & ((i // s) != (j // s)))
        s *= 2
    return masks


def _shift_rows(x, prev_row, next_row):
    n = x.shape[0]
    row = lax.broadcasted_iota(jnp.int32, x.shape, 0)
    xm1 = jnp.where(row == 0, prev_row, pltpu.roll(x, 1, 0))
    xp1 = jnp.where(row == n - 1, next_row, pltpu.roll(x, n - 1, 0))
    return xm1, xp1


def _store_stencil(dst_ref, x, prev_row, next_row, f):
    n = x.shape[0]
    dst_ref[...] = f(pltpu.roll(x, 1, 0), x, pltpu.roll(x, n - 1, 0))
    dst_ref[0:1, :] = f(prev_row, x[0:1], x[1:2])
    dst_ref[n - 1:n, :] = f(x[n - 2:n - 1], x[n - 1:n], next_row)


def _mod_kernel(c_ref, w_ref, b_ref, o_ref):
    c = c_ref[...]
    s = c * _sigmoid(c)
    o_ref[...] = _mm_hi(s, w_ref[...]) + b_ref[...]


def _mod(cc, w_ada, b_ada):
    rows, d = cc.shape
    n = w_ada.shape[1]
    tn = _tile(n, 1024, LANES)
    return pl.pallas_call(
        _mod_kernel,
        grid=(n // tn,),
        in_specs=[pl.BlockSpec((rows, d), lambda j: (0, 0)),
                  pl.BlockSpec((d, tn), lambda j: (0, j)),
                  pl.BlockSpec((1, tn), lambda j: (0, j))],
        out_specs=pl.BlockSpec((rows, tn), lambda j: (0, j)),
        out_shape=jax.ShapeDtypeStruct((rows, n), F32),
        compiler_params=_params(("arbitrary",)),
        name="mod",
    )(cc, w_ada, b_ada)


def _normproj_kernel(x_ref, mod_ref, g_ref, w_ref, o_ref, h_ref, *, sh_row, sc_row):
    @pl.when(pl.program_id(1) == 0)
    def _():
        x = x_ref[...]
        ms = jnp.mean(x * x, axis=-1, keepdims=True)
        y = x * lax.rsqrt(ms + NORM_EPS) * g_ref[...]
        sh = mod_ref[0, sh_row:sh_row + 1, :]
        sc = mod_ref[0, sc_row:sc_row + 1, :]
        h_ref[...] = (y * (1.0 + sc) + sh).astype(BF16)

    o_ref[...] = jnp.dot(h_ref[...], w_ref[...], preferred_element_type=F32)


def _normproj(x2, mod, gain, w, rows_per_mod, n_cols, sh_row, sc_row, name):
    m, d = x2.shape
    tm = _tile(rows_per_mod, 1024)
    tn = _tile(n_cols, 1024, LANES)
    per = rows_per_mod // tm
    return pl.pallas_call(
        functools.partial(_normproj_kernel, sh_row=sh_row, sc_row=sc_row),
        grid=(m // tm, n_cols // tn),
        in_specs=[pl.BlockSpec((tm, d), lambda i, j: (i, 0)),
                  pl.BlockSpec((1, 8, d), lambda i, j: (i // per, 0, 0)),
                  pl.BlockSpec((1, d), lambda i, j: (0, 0)),
                  pl.BlockSpec((d, tn), lambda i, j: (0, j))],
        out_specs=pl.BlockSpec((tm, tn), lambda i, j: (i, j)),
        out_shape=jax.ShapeDtypeStruct((m, n_cols), F32),
        scratch_shapes=[pltpu.VMEM((tm, d), BF16)],
        compiler_params=_params(("parallel", "arbitrary")),
        name=name,
    )(x2, mod, gain, w)


def _rwprep_kernel(pc_ref, pp_ref, pn_ref, mu_ref, kk_ref, ka_ref, rk_ref, w0_ref, wup_ref,
                   a0_ref, aup_ref, gup_ref,
                   at_ref, bt_ref, kt_ref, rt_ref, v_ref, pt_ref, bonus_ref, g_ref, xs_ref,
                   *, tiles_per_seq):
    i = pl.program_id(0)
    tm = pc_ref.shape[0]
    w = RW_HEAD * (kk_ref.shape[1] // RW_HEAD)
    first = (i % tiles_per_seq) == 0
    last = (i % tiles_per_seq) == tiles_per_seq - 1
    base = 3 * w
    used = base + 4 * RANK_PAD + gup_ref.shape[0]
    x = pc_ref[:, 0:used]
    prev_row = jnp.where(first, 0.0, pp_ref[7:8, 0:used])
    next_row = jnp.where(last, 0.0, pn_ref[0:1, 0:used])
    mu = mu_ref[:, 0:used]
    keep = 1.0 - mu
    half = 0.5 * mu
    _store_stencil(xs_ref, x, prev_row, next_row, lambda a, c, b: c * keep + (a + b) * half)

    r = xs_ref[:, 0:w]
    k = xs_ref[:, w:2 * w]
    v = xs_ref[:, 2 * w:3 * w]
    gd = xs_ref[:, base + 4 * RANK_PAD:used]
    g_ref[...] = _mm(_sigmoid(gd), gup_ref[...])
    v_ref[...] = v.astype(BF16)

    kx = k * kk_ref[...]
    kk = kx * lax.rsqrt(jnp.maximum(_seg_sum(kx * kx, RW_HEAD, full_precision=False), 1e-12))

    ksum = jnp.zeros_like(k)
    for d in range(2):
        wd = xs_ref[:, base + d * RANK_PAD: base + (d + 1) * RANK_PAD]
        ad = xs_ref[:, base + (2 + d) * RANK_PAD: base + (3 + d) * RANK_PAD]
        wl = w0_ref[d:d + 1, :] + _mm(jnp.tanh(wd), wup_ref[d])
        lw = -math.exp(-0.5) * _sigmoid(wl)
        a = _sigmoid(a0_ref[d:d + 1, :] + _mm(ad, aup_ref[d]))
        kd = k * (1.0 + (a - 1.0) * ka_ref[...])
        ksum = ksum + kd
        tri = _tri(rev=(d == 1))
        for c in range(tm // CHUNK):
            rows = slice(c * CHUNK, (c + 1) * CHUNK)
            lwc = lw[rows]
            cum = _mm_sel_left(tri, lwc)
            p_in = jnp.exp(cum)
            p_inv = jnp.exp(-cum)
            p_ex = jnp.exp(cum - lwc)
            at_ref[d, rows, :] = (-kk[rows] * p_ex).astype(BF16)
            bt_ref[d, rows, :] = (kk[rows] * a[rows] * p_inv).astype(BF16)
            kt_ref[d, rows, :] = (kd[rows] * p_inv).astype(BF16)
            rt_ref[d, rows, :] = (r[rows] * p_in).astype(BF16)
            tot = cum[CHUNK - 1:CHUNK] if d == 0 else cum[0:1]
            pt_ref[d, c, :, :] = jnp.exp(tot)
    bonus_ref[...] = _seg_sum(r * ksum * rk_ref[...], RW_HEAD, full_precision=False) * v


def _rwprep(p, seq_len, wts):
    m = p.shape[0]
    mu, k_k, k_a, r_k, w0, w_up, a0, a_up, g_up = wts
    w = k_k.shape[1]
    blk = mu.shape[1]
    tm = _tile(seq_len, 256, CHUNK)
    tps = seq_len // tm
    nb8 = m // 8
    full = lambda a: pl.BlockSpec(a.shape, lambda i: (0,) * a.ndim)
    feat = jax.ShapeDtypeStruct((2, m, w), BF16)
    feat_spec = pl.BlockSpec((2, tm, w), lambda i: (0, i, 0))
    row_spec = pl.BlockSpec((tm, w), lambda i: (i, 0))
    return pl.pallas_call(
        functools.partial(_rwprep_kernel, tiles_per_seq=tps),
        grid=(m // tm,),
        in_specs=[pl.BlockSpec((tm, blk), lambda i: (i, 0)),
                  pl.BlockSpec((8, blk), lambda i: (jnp.maximum(i * (tm // 8) - 1, 0), 0)),
                  pl.BlockSpec((8, blk), lambda i: (jnp.minimum((i + 1) * (tm // 8), nb8 - 1), 0)),
                  full(mu), full(k_k), full(k_a), full(r_k), full(w0), full(w_up), full(a0),
                  full(a_up), full(g_up)],
        out_specs=[feat_spec, feat_spec, feat_spec, feat_spec, row_spec,
                   pl.BlockSpec((2, tm // CHUNK, 1, w), lambda i: (0, i, 0, 0)),
                   row_spec, row_spec],
        out_shape=[feat, feat, feat, feat, jax.ShapeDtypeStruct((m, w), BF16),
                   jax.ShapeDtypeStruct((2, m // CHUNK, 1, w), F32),
                   jax.ShapeDtypeStruct((m, w), F32), jax.ShapeDtypeStruct((m, w), F32)],
        scratch_shapes=[pltpu.VMEM((tm, 3 * w + 4 * RANK_PAD + g_up.shape[0]), F32)],
        compiler_params=_params(("parallel",)),
        name="rwprep",
    )(p, p, p, mu, k_k, k_a, r_k, w0, w_up, a0, a_up, g_up)


def _pair_blockdiag(x, m0):
    zero = jnp.zeros_like(x)
    return jnp.concatenate([jnp.where(m0, x, zero), jnp.where(m0, zero, x)], axis=0)


def _tri_inverse(a, eye, blk, m0, sign):
    ts = [eye + sign * jnp.where(blk[0], x, 0.0) for x in a]
    for lvl in range(1, len(blk)):
        xs = [_mm(t, _pair_blockdiag(jnp.where(blk[lvl], x, 0.0), m0)) for t, x in zip(ts, a)]
        ts = [t + sign * _mm(x, _pair_blockdiag(t, m0)) for t, x in zip(ts, xs)]
    return ts


def _pair_masks(head_cols):
    i = lax.broadcasted_iota(jnp.int32, (CHUNK, LANES), 0)
    lane = lax.broadcasted_iota(jnp.int32, (CHUNK, LANES), 1)
    j = lane % head_cols
    strict = (j < i, j > i)
    incl = (j <= i, j >= i)
    eye = (j == i).astype(F32)
    return strict, incl, eye, _level_masks(i, j), lane < head_cols


def _rwscan_kernel(*refs, want_out):
    ins, rest = refs[:12], refs[12:]
    s0_ref = rest[0]
    if want_out:
        o_refs, s_ref, h_ref = rest[1:3], rest[3], rest[4]
    else:
        s_ref, h_ref = rest[1], rest[2]
    c = pl.program_id(1)
    n_pairs = h_ref.shape[1]

    @pl.when(c == 0)
    def _():
        h_ref[...] = s0_ref[0]

    strict, incl, eye, blk, m0 = _pair_masks(RW_HEAD)
    r2 = lax.broadcasted_iota(jnp.int32, (LANES, LANES), 0) // RW_HEAD
    c2 = lax.broadcasted_iota(jnp.int32, (LANES, LANES), 1) // RW_HEAD
    diag2 = r2 == c2

    chains = [(d, p) for d in range(2) for p in range(n_pairs)]
    cols = lambda p: slice(p * LANES, (p + 1) * LANES)
    at = [ins[6 * d + 0][0, :, cols(p)] for d, p in chains]
    bt = [ins[6 * d + 1][0, :, cols(p)] for d, p in chains]
    kt = [ins[6 * d + 2][0, :, cols(p)] for d, p in chains]
    rt = [ins[6 * d + 3][0, :, cols(p)] for d, p in chains]
    v = [ins[6 * d + 4][:, cols(p)] for d, p in chains]
    pt = [ins[6 * d + 5][0, 0, :, cols(p)] for d, p in chains]
    n = len(chains)
    bd = lambda x: _pair_blockdiag(x, m0)

    s4 = [_mm_nt(jnp.concatenate([at[i], rt[i]], axis=0),
                 jnp.concatenate([bd(bt[i]), bd(kt[i])], axis=0)) for i in range(n)]
    a_ab = [jnp.where(strict[chains[i][0]], s4[i][:CHUNK, :LANES], 0.0) for i in range(n)]
    a_ak = [jnp.where(strict[chains[i][0]], s4[i][:CHUNK, LANES:], 0.0) for i in range(n)]
    t = _tri_inverse(a_ab, eye, blk, m0, 1.0)
    av = [_mm(a_ak[i], bd(v[i])) for i in range(n)]
    wu = [_mm(t[i], jnp.concatenate([bd(at[i]), bd(av[i].astype(BF16))], axis=1)) for i in range(n)]

    ht = [h_ref[d, p] for d, p in chains]
    if want_out:
        m1 = [_mm_nt(jnp.concatenate([wu[i][:, :LANES].astype(BF16), rt[i]], axis=0), ht[i])
              for i in range(n)]
        u = [m1[i][:CHUNK] + wu[i][:, LANES:] for i in range(n)]
    else:
        u = [_mm_nt(wu[i][:, :LANES], ht[i]) + wu[i][:, LANES:] for i in range(n)]
    ub = [x.astype(BF16) for x in u]
    if want_out:
        for i, (d, p) in enumerate(chains):
            m_r = jnp.where(jnp.concatenate([incl[d], incl[d]], axis=1), s4[i][CHUNK:], 0.0)
            o = m1[i][CHUNK:] + _mm(m_r, jnp.concatenate([bd(ub[i]), bd(v[i])], axis=0))
            o_refs[d][:, cols(p)] = o
    for i, (d, p) in enumerate(chains):
        upd = _mm_tn(jnp.concatenate([ub[i], v[i]], axis=0),
                     jnp.concatenate([bt[i], kt[i]], axis=0))
        h_ref[d, p] = (ht[i] + jnp.where(diag2, upd, 0.0)) * pt[i]

    @pl.when(c == pl.num_programs(1) - 1)
    def _():
        s_ref[0] = h_ref[...]


def _rwscan(at, bt, kt, rt, v, pt, s0, batch, want_out):
    m, w = v.shape
    n_chunks = m // batch // CHUNK
    n_pairs = w // LANES
    rows = (lambda b, c: b * n_chunks + c, lambda b, c: b * n_chunks + n_chunks - 1 - c)

    in_specs, args = [], []
    for d in range(2):
        feat_spec = pl.BlockSpec((1, CHUNK, w), lambda b, c, d=d: (d, rows[d](b, c), 0))
        in_specs += [feat_spec] * 4
        in_specs += [pl.BlockSpec((CHUNK, w), lambda b, c, d=d: (rows[d](b, c), 0)),
                     pl.BlockSpec((1, 1, 1, w), lambda b, c, d=d: (d, rows[d](b, c), 0, 0))]
        args += [at, bt, kt, rt, v, pt]
    state_spec = pl.BlockSpec((1, 2, n_pairs, LANES, LANES), lambda b, c: (b, 0, 0, 0, 0))
    out_specs = [state_spec]
    out_shape = [jax.ShapeDtypeStruct(s0.shape, F32)]
    if want_out:
        out_specs = [pl.BlockSpec((CHUNK, w), lambda b, c, d=d: (rows[d](b, c), 0))
                     for d in range(2)] + out_specs
        out_shape = [jax.ShapeDtypeStruct((m, w), F32)] * 2 + out_shape
    return pl.pallas_call(
        functools.partial(_rwscan_kernel, want_out=want_out),
        grid=(batch, n_chunks),
        in_specs=in_specs + [state_spec],
        out_specs=out_specs,
        out_shape=out_shape,
        scratch_shapes=[pltpu.VMEM((2, n_pairs, LANES, LANES), F32)],
        compiler_params=_params(("parallel", "arbitrary")),
        name="rwscan_out" if want_out else "rwscan_state",
    )(*args, s0)


def _rwread_kernel(of_ref, ob_ref, bonus_ref, g_ref, gng_ref, gnb_ref, y_ref):
    o = of_ref[...] + ob_ref[...]
    inv = 1.0 / RW_HEAD
    mean = _seg_sum(o, RW_HEAD) * inv
    cen = o - mean
    var = _seg_sum(cen * cen, RW_HEAD) * inv
    on = cen * lax.rsqrt(var + RW_GN_EPS) * gng_ref[...] + gnb_ref[...]
    y_ref[...] = ((on + bonus_ref[...]) * g_ref[...]).astype(BF16)


def _rwread(o_f, o_b, bonus, g, gn_g, gn_b):
    m, w = o_f.shape
    tm = _tile(m, 512)
    row_spec = pl.BlockSpec((tm, w), lambda i: (i, 0))
    vec_spec = pl.BlockSpec((1, w), lambda i: (0, 0))
    return pl.pallas_call(
        _rwread_kernel,
        grid=(m // tm,),
        in_specs=[row_spec, row_spec, row_spec, row_spec, vec_spec, vec_spec],
        out_specs=row_spec,
        out_shape=jax.ShapeDtypeStruct((m, w), BF16),
        compiler_params=_params(("parallel",)),
        name="rwread",
    )(o_f, o_b, bonus, g, gn_g, gn_b)


def _gdprep_kernel(pc_ref, pp_ref, pn_ref, ab_ref, cw_ref, alog_ref, dtb_ref,
                   q_ref, k_ref, v_ref, gcum_ref, beta_ref, *, tiles_per_seq):
    i = pl.program_id(0)
    tm = pc_ref.shape[0]
    w = q_ref.shape[1]
    n_heads = w // GD_HEAD
    first = (i % tiles_per_seq) == 0
    last = (i % tiles_per_seq) == tiles_per_seq - 1
    x = pc_ref[:, 0:3 * w]
    prev_row = jnp.where(first, 0.0, pp_ref[7:8, 0:3 * w])
    next_row = jnp.where(last, 0.0, pn_ref[0:1, 0:3 * w])
    xm1, xp1 = _shift_rows(x, prev_row, next_row)
    y = xm1 * cw_ref[0:1, :] + x * cw_ref[1:2, :] + xp1 * cw_ref[2:3, :]
    y = y * _sigmoid(y)
    for h in range(n_heads):
        for part, ref, scale in ((0, q_ref, GD_HEAD ** -0.5), (1, k_ref, 1.0)):
            cols = slice(part * w + h * GD_HEAD, part * w + (h + 1) * GD_HEAD)
            t = y[:, cols]
            ss = jnp.sum(t * t, axis=-1, keepdims=True)
            ref[:, h * GD_HEAD:(h + 1) * GD_HEAD] = (
                t * (lax.rsqrt(jnp.maximum(ss, 1e-12)) * scale)).astype(BF16)
    v_ref[...] = y[:, 2 * w:3 * w].astype(BF16)

    ab = ab_ref[:, 6 * LANES:7 * LANES]
    a = ab[:, 0:2 * n_heads]
    b = ab[:, 2 * n_heads:4 * n_heads]
    glog = -jnp.exp(alog_ref[...]) * _softplus(a + dtb_ref[...])
    beta_ref[...] = _sigmoid(b)
    for c in range(tm // CHUNK):
        rows = slice(c * CHUNK, (c + 1) * CHUNK)
        gc = glog[rows]
        fwd = _mm_sel_left(_tri(False), gc)
        bwd = _mm_sel_left(_tri(True), gc)
        col = lax.broadcasted_iota(jnp.int32, gc.shape, 1)
        gcum_ref[rows, :] = jnp.where(col < n_heads, fwd, bwd)


def _gdprep(p, seq_len, conv_w, a_log, dt_bias, qkvz_block, ab_block, width):
    m = p.shape[0]
    blk = 4 * width
    tm = _tile(seq_len, 256, CHUNK)
    tps = seq_len // tm
    nb8 = m // 8
    n2h = a_log.shape[1]
    full = lambda a: pl.BlockSpec(a.shape, lambda i: (0,) * a.ndim)
    row_spec = pl.BlockSpec((tm, width), lambda i: (i, 0))
    small_spec = pl.BlockSpec((tm, n2h), lambda i: (i, 0))
    return pl.pallas_call(
        functools.partial(_gdprep_kernel, tiles_per_seq=tps),
        grid=(m // tm,),
        in_specs=[pl.BlockSpec((tm, blk), lambda i: (i, qkvz_block)),
                  pl.BlockSpec((8, blk), lambda i: (jnp.maximum(i * (tm // 8) - 1, 0), qkvz_block)),
                  pl.BlockSpec((8, blk),
                               lambda i: (jnp.minimum((i + 1) * (tm // 8), nb8 - 1), qkvz_block)),
                  pl.BlockSpec((tm, 8 * LANES), lambda i: (i, ab_block)),
                  full(conv_w), full(a_log), full(dt_bias)],
        out_specs=[row_spec, row_spec, row_spec, small_spec, small_spec],
        out_shape=[jax.ShapeDtypeStruct((m, width), BF16)] * 3
        + [jax.ShapeDtypeStruct((m, n2h), F32)] * 2,
        compiler_params=_params(("parallel",)),
        name="gdprep",
    )(p, p, p, p, conv_w, a_log, dt_bias)


def _gdscan_kernel(*refs, want_out):
    ins, rest = refs[:12], refs[12:]
    s0_ref = rest[0]
    if want_out:
        o_refs, s_ref, st_ref = rest[1:3], rest[3], rest[4]
    else:
        s_ref, st_ref = rest[1], rest[2]
    c = pl.program_id(1)
    n_pairs = st_ref.shape[1] // 2

    @pl.when(c == 0)
    def _():
        st_ref[...] = s0_ref[0]

    strict, incl, eye, blk, m0 = _pair_masks(CHUNK)
    bd = lambda x: _pair_blockdiag(x, m0)
    chains = [(d, p) for d in range(2) for p in range(n_pairs)]
    n = len(chains)
    hcols = lambda h: slice(h * GD_HEAD, (h + 1) * GD_HEAD)

    def head_vals(d, p, e):
        h = 2 * p + e
        q_ref, k_ref, v_ref, g_ref, _, beta_ref = ins[6 * d:6 * d + 6]
        return (q_ref[:, hcols(h)].astype(F32), k_ref[:, hcols(h)].astype(F32),
                v_ref[:, hcols(h)].astype(F32), g_ref[0, :, h:h + 1], beta_ref[0, :, h:h + 1])

    hv = [[head_vals(d, p, e) for e in range(2)] for d, p in chains]
    kb = [[hv[i][e][1] * hv[i][e][4] for e in range(2)] for i in range(n)]
    eg = [[jnp.exp(hv[i][e][3]) for e in range(2)] for i in range(n)]
    decay, s2 = [], []
    for i, (d, p) in enumerate(chains):
        gcol = jnp.where(m0, hv[i][0][3], hv[i][1][3])
        grow = ins[6 * d + 4][0, 0, p:p + 1, :]
        decay.append(jnp.where(incl[d], jnp.exp(jnp.where(incl[d], gcol - grow, 0.0)), 0.0))
        k0, k1 = hv[i][0][1], hv[i][1][1]
        zero = jnp.zeros_like(k0)
        lhs = jnp.concatenate([jnp.concatenate([kb[i][0], kb[i][1]], axis=1),
                               jnp.concatenate([hv[i][0][0], hv[i][1][0]], axis=1)], axis=0)
        rhs = jnp.concatenate([jnp.concatenate([k0, zero], axis=1),
                               jnp.concatenate([zero, k1], axis=1)], axis=0)
        s2.append(_mm_nt(lhs, rhs))
    a = [jnp.where(strict[chains[i][0]], s2[i][:CHUNK] * decay[i], 0.0) for i in range(n)]
    t = _tri_inverse(a, eye, blk, m0, -1.0)
    sol = [_mm(bd(t[i]), jnp.concatenate(
        [jnp.concatenate([hv[i][e][2] * hv[i][e][4], kb[i][e] * eg[i][e]], axis=1)
         for e in range(2)], axis=0)) for i in range(n)]

    st = [[st_ref[d, 2 * p + e] for e in range(2)] for d, p in chains]
    ws = [[_mm(jnp.concatenate([sol[i][e * CHUNK:(e + 1) * CHUNK, GD_HEAD:],
                                hv[i][e][0] * eg[i][e]], axis=0), st[i][e])
           for e in range(2)] for i in range(n)]
    v_new = [[sol[i][e * CHUNK:(e + 1) * CHUNK, :GD_HEAD] - ws[i][e][:CHUNK] for e in range(2)]
             for i in range(n)]
    if want_out:
        for i, (d, p) in enumerate(chains):
            intra = _mm(bd(s2[i][CHUNK:] * decay[i]),
                        jnp.concatenate([v_new[i][0], v_new[i][1]], axis=0))
            for e in range(2):
                o_refs[d][:, hcols(2 * p + e)] = (ws[i][e][CHUNK:]
                                                  + intra[e * CHUNK:(e + 1) * CHUNK])
    for i, (d, p) in enumerate(chains):
        for e in range(2):
            gcol = hv[i][e][3]
            g_last = jnp.min(gcol, axis=0, keepdims=True)
            k_dec = hv[i][e][1] * jnp.exp(g_last - gcol)
            st_ref[d, 2 * p + e] = st[i][e] * jnp.exp(g_last) + _mm_tn(k_dec, v_new[i][e])

    @pl.when(c == pl.num_programs(1) - 1)
    def _():
        s_ref[0] = st_ref[...]


def _gdscan(q, k, v, g, gt, beta, s0, batch, want_out):
    m, w = q.shape
    n_chunks = m // batch // CHUNK
    n_heads = w // GD_HEAD
    rows = (lambda b, c: b * n_chunks + c, lambda b, c: b * n_chunks + n_chunks - 1 - c)

    in_specs, args = [], []
    for d in range(2):
        row_spec = pl.BlockSpec((CHUNK, w), lambda b, c, d=d: (rows[d](b, c), 0))
        col_spec = pl.BlockSpec((1, CHUNK, n_heads), lambda b, c, d=d: (d, rows[d](b, c), 0))
        in_specs += [row_spec, row_spec, row_spec, col_spec,
                     pl.BlockSpec((1, 1, n_heads // 2, LANES),
                                  lambda b, c, d=d: (d, rows[d](b, c), 0, 0)),
                     col_spec]
        args += [q, k, v, g, gt, beta]
    state_spec = pl.BlockSpec((1, 2, n_heads, GD_HEAD, GD_HEAD), lambda b, c: (b, 0, 0, 0, 0))
    out_specs = [state_spec]
    out_shape = [jax.ShapeDtypeStruct(s0.shape, F32)]
    if want_out:
        out_specs = [pl.BlockSpec((CHUNK, w), lambda b, c, d=d: (rows[d](b, c), 0))
                     for d in range(2)] + out_specs
        out_shape = [jax.ShapeDtypeStruct((m, w), F32)] * 2 + out_shape
    return pl.pallas_call(
        functools.partial(_gdscan_kernel, want_out=want_out),
        grid=(batch, n_chunks),
        in_specs=in_specs + [state_spec],
        out_specs=out_specs,
        out_shape=out_shape,
        scratch_shapes=[pltpu.VMEM((2, n_heads, GD_HEAD, GD_HEAD), F32)],
        compiler_params=_params(("parallel", "arbitrary")),
        name="gdscan_out" if want_out else "gdscan_state",
    )(*args, s0)


def _gdread_kernel(of_ref, ob_ref, z_ref, ng_ref, y_ref):
    o = of_ref[...] + ob_ref[...]
    z = z_ref[...]
    gate = z * _sigmoid(z)
    for h in range(o.shape[1] // GD_HEAD):
        cols = slice(h * GD_HEAD, (h + 1) * GD_HEAD)
        oh = o[:, cols]
        ms = jnp.mean(oh * oh, axis=-1, keepdims=True)
        y_ref[:, cols] = (oh * lax.rsqrt(ms + NORM_EPS) * ng_ref[...] * gate[:, cols]).astype(BF16)


def _gdread(o_f, o_b, p, z_block, norm_g):
    m, w = o_f.shape
    tm = _tile(m, 512)
    return pl.pallas_call(
        _gdread_kernel,
        grid=(m // tm,),
        in_specs=[pl.BlockSpec((tm, w), lambda i: (i, 0)),
                  pl.BlockSpec((tm, w), lambda i: (i, 0)),
                  pl.BlockSpec((tm, w), lambda i: (i, z_block)),
                  pl.BlockSpec((1, GD_HEAD), lambda i: (0, 0))],
        out_specs=pl.BlockSpec((tm, w), lambda i: (i, 0)),
        out_shape=jax.ShapeDtypeStruct((m, w), BF16),
        compiler_params=_params(("parallel",)),
        name="gdread",
    )(o_f, o_b, p, norm_g)


def _merge1_kernel(ya_ref, yb_ref, wa_ref, wb_ref, ga_ref, gb_ref, o_ref):
    a = jnp.dot(ya_ref[...], wa_ref[...], preferred_element_type=F32)
    b = jnp.dot(yb_ref[...], wb_ref[...], preferred_element_type=F32)
    o_ref[...] = (_sigmoid(ga_ref[...]) * a + _sigmoid(gb_ref[...]) * b).astype(BF16)


def _merge1(ya, yb, wa, wb, p, gate_col0):
    m, ka = ya.shape
    kb = yb.shape[1]
    d = wa.shape[1]
    tm = _tile(m, 1024)
    tn = _tile(d, 1024, LANES)
    ga0 = gate_col0 // tn
    gb0 = (gate_col0 + d) // tn
    return pl.pallas_call(
        _merge1_kernel,
        grid=(m // tm, d // tn),
        in_specs=[pl.BlockSpec((tm, ka), lambda i, j: (i, 0)),
                  pl.BlockSpec((tm, kb), lambda i, j: (i, 0)),
                  pl.BlockSpec((ka, tn), lambda i, j: (0, j)),
                  pl.BlockSpec((kb, tn), lambda i, j: (0, j)),
                  pl.BlockSpec((tm, tn), lambda i, j: (i, ga0 + j)),
                  pl.BlockSpec((tm, tn), lambda i, j: (i, gb0 + j))],
        out_specs=pl.BlockSpec((tm, tn), lambda i, j: (i, j)),
        out_shape=jax.ShapeDtypeStruct((m, d), BF16),
        compiler_params=_params(("parallel", "arbitrary")),
        name="merge1",
    )(ya, yb, wa, wb, p, p)


def _merge2_kernel(mg_ref, wo_ref, x_ref, mod_ref, o_ref, *, gate_row):
    y = jnp.dot(mg_ref[...], wo_ref[...], preferred_element_type=F32)
    o_ref[...] = x_ref[...] + mod_ref[0, gate_row:gate_row + 1, :] * y


def _merge2(merged, wo, x2, mod, rows_per_mod, gate_row):
    m, d = x2.shape
    tm = _tile(rows_per_mod, 1024)
    tn = _tile(d, 1024, LANES)
    per = rows_per_mod // tm
    return pl.pallas_call(
        functools.partial(_merge2_kernel, gate_row=gate_row),
        grid=(m // tm, d // tn),
        in_specs=[pl.BlockSpec((tm, d), lambda i, j: (i, 0)),
                  pl.BlockSpec((d, tn), lambda i, j: (0, j)),
                  pl.BlockSpec((tm, tn), lambda i, j: (i, j)),
                  pl.BlockSpec((1, 8, tn), lambda i, j: (i // per, 0, j))],
        out_specs=pl.BlockSpec((tm, tn), lambda i, j: (i, j)),
        out_shape=jax.ShapeDtypeStruct((m, d), F32),
        compiler_params=_params(("parallel", "arbitrary")),
        name="merge2",
    )(merged, wo, x2, mod)


def _conv3x3_gelu(g, cw_ref):
    n = g.shape[0]
    col = lax.broadcasted_iota(jnp.int32, g.shape, 0) % GRID_W
    left = jnp.where(col > 0, pltpu.roll(g, 1, 0), 0.0)
    right = jnp.where(col < GRID_W - 1, pltpu.roll(g, n - 1, 0), 0.0)
    lines = [left * cw_ref[3 * kh:3 * kh + 1, :] + g * cw_ref[3 * kh + 1:3 * kh + 2, :]
             + right * cw_ref[3 * kh + 2:3 * kh + 3, :] for kh in range(3)]
    pad = jnp.zeros((GRID_W, g.shape[1]), F32)
    acc = (lines[1] + jnp.concatenate([pad, lines[0][:n - GRID_W]], axis=0)
           + jnp.concatenate([lines[2][GRID_W:], pad], axis=0))
    return 0.5 * acc * (1.0 + lax.erf(acc * (2.0 ** -0.5)))


def _ffn_act_kernel(x_ref, mod_ref, g_ref, w1g_ref, w1v_ref, cw_ref, o_ref, h_ref, gate_ref, val_ref,
                    *, sh_row, sc_row, prologue_rows, dot_rows):
    j = pl.program_id(1)

    @pl.when(j == 0)
    def _():
        sh = mod_ref[0, sh_row:sh_row + 1, :]
        sc = mod_ref[0, sc_row:sc_row + 1, :]

        def body(r, carry):
            rows = pl.ds(pl.multiple_of(r * prologue_rows, prologue_rows), prologue_rows)
            x = x_ref[rows, :]
            ms = jnp.mean(x * x, axis=-1, keepdims=True)
            y = x * lax.rsqrt(ms + NORM_EPS) * g_ref[...]
            h_ref[rows, :] = (y * (1.0 + sc) + sh).astype(BF16)
            return carry

        lax.fori_loop(0, x_ref.shape[0] // prologue_rows, body, 0)
        gate_ref[1] = jnp.zeros(gate_ref.shape[1:], F32)
        val_ref[1] = jnp.zeros(val_ref.shape[1:], F32)

    slot = j % 2
    o_ref[...] = (_conv3x3_gelu(gate_ref[1 - slot], cw_ref) * val_ref[1 - slot]).astype(BF16)
    w1g = w1g_ref[...].astype(BF16)
    w1v = w1v_ref[...].astype(BF16)
    for r0 in range(0, h_ref.shape[0], dot_rows):
        rows = slice(r0, r0 + dot_rows)
        hh = h_ref[rows, :]
        gate_ref[slot, rows, :] = jnp.dot(hh, w1g, preferred_element_type=F32)
        val_ref[slot, rows, :] = jnp.dot(hh, w1v, preferred_element_type=F32)


def _ffn_act(x1, mod, gain, w1, conv_w, batch, sh_row, sc_row):
    m, d = x1.shape
    dff = conv_w.shape[1]
    seq = m // batch
    tf = _tile(dff, 256, LANES)
    nblk = dff // tf
    cur = lambda j: jnp.minimum(j, nblk - 1)
    prev = lambda j: jnp.maximum(j - 1, 0)
    return pl.pallas_call(
        functools.partial(_ffn_act_kernel, sh_row=sh_row, sc_row=sc_row,
                          prologue_rows=_tile(seq, 256), dot_rows=_tile(seq, 2048)),
        grid=(batch, nblk + 1),
        in_specs=[pl.BlockSpec((seq, d), lambda b, j: (b, 0), pipeline_mode=pl.Buffered(1)),
                  pl.BlockSpec((1, 8, d), lambda b, j: (b, 0, 0)),
                  pl.BlockSpec((1, d), lambda b, j: (0, 0)),
                  pl.BlockSpec((d, tf), lambda b, j: (0, cur(j))),
                  pl.BlockSpec((d, tf), lambda b, j: (0, nblk + cur(j))),
                  pl.BlockSpec((conv_w.shape[0], tf), lambda b, j: (0, prev(j)))],
        out_specs=pl.BlockSpec((seq, tf), lambda b, j: (b, prev(j))),
        out_shape=jax.ShapeDtypeStruct((m, dff), BF16),
        scratch_shapes=[pltpu.VMEM((seq, d), BF16),
                        pltpu.VMEM((2, seq, tf), F32),
                        pltpu.VMEM((2, seq, tf), F32)],
        compiler_params=_params(("parallel", "arbitrary")),
        name="ffn_act",
    )(x1, mod, gain, w1, w1, conv_w)


def _ffn_out_kernel(act_ref, w2_ref, x_ref, mod_ref, g_ref, o_ref, acc_ref, *, gate_row):
    kstep = pl.program_id(1)

    @pl.when(kstep == 0)
    def _():
        acc_ref[...] = jnp.zeros_like(acc_ref)

    acc_ref[...] += jnp.dot(act_ref[...], w2_ref[...], preferred_element_type=F32)

    @pl.when(kstep == pl.num_programs(1) - 1)
    def _():
        y = x_ref[...] + mod_ref[0, gate_row:gate_row + 1, :] * acc_ref[...]
        ms = jnp.mean(y * y, axis=-1, keepdims=True)
        o_ref[...] = y * lax.rsqrt(ms + NORM_EPS) * g_ref[...]


def _ffn_out(act, w2, x1, mod, final_g, rows_per_mod, gate_row):
    m, d = x1.shape
    dff = act.shape[1]
    tm = _tile(rows_per_mod, 512)
    tk = _tile(dff, 1408, LANES)
    per = rows_per_mod // tm
    return pl.pallas_call(
        functools.partial(_ffn_out_kernel, gate_row=gate_row),
        grid=(m // tm, dff // tk),
        in_specs=[pl.BlockSpec((tm, tk), lambda i, k: (i, k)),
                  pl.BlockSpec((tk, d), lambda i, k: (k, 0)),
                  pl.BlockSpec((tm, d), lambda i, k: (i, 0)),
                  pl.BlockSpec((1, 8, d), lambda i, k: (i // per, 0, 0)),
                  pl.BlockSpec((1, d), lambda i, k: (0, 0))],
        out_specs=pl.BlockSpec((tm, d), lambda i, k: (i, 0)),
        out_shape=jax.ShapeDtypeStruct((m, d), F32),
        scratch_shapes=[pltpu.VMEM((tm, d), F32)],
        compiler_params=_params(("parallel", "arbitrary")),
        name="ffn_out",
    )(act, w2, x1, mod, final_g)


def _pad_cols(a, width):
    return jnp.pad(a, [(0, 0)] * (a.ndim - 1) + [(0, width - a.shape[-1])])


def _pad_rank(a):
    return jnp.pad(a, [(0, 0)] * (a.ndim - 2) + [(0, RANK_PAD - a.shape[-2]), (0, 0)])


def kernel(x, c, ctx, c_ctx, w_ada, b_ada, norm1_g, norm2_g, w_in, rw_mu, rw_k_k, rw_k_a, rw_r_k, rw_w0, rw_w_up, rw_a0, rw_a_up, rw_g_up, rw_gn_g, rw_gn_b, gd_conv_w, gd_a_log, gd_dt_bias, gd_norm_g, w_a_out, w_b_out, w_o, ffn_w1, ffn_conv_w, ffn_w2, final_norm_g):
    batch, seq, d = x.shape
    ctx_len = ctx.shape[1]
    assert w_ada.shape[0] == 1, "single layer only"
    rw_w = rw_k_k.shape[1]
    gd_w = w_b_out.shape[1]
    dec_rank = rw_w_up.shape[2]
    icl_rank = rw_a_up.shape[2]
    gate_rank = rw_g_up.shape[1]
    gd_heads = gd_a_log.shape[2]
    assert max(dec_rank, icl_rank) <= RANK_PAD and 4 * gd_heads <= 2 * LANES
    assert seq % CHUNK == 0 and ctx_len % CHUNK == 0 and seq % GRID_W == 0
    assert 3 * rw_w == 3 * gd_w and gate_rank <= 2 * LANES

    low_w = 8 * LANES
    blk0 = 3 * rw_w + low_w
    assert blk0 == 4 * gd_w
    wi = w_in[0]
    o_rw = 3 * rw_w
    o_gd = o_rw + 2 * dec_rank + 2 * icl_rank + gate_rank
    o_ab = o_gd + 4 * gd_w
    o_gate = o_ab + 4 * gd_heads

    def pack_cols(a):
        pieces = [a[..., :o_rw]]
        off = o_rw
        for r in (dec_rank, dec_rank, icl_rank, icl_rank):
            pieces.append(_pad_cols(a[..., off:off + r], RANK_PAD))
            off += r
        pieces.append(_pad_cols(a[..., off:off + gate_rank], 2 * LANES))
        return pieces

    w_pack = jnp.concatenate(
        pack_cols(wi) + [_pad_cols(wi[:, o_ab:o_gate], 2 * LANES), wi[:, o_gd:o_ab], wi[:, o_gate:]],
        axis=1).astype(BF16)
    n_ctx_cols = 2 * blk0
    gate_col0 = 2 * blk0
    mu_pack = jnp.concatenate(pack_cols(rw_mu) + [jnp.zeros((1, 2 * LANES), F32)], axis=1)

    cc = jnp.concatenate([c, c_ctx[None, :], jnp.zeros((16 - batch - 1, d), F32)], axis=0)
    mods = _mod(cc, w_ada[0], b_ada)
    mod_lat = _pad_rows8(mods[:batch].reshape(batch, 6, d))
    mod_ctx = _pad_rows8(mods[batch:batch + 1].reshape(1, 6, d))

    x2 = x.reshape(batch * seq, d)
    ctx2 = ctx.reshape(batch * ctx_len, d)
    p_lat = _normproj(x2, mod_lat, norm1_g, w_pack, seq, w_pack.shape[1], 0, 1, "inproj_lat")
    p_ctx = _normproj(ctx2, mod_ctx, norm1_g, w_pack, batch * ctx_len, n_ctx_cols, 0, 1, "inproj_ctx")

    rw_wts = (mu_pack, rw_k_k, rw_k_a, rw_r_k, rw_w0[0], _pad_rank(rw_w_up[0]).astype(BF16),
              rw_a0[0], _pad_rank(rw_a_up[0]).astype(BF16),
              jnp.pad(rw_g_up[0], ((0, 2 * LANES - gate_rank), (0, 0))).astype(BF16))
    n_pairs = rw_w // LANES
    s0 = jnp.zeros((batch, 2, n_pairs, LANES, LANES), F32)
    f_ctx = _rwprep(p_ctx, ctx_len, rw_wts)
    f_lat = _rwprep(p_lat, seq, rw_wts)
    (s_ctx,) = _rwscan(*f_ctx[:6], s0, batch, want_out=False)
    o_rw_f, o_rw_b, _ = _rwscan(*f_lat[:6], s_ctx, batch, want_out=True)
    ya = _rwread(o_rw_f, o_rw_b, f_lat[6], f_lat[7], rw_gn_g, rw_gn_b)

    s0g = jnp.zeros((batch, 2, gd_heads, GD_HEAD, GD_HEAD), F32)
    a_log2 = gd_a_log[0].reshape(1, 2 * gd_heads)
    dtb2 = gd_dt_bias[0].reshape(1, 2 * gd_heads)

    def gd_feats(p, seq_len):
        q, k, v, gcum, beta = _gdprep(p, seq_len, gd_conv_w[0], a_log2, dtb2, 1, 3, gd_w)
        m = p.shape[0]
        g3 = gcum.reshape(m, 2, gd_heads).transpose(1, 0, 2)
        gt = g3.reshape(2, m // CHUNK, CHUNK, gd_heads).transpose(0, 1, 3, 2)
        gt = gt.reshape(2, m // CHUNK, gd_heads // 2, 2 * CHUNK)
        b3 = beta.reshape(m, 2, gd_heads).transpose(1, 0, 2)
        return q, k, v, g3, gt, b3

    (sg_ctx,) = _gdscan(*gd_feats(p_ctx, ctx_len), s0g, batch, want_out=False)
    o_gd_f, o_gd_b, _ = _gdscan(*gd_feats(p_lat, seq), sg_ctx, batch, want_out=True)
    yb = _gdread(o_gd_f, o_gd_b, p_lat, 2 * blk0 // gd_w - 1, gd_norm_g)

    merged = _merge1(ya, yb, w_a_out[0].astype(BF16), w_b_out[0].astype(BF16), p_lat, gate_col0)
    x1 = _merge2(merged, w_o[0].astype(BF16), x2, mod_lat, seq, 2)

    act = _ffn_act(x1, mod_lat, norm2_g, ffn_w1[0],
                   ffn_conv_w[0].reshape(-1, ffn_conv_w.shape[-1]), batch, 3, 4)
    out = _ffn_out(act, ffn_w2[0].astype(BF16), x1, mod_lat, final_norm_g[None, :], seq, 5)
    return out.reshape(batch, seq, d)


def _pad_rows8(a):
    return jnp.pad(a, ((0, 0), (0, 8 - a.shape[1]), (0, 0)))
```

```python
import functools
import math

import jax
import jax.numpy as jnp
from jax import lax
from jax.experimental import pallas as pl
from jax.experimental.pallas import tpu as pltpu

F32 = jnp.float32
BF16 = jnp.bfloat16
HIGHEST = lax.Precision.HIGHEST

NORM_EPS = 1e-6
RW_GN_EPS = 64e-5
RW_HEAD = 64
GD_HEAD = 128
LANES = 128
CHUNK = 64
GRID_W = 64
RANK_PAD = 128
VMEM_LIMIT = 56 * 1024 * 1024


def _params(sem):
    return pltpu.CompilerParams(dimension_semantics=sem, vmem_limit_bytes=VMEM_LIMIT)


def _tile(n, pref, mult=8):
    if n <= pref:
        return n
    t = (pref // mult) * mult
    while t >= mult:
        if n % t == 0:
            return t
        t -= mult
    return n


def _mm(a, b):
    return jnp.dot(a.astype(BF16), b.astype(BF16), preferred_element_type=F32)


def _mm_nt(a, b):
    return lax.dot_general(a.astype(BF16), b.astype(BF16), (((1,), (1,)), ((), ())),
                           preferred_element_type=F32)


def _mm_tn(a, b):
    return lax.dot_general(a.astype(BF16), b.astype(BF16), (((0,), (0,)), ((), ())),
                           preferred_element_type=F32)


def _mm_hi(a, b):
    return jnp.dot(a, b, precision=HIGHEST, preferred_element_type=F32)


def _split3(x):
    x1 = x.astype(BF16)
    r1 = x - x1.astype(F32)
    x2 = r1.astype(BF16)
    x3 = (r1 - x2.astype(F32)).astype(BF16)
    return x1, x2, x3


def _mm_sel_left(c, x):
    cb = c.astype(BF16)
    return jnp.dot(jnp.concatenate([cb, cb, cb], axis=1), jnp.concatenate(_split3(x), axis=0),
                   preferred_element_type=F32)


def _mm_sel_right(x, c):
    cb = c.astype(BF16)
    return jnp.dot(jnp.concatenate(_split3(x), axis=1), jnp.concatenate([cb, cb, cb], axis=0),
                   preferred_element_type=F32)


def _softplus(x):
    return jnp.maximum(x, 0.0) + jnp.log(1.0 + jnp.exp(-jnp.abs(x)))


def _sigmoid(x):
    return jax.nn.sigmoid(x)


def _seg_ones(width):
    i = lax.broadcasted_iota(jnp.int32, (LANES, LANES), 0) // width
    j = lax.broadcasted_iota(jnp.int32, (LANES, LANES), 1) // width
    return (i == j).astype(F32)


def _seg_sum(x, width, full_precision=True):
    e = _seg_ones(width)
    n = x.shape[-1] // LANES
    mm = _mm_sel_right if full_precision else _mm
    parts = [mm(x[:, g * LANES:(g + 1) * LANES], e) for g in range(n)]
    return parts[0] if n == 1 else jnp.concatenate(parts, axis=-1)


def _tri(rev):
    i = lax.broadcasted_iota(jnp.int32, (CHUNK, CHUNK), 0)
    j = lax.broadcasted_iota(jnp.int32, (CHUNK, CHUNK), 1)
    return ((j >= i) if rev else (j <= i)).astype(F32)


def _level_masks(i, j):
    masks = [(i // 2) == (j // 2)]
    s = 2
    while s < CHUNK:
        masks.append(((i // (2 * s)) == (j // (2 * s))) & ((i // s) != (j // s)))
        s *= 2
    return masks


def _shift_rows(x, prev_row, next_row):
    n = x.shape[0]
    row = lax.broadcasted_iota(jnp.int32, x.shape, 0)
    xm1 = jnp.where(row == 0, prev_row, pltpu.roll(x, 1, 0))
    xp1 = jnp.where(row == n - 1, next_row, pltpu.roll(x, n - 1, 0))
    return xm1, xp1


def _store_stencil(dst_ref, x, prev_row, next_row, f):
    n = x.shape[0]
    dst_ref[...] = f(pltpu.roll(x, 1, 0), x, pltpu.roll(x, n - 1, 0))
    dst_ref[0:1, :] = f(prev_row, x[0:1], x[1:2])
    dst_ref[n - 1:n, :] = f(x[n - 2:n - 1], x[n - 1:n], next_row)


def _mod_kernel(c_ref, w_ref, b_ref, o_ref):
    c = c_ref[...]
    s = c * _sigmoid(c)
    o_ref[...] = _mm_hi(s, w_ref[...]) + b_ref[...]


def _mod(cc, w_ada, b_ada):
    rows, d = cc.shape
    n = w_ada.shape[1]
    tn = _tile(n, 1024, LANES)
    return pl.pallas_call(
        _mod_kernel,
        grid=(n // tn,),
        in_specs=[pl.BlockSpec((rows, d), lambda j: (0, 0)),
                  pl.BlockSpec((d, tn), lambda j: (0, j)),
                  pl.BlockSpec((1, tn), lambda j: (0, j))],
        out_specs=pl.BlockSpec((rows, tn), lambda j: (0, j)),
        out_shape=jax.ShapeDtypeStruct((rows, n), F32),
        compiler_params=_params(("arbitrary",)),
        name="mod",
    )(cc, w_ada, b_ada)


def _normproj_kernel(x_ref, mod_ref, g_ref, w_ref, o_ref, h_ref, *, sh_row, sc_row):
    @pl.when(pl.program_id(1) == 0)
    def _():
        x = x_ref[...]
        ms = jnp.mean(x * x, axis=-1, keepdims=True)
        y = x * lax.rsqrt(ms + NORM_EPS) * g_ref[...]
        sh = mod_ref[0, sh_row:sh_row + 1, :]
        sc = mod_ref[0, sc_row:sc_row + 1, :]
        h_ref[...] = (y * (1.0 + sc) + sh).astype(BF16)

    o_ref[...] = jnp.dot(h_ref[...], w_ref[...], preferred_element_type=F32)


def _normproj(x2, mod, gain, w, rows_per_mod, n_cols, sh_row, sc_row, name):
    m, d = x2.shape
    tm = _tile(rows_per_mod, 1024)
    tn = _tile(n_cols, 1024, LANES)
    per = rows_per_mod // tm
    return pl.pallas_call(
        functools.partial(_normproj_kernel, sh_row=sh_row, sc_row=sc_row),
        grid=(m // tm, n_cols // tn),
        in_specs=[pl.BlockSpec((tm, d), lambda i, j: (i, 0)),
                  pl.BlockSpec((1, 8, d), lambda i, j: (i // per, 0, 0)),
                  pl.BlockSpec((1, d), lambda i, j: (0, 0)),
                  pl.BlockSpec((d, tn), lambda i, j: (0, j))],
        out_specs=pl.BlockSpec((tm, tn), lambda i, j: (i, j)),
        out_shape=jax.ShapeDtypeStruct((m, n_cols), F32),
        scratch_shapes=[pltpu.VMEM((tm, d), BF16)],
        compiler_params=_params(("parallel", "arbitrary")),
        name=name,
    )(x2, mod, gain, w)


def _rwprep_kernel(pc_ref, pp_ref, pn_ref, mu_ref, kk_ref, ka_ref, rk_ref, w0_ref, wup_ref,
                   a0_ref, aup_ref, gup_ref,
                   at_ref, bt_ref, kt_ref, rt_ref, v_ref, pt_ref, bonus_ref, g_ref, xs_ref,
                   *, tiles_per_seq):
    i = pl.program_id(0)
    tm = pc_ref.shape[0]
    w = RW_HEAD * (kk_ref.shape[1] // RW_HEAD)
    first = (i % tiles_per_seq) == 0
    last = (i % tiles_per_seq) == tiles_per_seq - 1
    base = 3 * w
    used = base + 4 * RANK_PAD + gup_ref.shape[0]
    x = pc_ref[:, 0:used]
    prev_row = jnp.where(first, 0.0, pp_ref[7:8, 0:used])
    next_row = jnp.where(last, 0.0, pn_ref[0:1, 0:used])
    mu = mu_ref[:, 0:used]
    keep = 1.0 - mu
    half = 0.5 * mu
    _store_stencil(xs_ref, x, prev_row, next_row, lambda a, c, b: c * keep + (a + b) * half)

    r = xs_ref[:, 0:w]
    k = xs_ref[:, w:2 * w]
    v = xs_ref[:, 2 * w:3 * w]
    gd = xs_ref[:, base + 4 * RANK_PAD:used]
    g_ref[...] = _mm(_sigmoid(gd), gup_ref[...])
    v_ref[...] = v.astype(BF16)

    kx = k * kk_ref[...]
    kk = kx * lax.rsqrt(jnp.maximum(_seg_sum(kx * kx, RW_HEAD, full_precision=False), 1e-12))

    ksum = jnp.zeros_like(k)
    for d in range(2):
        wd = xs_ref[:, base + d * RANK_PAD: base + (d + 1) * RANK_PAD]
        ad = xs_ref[:, base + (2 + d) * RANK_PAD: base + (3 + d) * RANK_PAD]
        wl = w0_ref[d:d + 1, :] + _mm(jnp.tanh(wd), wup_ref[d])
        lw = -math.exp(-0.5) * _sigmoid(wl)
        a = _sigmoid(a0_ref[d:d + 1, :] + _mm(ad, aup_ref[d]))
        kd = k * (1.0 + (a - 1.0) * ka_ref[...])
        ksum = ksum + kd
        tri = _tri(rev=(d == 1))
        for c in range(tm // CHUNK):
            rows = slice(c * CHUNK, (c + 1) * CHUNK)
            lwc = lw[rows]
            cum = _mm_sel_left(tri, lwc)
            p_in = jnp.exp(cum)
            p_inv = jnp.exp(-cum)
            p_ex = jnp.exp(cum - lwc)
            at_ref[d, rows, :] = (-kk[rows] * p_ex).astype(BF16)
            bt_ref[d, rows, :] = (kk[rows] * a[rows] * p_inv).astype(BF16)
            kt_ref[d, rows, :] = (kd[rows] * p_inv).astype(BF16)
            rt_ref[d, rows, :] = (r[rows] * p_in).astype(BF16)
            tot = cum[CHUNK - 1:CHUNK] if d == 0 else cum[0:1]
            pt_ref[d, c, :, :] = jnp.exp(tot)
    bonus_ref[...] = _seg_sum(r * ksum * rk_ref[...], RW_HEAD, full_precision=False) * v


def _rwprep(p, seq_len, wts):
    m = p.shape[0]
    mu, k_k, k_a, r_k, w0, w_up, a0, a_up, g_up = wts
    w = k_k.shape[1]
    blk = mu.shape[1]
    tm = _tile(seq_len, 256, CHUNK)
    tps = seq_len // tm
    nb8 = m // 8
    full = lambda a: pl.BlockSpec(a.shape, lambda i: (0,) * a.ndim)
    feat = jax.ShapeDtypeStruct((2, m, w), BF16)
    feat_spec = pl.BlockSpec((2, tm, w), lambda i: (0, i, 0))
    row_spec = pl.BlockSpec((tm, w), lambda i: (i, 0))
    return pl.pallas_call(
        functools.partial(_rwprep_kernel, tiles_per_seq=tps),
        grid=(m // tm,),
        in_specs=[pl.BlockSpec((tm, blk), lambda i: (i, 0)),
                  pl.BlockSpec((8, blk), lambda i: (jnp.maximum(i * (tm // 8) - 1, 0), 0)),
                  pl.BlockSpec((8, blk), lambda i: (jnp.minimum((i + 1) * (tm // 8), nb8 - 1), 0)),
                  full(mu), full(k_k), full(k_a), full(r_k), full(w0), full(w_up), full(a0),
                  full(a_up), full(g_up)],
        out_specs=[feat_spec, feat_spec, feat_spec, feat_spec, row_spec,
                   pl.BlockSpec((2, tm // CHUNK, 1, w), lambda i: (0, i, 0, 0)),
                   row_spec, row_spec],
        out_shape=[feat, feat, feat, feat, jax.ShapeDtypeStruct((m, w), BF16),
                   jax.ShapeDtypeStruct((2, m // CHUNK, 1, w), F32),
                   jax.ShapeDtypeStruct((m, w), F32), jax.ShapeDtypeStruct((m, w), F32)],
        scratch_shapes=[pltpu.VMEM((tm, 3 * w + 4 * RANK_PAD + g_up.shape[0]), F32)],
        compiler_params=_params(("parallel",)),
        name="rwprep",
    )(p, p, p, mu, k_k, k_a, r_k, w0, w_up, a0, a_up, g_up)


def _pair_blockdiag(x, m0):
    zero = jnp.zeros_like(x)
    return jnp.concatenate([jnp.where(m0, x, zero), jnp.where(m0, zero, x)], axis=0)


def _tri_inverse(a, eye, blk, m0, sign):
    ts = [eye + sign * jnp.where(blk[0], x, 0.0) for x in a]
    for lvl in range(1, len(blk)):
        xs = [_mm(t, _pair_blockdiag(jnp.where(blk[lvl], x, 0.0), m0)) for t, x in zip(ts, a)]
        ts = [t + sign * _mm(x, _pair_blockdiag(t, m0)) for t, x in zip(ts, xs)]
    return ts


def _pair_masks(head_cols):
    i = lax.broadcasted_iota(jnp.int32, (CHUNK, LANES), 0)
    lane = lax.broadcasted_iota(jnp.int32, (CHUNK, LANES), 1)
    j = lane % head_cols
    strict = (j < i, j > i)
    incl = (j <= i, j >= i)
    eye = (j == i).astype(F32)
    return strict, incl, eye, _level_masks(i, j), lane < head_cols


def _rwscan_kernel(*refs, want_out):
    ins, rest = refs[:12], refs[12:]
    s0_ref = rest[0]
    if want_out:
        o_refs, s_ref, h_ref = rest[1:3], rest[3], rest[4]
    else:
        s_ref, h_ref = rest[1], rest[2]
    c = pl.program_id(1)
    n_pairs = h_ref.shape[1]

    @pl.when(c == 0)
    def _():
        h_ref[...] = s0_ref[0]

    strict, incl, eye, blk, m0 = _pair_masks(RW_HEAD)
    r2 = lax.broadcasted_iota(jnp.int32, (LANES, LANES), 0) // RW_HEAD
    c2 = lax.broadcasted_iota(jnp.int32, (LANES, LANES), 1) // RW_HEAD
    diag2 = r2 == c2

    chains = [(d, p) for d in range(2) for p in range(n_pairs)]
    cols = lambda p: slice(p * LANES, (p + 1) * LANES)
    at = [ins[6 * d + 0][0, :, cols(p)] for d, p in chains]
    bt = [ins[6 * d + 1][0, :, cols(p)] for d, p in chains]
    kt = [ins[6 * d + 2][0, :, cols(p)] for d, p in chains]
    rt = [ins[6 * d + 3][0, :, cols(p)] for d, p in chains]
    v = [ins[6 * d + 4][:, cols(p)] for d, p in chains]
    pt = [ins[6 * d + 5][0, 0, :, cols(p)] for d, p in chains]
    n = len(chains)
    bd = lambda x: _pair_blockdiag(x, m0)

    s4 = [_mm_nt(jnp.concatenate([at[i], rt[i]], axis=0),
                 jnp.concatenate([bd(bt[i]), bd(kt[i])], axis=0)) for i in range(n)]
    a_ab = [jnp.where(strict[chains[i][0]], s4[i][:CHUNK, :LANES], 0.0) for i in range(n)]
    a_ak = [jnp.where(strict[chains[i][0]], s4[i][:CHUNK, LANES:], 0.0) for i in range(n)]
    t = _tri_inverse(a_ab, eye, blk, m0, 1.0)
    av = [_mm(a_ak[i], bd(v[i])) for i in range(n)]
    wu = [_mm(t[i], jnp.concatenate([bd(at[i]), bd(av[i].astype(BF16))], axis=1)) for i in range(n)]

    ht = [h_ref[d, p] for d, p in chains]
    if want_out:
        m1 = [_mm_nt(jnp.concatenate([wu[i][:, :LANES].astype(BF16), rt[i]], axis=0), ht[i])
              for i in range(n)]
        u = [m1[i][:CHUNK] + wu[i][:, LANES:] for i in range(n)]
    else:
        u = [_mm_nt(wu[i][:, :LANES], ht[i]) + wu[i][:, LANES:] for i in range(n)]
    ub = [x.astype(BF16) for x in u]
    if want_out:
        for i, (d, p) in enumerate(chains):
            m_r = jnp.where(jnp.concatenate([incl[d], incl[d]], axis=1), s4[i][CHUNK:], 0.0)
            o = m1[i][CHUNK:] + _mm(m_r, jnp.concatenate([bd(ub[i]), bd(v[i])], axis=0))
            o_refs[d][:, cols(p)] = o
    for i, (d, p) in enumerate(chains):
        upd = _mm_tn(jnp.concatenate([ub[i], v[i]], axis=0),
                     jnp.concatenate([bt[i], kt[i]], axis=0))
        h_ref[d, p] = (ht[i] + jnp.where(diag2, upd, 0.0)) * pt[i]

    @pl.when(c == pl.num_programs(1) - 1)
    def _():
        s_ref[0] = h_ref[...]


def _rwscan(at, bt, kt, rt, v, pt, s0, batch, want_out):
    m, w = v.shape
    n_chunks = m // batch // CHUNK
    n_pairs = w // LANES
    rows = (lambda b, c: b * n_chunks + c, lambda b, c: b * n_chunks + n_chunks - 1 - c)

    in_specs, args = [], []
    for d in range(2):
        feat_spec = pl.BlockSpec((1, CHUNK, w), lambda b, c, d=d: (d, rows[d](b, c), 0))
        in_specs += [feat_spec] * 4
        in_specs += [pl.BlockSpec((CHUNK, w), lambda b, c, d=d: (rows[d](b, c), 0)),
                     pl.BlockSpec((1, 1, 1, w), lambda b, c, d=d: (d, rows[d](b, c), 0, 0))]
        args += [at, bt, kt, rt, v, pt]
    state_spec = pl.BlockSpec((1, 2, n_pairs, LANES, LANES), lambda b, c: (b, 0, 0, 0, 0))
    out_specs = [state_spec]
    out_shape = [jax.ShapeDtypeStruct(s0.shape, F32)]
    if want_out:
        out_specs = [pl.BlockSpec((CHUNK, w), lambda b, c, d=d: (rows[d](b, c), 0))
                     for d in range(2)] + out_specs
        out_shape = [jax.ShapeDtypeStruct((m, w), F32)] * 2 + out_shape
    return pl.pallas_call(
        functools.partial(_rwscan_kernel, want_out=want_out),
        grid=(batch, n_chunks),
        in_specs=in_specs + [state_spec],
        out_specs=out_specs,
        out_shape=out_shape,
        scratch_shapes=[pltpu.VMEM((2, n_pairs, LANES, LANES), F32)],
        compiler_params=_params(("parallel", "arbitrary")),
        name="rwscan_out" if want_out else "rwscan_state",
    )(*args, s0)


def _rwread_kernel(of_ref, ob_ref, bonus_ref, g_ref, gng_ref, gnb_ref, y_ref):
    o = of_ref[...] + ob_ref[...]
    inv = 1.0 / RW_HEAD
    mean = _seg_sum(o, RW_HEAD) * inv
    cen = o - mean
    var = _seg_sum(cen * cen, RW_HEAD) * inv
    on = cen * lax.rsqrt(var + RW_GN_EPS) * gng_ref[...] + gnb_ref[...]
    y_ref[...] = ((on + bonus_ref[...]) * g_ref[...]).astype(BF16)


def _rwread(o_f, o_b, bonus, g, gn_g, gn_b):
    m, w = o_f.shape
    tm = _tile(m, 512)
    row_spec = pl.BlockSpec((tm, w), lambda i: (i, 0))
    vec_spec = pl.BlockSpec((1, w), lambda i: (0, 0))
    return pl.pallas_call(
        _rwread_kernel,
        grid=(m // tm,),
        in_specs=[row_spec, row_spec, row_spec, row_spec, vec_spec, vec_spec],
        out_specs=row_spec,
        out_shape=jax.ShapeDtypeStruct((m, w), BF16),
        compiler_params=_params(("parallel",)),
        name="rwread",
    )(o_f, o_b, bonus, g, gn_g, gn_b)


def _gdprep_kernel(pc_ref, pp_ref, pn_ref, ab_ref, cw_ref, alog_ref, dtb_ref,
                   q_ref, k_ref, v_ref, gcum_ref, beta_ref, *, tiles_per_seq):
    i = pl.program_id(0)
    tm = pc_ref.shape[0]
    w = q_ref.shape[1]
    n_heads = w // GD_HEAD
    first = (i % tiles_per_seq) == 0
    last = (i % tiles_per_seq) == tiles_per_seq - 1
    x = pc_ref[:, 0:3 * w]
    prev_row = jnp.where(first, 0.0, pp_ref[7:8, 0:3 * w])
    next_row = jnp.where(last, 0.0, pn_ref[0:1, 0:3 * w])
    xm1, xp1 = _shift_rows(x, prev_row, next_row)
    y = xm1 * cw_ref[0:1, :] + x * cw_ref[1:2, :] + xp1 * cw_ref[2:3, :]
    y = y * _sigmoid(y)
    for h in range(n_heads):
        for part, ref, scale in ((0, q_ref, GD_HEAD ** -0.5), (1, k_ref, 1.0)):
            cols = slice(part * w + h * GD_HEAD, part * w + (h + 1) * GD_HEAD)
            t = y[:, cols]
            ss = jnp.sum(t * t, axis=-1, keepdims=True)
            ref[:, h * GD_HEAD:(h + 1) * GD_HEAD] = (
                t * (lax.rsqrt(jnp.maximum(ss, 1e-12)) * scale)).astype(BF16)
    v_ref[...] = y[:, 2 * w:3 * w].astype(BF16)

    ab = ab_ref[:, 6 * LANES:7 * LANES]
    a = ab[:, 0:2 * n_heads]
    b = ab[:, 2 * n_heads:4 * n_heads]
    glog = -jnp.exp(alog_ref[...]) * _softplus(a + dtb_ref[...])
    beta_ref[...] = _sigmoid(b)
    for c in range(tm // CHUNK):
        rows = slice(c * CHUNK, (c + 1) * CHUNK)
        gc = glog[rows]
        fwd = _mm_sel_left(_tri(False), gc)
        bwd = _mm_sel_left(_tri(True), gc)
        col = lax.broadcasted_iota(jnp.int32, gc.shape, 1)
        gcum_ref[rows, :] = jnp.where(col < n_heads, fwd, bwd)


def _gdprep(p, seq_len, conv_w, a_log, dt_bias, qkvz_block, ab_block, width):
    m = p.shape[0]
    blk = 4 * width
    tm = _tile(seq_len, 256, CHUNK)
    tps = seq_len // tm
    nb8 = m // 8
    n2h = a_log.shape[1]
    full = lambda a: pl.BlockSpec(a.shape, lambda i: (0,) * a.ndim)
    row_spec = pl.BlockSpec((tm, width), lambda i: (i, 0))
    small_spec = pl.BlockSpec((tm, n2h), lambda i: (i, 0))
    return pl.pallas_call(
        functools.partial(_gdprep_kernel, tiles_per_seq=tps),
        grid=(m // tm,),
        in_specs=[pl.BlockSpec((tm, blk), lambda i: (i, qkvz_block)),
                  pl.BlockSpec((8, blk), lambda i: (jnp.maximum(i * (tm // 8) - 1, 0), qkvz_block)),
                  pl.BlockSpec((8, blk),
                               lambda i: (jnp.minimum((i + 1) * (tm // 8), nb8 - 1), qkvz_block)),
                  pl.BlockSpec((tm, 8 * LANES), lambda i: (i, ab_block)),
                  full(conv_w), full(a_log), full(dt_bias)],
        out_specs=[row_spec, row_spec, row_spec, small_spec, small_spec],
        out_shape=[jax.ShapeDtypeStruct((m, width), BF16)] * 3
        + [jax.ShapeDtypeStruct((m, n2h), F32)] * 2,
        compiler_params=_params(("parallel",)),
        name="gdprep",
    )(p, p, p, p, conv_w, a_log, dt_bias)


def _gdscan_kernel(*refs, want_out):
    ins, rest = refs[:12], refs[12:]
    s0_ref = rest[0]
    if want_out:
        o_refs, s_ref, st_ref = rest[1:3], rest[3], rest[4]
    else:
        s_ref, st_ref = rest[1], rest[2]
    c = pl.program_id(1)
    n_pairs = st_ref.shape[1] // 2

    @pl.when(c == 0)
    def _():
        st_ref[...] = s0_ref[0]

    strict, incl, eye, blk, m0 = _pair_masks(CHUNK)
    bd = lambda x: _pair_blockdiag(x, m0)
    chains = [(d, p) for d in range(2) for p in range(n_pairs)]
    n = len(chains)
    hcols = lambda h: slice(h * GD_HEAD, (h + 1) * GD_HEAD)

    def head_vals(d, p, e):
        h = 2 * p + e
        q_ref, k_ref, v_ref, g_ref, _, beta_ref = ins[6 * d:6 * d + 6]
        return (q_ref[:, hcols(h)].astype(F32), k_ref[:, hcols(h)].astype(F32),
                v_ref[:, hcols(h)].astype(F32), g_ref[0, :, h:h + 1], beta_ref[0, :, h:h + 1])

    hv = [[head_vals(d, p, e) for e in range(2)] for d, p in chains]
    kb = [[hv[i][e][1] * hv[i][e][4] for e in range(2)] for i in range(n)]
    eg = [[jnp.exp(hv[i][e][3]) for e in range(2)] for i in range(n)]
    decay, s2 = [], []
    for i, (d, p) in enumerate(chains):
        gcol = jnp.where(m0, hv[i][0][3], hv[i][1][3])
        grow = ins[6 * d + 4][0, 0, p:p + 1, :]
        decay.append(jnp.where(incl[d], jnp.exp(jnp.where(incl[d], gcol - grow, 0.0)), 0.0))
        k0, k1 = hv[i][0][1], hv[i][1][1]
        zero = jnp.zeros_like(k0)
        lhs = jnp.concatenate([jnp.concatenate([kb[i][0], kb[i][1]], axis=1),
                               jnp.concatenate([hv[i][0][0], hv[i][1][0]], axis=1)], axis=0)
        rhs = jnp.concatenate([jnp.concatenate([k0, zero], axis=1),
                               jnp.concatenate([zero, k1], axis=1)], axis=0)
        s2.append(_mm_nt(lhs, rhs))
    a = [jnp.where(strict[chains[i][0]], s2[i][:CHUNK] * decay[i], 0.0) for i in range(n)]
    t = _tri_inverse(a, eye, blk, m0, -1.0)
    sol = [_mm(bd(t[i]), jnp.concatenate(
        [jnp.concatenate([hv[i][e][2] * hv[i][e][4], kb[i][e] * eg[i][e]], axis=1)
         for e in range(2)], axis=0)) for i in range(n)]

    st = [[st_ref[d, 2 * p + e] for e in range(2)] for d, p in chains]
    ws = [[_mm(jnp.concatenate([sol[i][e * CHUNK:(e + 1) * CHUNK, GD_HEAD:],
                                hv[i][e][0] * eg[i][e]], axis=0), st[i][e])
           for e in range(2)] for i in range(n)]
    v_new = [[sol[i][e * CHUNK:(e + 1) * CHUNK, :GD_HEAD] - ws[i][e][:CHUNK] for e in range(2)]
             for i in range(n)]
    if want_out:
        for i, (d, p) in enumerate(chains):
            intra = _mm(bd(s2[i][CHUNK:] * decay[i]),
                        jnp.concatenate([v_new[i][0], v_new[i][1]], axis=0))
            for e in range(2):
                o_refs[d][:, hcols(2 * p + e)] = (ws[i][e][CHUNK:]
                                                  + intra[e * CHUNK:(e + 1) * CHUNK])
    for i, (d, p) in enumerate(chains):
        for e in range(2):
            gcol = hv[i][e][3]
            g_last = jnp.min(gcol, axis=0, keepdims=True)
            k_dec = hv[i][e][1] * jnp.exp(g_last - gcol)
            st_ref[d, 2 * p + e] = st[i][e] * jnp.exp(g_last) + _mm_tn(k_dec, v_new[i][e])

    @pl.when(c == pl.num_programs(1) - 1)
    def _():
        s_ref[0] = st_ref[...]


def _gdscan(q, k, v, g, gt, beta, s0, batch, want_out):
    m, w = q.shape
    n_chunks = m // batch // CHUNK
    n_heads = w // GD_HEAD
    rows = (lambda b, c: b * n_chunks + c, lambda b, c: b * n_chunks + n_chunks - 1 - c)

    in_specs, args = [], []
    for d in range(2):
        row_spec = pl.BlockSpec((CHUNK, w), lambda b, c, d=d: (rows[d](b, c), 0))
        col_spec = pl.BlockSpec((1, CHUNK, n_heads), lambda b, c, d=d: (d, rows[d](b, c), 0))
        in_specs += [row_spec, row_spec, row_spec, col_spec,
                     pl.BlockSpec((1, 1, n_heads // 2, LANES),
                                  lambda b, c, d=d: (d, rows[d](b, c), 0, 0)),
                     col_spec]
        args += [q, k, v, g, gt, beta]
    state_spec = pl.BlockSpec((1, 2, n_heads, GD_HEAD, GD_HEAD), lambda b, c: (b, 0, 0, 0, 0))
    out_specs = [state_spec]
    out_shape = [jax.ShapeDtypeStruct(s0.shape, F32)]
    if want_out:
        out_specs = [pl.BlockSpec((CHUNK, w), lambda b, c, d=d: (rows[d](b, c), 0))
                     for d in range(2)] + out_specs
        out_shape = [jax.ShapeDtypeStruct((m, w), F32)] * 2 + out_shape
    return pl.pallas_call(
        functools.partial(_gdscan_kernel, want_out=want_out),
        grid=(batch, n_chunks),
        in_specs=in_specs + [state_spec],
        out_specs=out_specs,
        out_shape=out_shape,
        scratch_shapes=[pltpu.VMEM((2, n_heads, GD_HEAD, GD_HEAD), F32)],
        compiler_params=_params(("parallel", "arbitrary")),
        name="gdscan_out" if want_out else "gdscan_state",
    )(*args, s0)


def _gdread_kernel(of_ref, ob_ref, z_ref, ng_ref, y_ref):
    o = of_ref[...] + ob_ref[...]
    z = z_ref[...]
    gate = z * _sigmoid(z)
    for h in range(o.shape[1] // GD_HEAD):
        cols = slice(h * GD_HEAD, (h + 1) * GD_HEAD)
        oh = o[:, cols]
        ms = jnp.mean(oh * oh, axis=-1, keepdims=True)
        y_ref[:, cols] = (oh * lax.rsqrt(ms + NORM_EPS) * ng_ref[...] * gate[:, cols]).astype(BF16)


def _gdread(o_f, o_b, p, z_block, norm_g):
    m, w = o_f.shape
    tm = _tile(m, 512)
    return pl.pallas_call(
        _gdread_kernel,
        grid=(m // tm,),
        in_specs=[pl.BlockSpec((tm, w), lambda i: (i, 0)),
                  pl.BlockSpec((tm, w), lambda i: (i, 0)),
                  pl.BlockSpec((tm, w), lambda i: (i, z_block)),
                  pl.BlockSpec((1, GD_HEAD), lambda i: (0, 0))],
        out_specs=pl.BlockSpec((tm, w), lambda i: (i, 0)),
        out_shape=jax.ShapeDtypeStruct((m, w), BF16),
        compiler_params=_params(("parallel",)),
        name="gdread",
    )(o_f, o_b, p, norm_g)


def _merge1_kernel(ya_ref, yb_ref, wa_ref, wb_ref, ga_ref, gb_ref, o_ref):
    a = jnp.dot(ya_ref[...], wa_ref[...], preferred_element_type=F32)
    b = jnp.dot(yb_ref[...], wb_ref[...], preferred_element_type=F32)
    o_ref[...] = (_sigmoid(ga_ref[...]) * a + _sigmoid(gb_ref[...]) * b).astype(BF16)


def _merge1(ya, yb, wa, wb, p, gate_col0):
    m, ka = ya.shape
    kb = yb.shape[1]
    d = wa.shape[1]
    tm = _tile(m, 1024)
    tn = _tile(d, 1024, LANES)
    ga0 = gate_col0 // tn
    gb0 = (gate_col0 + d) // tn
    return pl.pallas_call(
        _merge1_kernel,
        grid=(m // tm, d // tn),
        in_specs=[pl.BlockSpec((tm, ka), lambda i, j: (i, 0)),
                  pl.BlockSpec((tm, kb), lambda i, j: (i, 0)),
                  pl.BlockSpec((ka, tn), lambda i, j: (0, j)),
                  pl.BlockSpec((kb, tn), lambda i, j: (0, j)),
                  pl.BlockSpec((tm, tn), lambda i, j: (i, ga0 + j)),
                  pl.BlockSpec((tm, tn), lambda i, j: (i, gb0 + j))],
        out_specs=pl.BlockSpec((tm, tn), lambda i, j: (i, j)),
        out_shape=jax.ShapeDtypeStruct((m, d), BF16),
        compiler_params=_params(("parallel", "arbitrary")),
        name="merge1",
    )(ya, yb, wa, wb, p, p)


def _merge2_kernel(mg_ref, wo_ref, x_ref, mod_ref, o_ref, *, gate_row):
    y = jnp.dot(mg_ref[...], wo_ref[...], preferred_element_type=F32)
    o_ref[...] = x_ref[...] + mod_ref[0, gate_row:gate_row + 1, :] * y


def _merge2(merged, wo, x2, mod, rows_per_mod, gate_row):
    m, d = x2.shape
    tm = _tile(rows_per_mod, 1024)
    tn = _tile(d, 1024, LANES)
    per = rows_per_mod // tm
    return pl.pallas_call(
        functools.partial(_merge2_kernel, gate_row=gate_row),
        grid=(m // tm, d // tn),
        in_specs=[pl.BlockSpec((tm, d), lambda i, j: (i, 0)),
                  pl.BlockSpec((d, tn), lambda i, j: (0, j)),
                  pl.BlockSpec((tm, tn), lambda i, j: (i, j)),
                  pl.BlockSpec((1, 8, tn), lambda i, j: (i // per, 0, j))],
        out_specs=pl.BlockSpec((tm, tn), lambda i, j: (i, j)),
        out_shape=jax.ShapeDtypeStruct((m, d), F32),
        compiler_params=_params(("parallel", "arbitrary")),
        name="merge2",
    )(merged, wo, x2, mod)


def _conv3x3_gelu(g, cw_ref):
    n = g.shape[0]
    col = lax.broadcasted_iota(jnp.int32, g.shape, 0) % GRID_W
    left = jnp.where(col > 0, pltpu.roll(g, 1, 0), 0.0)
    right = jnp.where(col < GRID_W - 1, pltpu.roll(g, n - 1, 0), 0.0)
    lines = [left * cw_ref[3 * kh:3 * kh + 1, :] + g * cw_ref[3 * kh + 1:3 * kh + 2, :]
             + right * cw_ref[3 * kh + 2:3 * kh + 3, :] for kh in range(3)]
    pad = jnp.zeros((GRID_W, g.shape[1]), F32)
    acc = (lines[1] + jnp.concatenate([pad, lines[0][:n - GRID_W]], axis=0)
           + jnp.concatenate([lines[2][GRID_W:], pad], axis=0))
    return 0.5 * acc * (1.0 + lax.erf(acc * (2.0 ** -0.5)))


def _ffn_act_kernel(x_ref, mod_ref, g_ref, w1g_ref, w1v_ref, cw_ref, o_ref, h_ref, gate_ref, val_ref,
                    *, sh_row, sc_row, prologue_rows, dot_rows):
    j = pl.program_id(1)

    @pl.when(j == 0)
    def _():
        sh = mod_ref[0, sh_row:sh_row + 1, :]
        sc = mod_ref[0, sc_row:sc_row + 1, :]

        def body(r, carry):
            rows = pl.ds(pl.multiple_of(r * prologue_rows, prologue_rows), prologue_rows)
            x = x_ref[rows, :]
            ms = jnp.mean(x * x, axis=-1, keepdims=True)
            y = x * lax.rsqrt(ms + NORM_EPS) * g_ref[...]
            h_ref[rows, :] = (y * (1.0 + sc) + sh).astype(BF16)
            return carry

        lax.fori_loop(0, x_ref.shape[0] // prologue_rows, body, 0)
        gate_ref[1] = jnp.zeros(gate_ref.shape[1:], F32)
        val_ref[1] = jnp.zeros(val_ref.shape[1:], F32)

    slot = j % 2
    o_ref[...] = (_conv3x3_gelu(gate_ref[1 - slot], cw_ref) * val_ref[1 - slot]).astype(BF16)
    w1g = w1g_ref[...].astype(BF16)
    w1v = w1v_ref[...].astype(BF16)
    for r0 in range(0, h_ref.shape[0], dot_rows):
        rows = slice(r0, r0 + dot_rows)
        hh = h_ref[rows, :]
        gate_ref[slot, rows, :] = jnp.dot(hh, w1g, preferred_element_type=F32)
        val_ref[slot, rows, :] = jnp.dot(hh, w1v, preferred_element_type=F32)


def _ffn_act(x1, mod, gain, w1, conv_w, batch, sh_row, sc_row):
    m, d = x1.shape
    dff = conv_w.shape[1]
    seq = m // batch
    tf = _tile(dff, 256, LANES)
    nblk = dff // tf
    cur = lambda j: jnp.minimum(j, nblk - 1)
    prev = lambda j: jnp.maximum(j - 1, 0)
    return pl.pallas_call(
        functools.partial(_ffn_act_kernel, sh_row=sh_row, sc_row=sc_row,
                          prologue_rows=_tile(seq, 256), dot_rows=_tile(seq, 256)),
        grid=(batch, nblk + 1),
        in_specs=[pl.BlockSpec((seq, d), lambda b, j: (b, 0), pipeline_mode=pl.Buffered(1)),
                  pl.BlockSpec((1, 8, d), lambda b, j: (b, 0, 0)),
                  pl.BlockSpec((1, d), lambda b, j: (0, 0)),
                  pl.BlockSpec((d, tf), lambda b, j: (0, cur(j))),
                  pl.BlockSpec((d, tf), lambda b, j: (0, nblk + cur(j))),
                  pl.BlockSpec((conv_w.shape[0], tf), lambda b, j: (0, prev(j)))],
        out_specs=pl.BlockSpec((seq, tf), lambda b, j: (b, prev(j))),
        out_shape=jax.ShapeDtypeStruct((m, dff), BF16),
        scratch_shapes=[pltpu.VMEM((seq, d), BF16),
                        pltpu.VMEM((2, seq, tf), F32),
                        pltpu.VMEM((2, seq, tf), F32)],
        compiler_params=_params(("parallel", "arbitrary")),
        name="ffn_act",
    )(x1, mod, gain, w1, w1, conv_w)


def _ffn_out_kernel(act_ref, w2_ref, x_ref, mod_ref, g_ref, o_ref, acc_ref, *, gate_row):
    kstep = pl.program_id(1)

    @pl.when(kstep == 0)
    def _():
        acc_ref[...] = jnp.zeros_like(acc_ref)

    acc_ref[...] += jnp.dot(act_ref[...], w2_ref[...], preferred_element_type=F32)

    @pl.when(kstep == pl.num_programs(1) - 1)
    def _():
        y = x_ref[...] + mod_ref[0, gate_row:gate_row + 1, :] * acc_ref[...]
        ms = jnp.mean(y * y, axis=-1, keepdims=True)
        o_ref[...] = y * lax.rsqrt(ms + NORM_EPS) * g_ref[...]


def _ffn_out(act, w2, x1, mod, final_g, rows_per_mod, gate_row):
    m, d = x1.shape
    dff = act.shape[1]
    tm = _tile(rows_per_mod, 512)
    tk = _tile(dff, 1408, LANES)
    per = rows_per_mod // tm
    return pl.pallas_call(
        functools.partial(_ffn_out_kernel, gate_row=gate_row),
        grid=(m // tm, dff // tk),
        in_specs=[pl.BlockSpec((tm, tk), lambda i, k: (i, k)),
                  pl.BlockSpec((tk, d), lambda i, k: (k, 0)),
                  pl.BlockSpec((tm, d), lambda i, k: (i, 0)),
                  pl.BlockSpec((1, 8, d), lambda i, k: (i // per, 0, 0)),
                  pl.BlockSpec((1, d), lambda i, k: (0, 0))],
        out_specs=pl.BlockSpec((tm, d), lambda i, k: (i, 0)),
        out_shape=jax.ShapeDtypeStruct((m, d), F32),
        scratch_shapes=[pltpu.VMEM((tm, d), F32)],
        compiler_params=_params(("parallel", "arbitrary")),
        name="ffn_out",
    )(act, w2, x1, mod, final_g)


def _pad_cols(a, width):
    return jnp.pad(a, [(0, 0)] * (a.ndim - 1) + [(0, width - a.shape[-1])])


def _pad_rank(a):
    return jnp.pad(a, [(0, 0)] * (a.ndim - 2) + [(0, RANK_PAD - a.shape[-2]), (0, 0)])


def kernel(x, c, ctx, c_ctx, w_ada, b_ada, norm1_g, norm2_g, w_in, rw_mu, rw_k_k, rw_k_a, rw_r_k, rw_w0, rw_w_up, rw_a0, rw_a_up, rw_g_up, rw_gn_g, rw_gn_b, gd_conv_w, gd_a_log, gd_dt_bias, gd_norm_g, w_a_out, w_b_out, w_o, ffn_w1, ffn_conv_w, ffn_w2, final_norm_g):
    batch, seq, d = x.shape
    ctx_len = ctx.shape[1]
    assert w_ada.shape[0] == 1, "single layer only"
    rw_w = rw_k_k.shape[1]
    gd_w = w_b_out.shape[1]
    dec_rank = rw_w_up.shape[2]
    icl_rank = rw_a_up.shape[2]
    gate_rank = rw_g_up.shape[1]
    gd_heads = gd_a_log.shape[2]
    assert max(dec_rank, icl_rank) <= RANK_PAD and 4 * gd_heads <= 2 * LANES
    assert seq % CHUNK == 0 and ctx_len % CHUNK == 0 and seq % GRID_W == 0
    assert 3 * rw_w == 3 * gd_w and gate_rank <= 2 * LANES

    low_w = 8 * LANES
    blk0 = 3 * rw_w + low_w
    assert blk0 == 4 * gd_w
    wi = w_in[0]
    o_rw = 3 * rw_w
    o_gd = o_rw + 2 * dec_rank + 2 * icl_rank + gate_rank
    o_ab = o_gd + 4 * gd_w
    o_gate = o_ab + 4 * gd_heads

    def pack_cols(a):
        pieces = [a[..., :o_rw]]
        off = o_rw
        for r in (dec_rank, dec_rank, icl_rank, icl_rank):
            pieces.append(_pad_cols(a[..., off:off + r], RANK_PAD))
            off += r
        pieces.append(_pad_cols(a[..., off:off + gate_rank], 2 * LANES))
        return pieces

    w_pack = jnp.concatenate(
        pack_cols(wi) + [_pad_cols(wi[:, o_ab:o_gate], 2 * LANES), wi[:, o_gd:o_ab], wi[:, o_gate:]],
        axis=1).astype(BF16)
    n_ctx_cols = 2 * blk0
    gate_col0 = 2 * blk0
    mu_pack = jnp.concatenate(pack_cols(rw_mu) + [jnp.zeros((1, 2 * LANES), F32)], axis=1)

    cc = jnp.concatenate([c, c_ctx[None, :], jnp.zeros((16 - batch - 1, d), F32)], axis=0)
    mods = _mod(cc, w_ada[0], b_ada)
    mod_lat = _pad_rows8(mods[:batch].reshape(batch, 6, d))
    mod_ctx = _pad_rows8(mods[batch:batch + 1].reshape(1, 6, d))

    x2 = x.reshape(batch * seq, d)
    ctx2 = ctx.reshape(batch * ctx_len, d)
    p_lat = _normproj(x2, mod_lat, norm1_g, w_pack, seq, w_pack.shape[1], 0, 1, "inproj_lat")
    p_ctx = _normproj(ctx2, mod_ctx, norm1_g, w_pack, batch * ctx_len, n_ctx_cols, 0, 1, "inproj_ctx")

    rw_wts = (mu_pack, rw_k_k, rw_k_a, rw_r_k, rw_w0[0], _pad_rank(rw_w_up[0]).astype(BF16),
              rw_a0[0], _pad_rank(rw_a_up[0]).astype(BF16),
              jnp.pad(rw_g_up[0], ((0, 2 * LANES - gate_rank), (0, 0))).astype(BF16))
    n_pairs = rw_w // LANES
    s0 = jnp.zeros((batch, 2, n_pairs, LANES, LANES), F32)
    f_ctx = _rwprep(p_ctx, ctx_len, rw_wts)
    f_lat = _rwprep(p_lat, seq, rw_wts)
    (s_ctx,) = _rwscan(*f_ctx[:6], s0, batch, want_out=False)
    o_rw_f, o_rw_b, _ = _rwscan(*f_lat[:6], s_ctx, batch, want_out=True)
    ya = _rwread(o_rw_f, o_rw_b, f_lat[6], f_lat[7], rw_gn_g, rw_gn_b)

    s0g = jnp.zeros((batch, 2, gd_heads, GD_HEAD, GD_HEAD), F32)
    a_log2 = gd_a_log[0].reshape(1, 2 * gd_heads)
    dtb2 = gd_dt_bias[0].reshape(1, 2 * gd_heads)

    def gd_feats(p, seq_len):
        q, k, v, gcum, beta = _gdprep(p, seq_len, gd_conv_w[0], a_log2, dtb2, 1, 3, gd_w)
        m = p.shape[0]
        g3 = gcum.reshape(m, 2, gd_heads).transpose(1, 0, 2)
        gt = g3.reshape(2, m // CHUNK, CHUNK, gd_heads).transpose(0, 1, 3, 2)
        gt = gt.reshape(2, m // CHUNK, gd_heads // 2, 2 * CHUNK)
        b3 = beta.reshape(m, 2, gd_heads).transpose(1, 0, 2)
        return q, k, v, g3, gt, b3

    (sg_ctx,) = _gdscan(*gd_feats(p_ctx, ctx_len), s0g, batch, want_out=False)
    o_gd_f, o_gd_b, _ = _gdscan(*gd_feats(p_lat, seq), sg_ctx, batch, want_out=True)
    yb = _gdread(o_gd_f, o_gd_b, p_lat, 2 * blk0 // gd_w - 1, gd_norm_g)

    merged = _merge1(ya, yb, w_a_out[0].astype(BF16), w_b_out[0].astype(BF16), p_lat, gate_col0)
    x1 = _merge2(merged, w_o[0].astype(BF16), x2, mod_lat, seq, 2)

    act = _ffn_act(x1, mod_lat, norm2_g, ffn_w1[0],
                   ffn_conv_w[0].reshape(-1, ffn_conv_w.shape[-1]), batch, 3, 4)
    out = _ffn_out(act, ffn_w2[0].astype(BF16), x1, mod_lat, final_norm_g[None, :], seq, 5)
    return out.reshape(batch, seq, d)


def _pad_rows8(a):
    return jnp.pad(a, ((0, 0), (0, 8 - a.shape[1]), (0, 0)))
```

```python
import functools
import math

import jax
import jax.numpy as jnp
from jax import lax
from jax.experimental import pallas as pl
from jax.experimental.pallas import tpu as pltpu

F32 = jnp.float32
BF16 = jnp.bfloat16
HIGHEST = lax.Precision.HIGHEST

NORM_EPS = 1e-6
RW_GN_EPS = 64e-5
RW_HEAD = 64
GD_HEAD = 128
LANES = 128
CHUNK = 64
GRID_W = 64
RANK_PAD = 128
VMEM_LIMIT = 56 * 1024 * 1024


def _params(sem):
    return pltpu.CompilerParams(dimension_semantics=sem, vmem_limit_bytes=VMEM_LIMIT)


def _tile(n, pref, mult=8):
    if n <= pref:
        return n
    t = (pref // mult) * mult
    while t >= mult:
        if n % t == 0:
            return t
        t -= mult
    return n


def _mm(a, b):
    return jnp.dot(a.astype(BF16), b.astype(BF16), preferred_element_type=F32)


def _mm_nt(a, b):
    return lax.dot_general(a.astype(BF16), b.astype(BF16), (((1,), (1,)), ((), ())),
                           preferred_element_type=F32)


def _mm_tn(a, b):
    return lax.dot_general(a.astype(BF16), b.astype(BF16), (((0,), (0,)), ((), ())),
                           preferred_element_type=F32)


def _mm_hi(a, b):
    return jnp.dot(a, b, precision=HIGHEST, preferred_element_type=F32)


def _split3(x):
    x1 = x.astype(BF16)
    r1 = x - x1.astype(F32)
    x2 = r1.astype(BF16)
    x3 = (r1 - x2.astype(F32)).astype(BF16)
    return x1, x2, x3


def _mm_sel_left(c, x):
    cb = c.astype(BF16)
    return jnp.dot(jnp.concatenate([cb, cb, cb], axis=1), jnp.concatenate(_split3(x), axis=0),
                   preferred_element_type=F32)


def _mm_sel_right(x, c):
    cb = c.astype(BF16)
    return jnp.dot(jnp.concatenate(_split3(x), axis=1), jnp.concatenate([cb, cb, cb], axis=0),
                   preferred_element_type=F32)


def _softplus(x):
    return jnp.maximum(x, 0.0) + jnp.log(1.0 + jnp.exp(-jnp.abs(x)))


def _sigmoid(x):
    return jax.nn.sigmoid(x)


def _seg_ones(width):
    i = lax.broadcasted_iota(jnp.int32, (LANES, LANES), 0) // width
    j = lax.broadcasted_iota(jnp.int32, (LANES, LANES), 1) // width
    return (i == j).astype(F32)


def _seg_sum(x, width, full_precision=True):
    e = _seg_ones(width)
    n = x.shape[-1] // LANES
    mm = _mm_sel_right if full_precision else _mm
    parts = [mm(x[:, g * LANES:(g + 1) * LANES], e) for g in range(n)]
    return parts[0] if n == 1 else jnp.concatenate(parts, axis=-1)


def _tri(rev):
    i = lax.broadcasted_iota(jnp.int32, (CHUNK, CHUNK), 0)
    j = lax.broadcasted_iota(jnp.int32, (CHUNK, CHUNK), 1)
    return ((j >= i) if rev else (j <= i)).astype(F32)


def _level_masks(i, j):
    masks = [(i // 2) == (j // 2)]
    s = 2
    while s < CHUNK:
        masks.append(((i // (2 * s)) == (j // (2 * s))) & ((i // s) != (j // s)))
        s *= 2
    return masks


def _shift_rows(x, prev_row, next_row):
    n = x.shape[0]
    row = lax.broadcasted_iota(jnp.int32, x.shape, 0)
    xm1 = jnp.where(row == 0, prev_row, pltpu.roll(x, 1, 0))
    xp1 = jnp.where(row == n - 1, next_row, pltpu.roll(x, n - 1, 0))
    return xm1, xp1


def _store_stencil(dst_ref, x, prev_row, next_row, f):
    n = x.shape[0]
    dst_ref[...] = f(pltpu.roll(x, 1, 0), x, pltpu.roll(x, n - 1, 0))
    dst_ref[0:1, :] = f(prev_row, x[0:1], x[1:2])
    dst_ref[n - 1:n, :] = f(x[n - 2:n - 1], x[n - 1:n], next_row)


def _mod_kernel(c_ref, w_ref, b_ref, o_ref):
    c = c_ref[...]
    s = c * _sigmoid(c)
    o_ref[...] = _mm_hi(s, w_ref[...]) + b_ref[...]


def _mod(cc, w_ada, b_ada):
    rows, d = cc.shape
    n = w_ada.shape[1]
    tn = _tile(n, 1024, LANES)
    return pl.pallas_call(
        _mod_kernel,
        grid=(n // tn,),
        in_specs=[pl.BlockSpec((rows, d), lambda j: (0, 0)),
                  pl.BlockSpec((d, tn), lambda j: (0, j)),
                  pl.BlockSpec((1, tn), lambda j: (0, j))],
        out_specs=pl.BlockSpec((rows, tn), lambda j: (0, j)),
        out_shape=jax.ShapeDtypeStruct((rows, n), F32),
        compiler_params=_params(("arbitrary",)),
        name="mod",
    )(cc, w_ada, b_ada)


def _normproj_kernel(x_ref, mod_ref, g_ref, w_ref, o_ref, h_ref, *, sh_row, sc_row):
    @pl.when(pl.program_id(1) == 0)
    def _():
        x = x_ref[...]
        ms = jnp.mean(x * x, axis=-1, keepdims=True)
        y = x * lax.rsqrt(ms + NORM_EPS) * g_ref[...]
        sh = mod_ref[0, sh_row:sh_row + 1, :]
        sc = mod_ref[0, sc_row:sc_row + 1, :]
        h_ref[...] = (y * (1.0 + sc) + sh).astype(BF16)

    o_ref[...] = jnp.dot(h_ref[...], w_ref[...], preferred_element_type=F32)


def _normproj(x2, mod, gain, w, rows_per_mod, n_cols, sh_row, sc_row, name):
    m, d = x2.shape
    tm = _tile(rows_per_mod, 1024)
    tn = _tile(n_cols, 1024, LANES)
    per = rows_per_mod // tm
    return pl.pallas_call(
        functools.partial(_normproj_kernel, sh_row=sh_row, sc_row=sc_row),
        grid=(m // tm, n_cols // tn),
        in_specs=[pl.BlockSpec((tm, d), lambda i, j: (i, 0)),
                  pl.BlockSpec((1, 8, d), lambda i, j: (i // per, 0, 0)),
                  pl.BlockSpec((1, d), lambda i, j: (0, 0)),
                  pl.BlockSpec((d, tn), lambda i, j: (0, j))],
        out_specs=pl.BlockSpec((tm, tn), lambda i, j: (i, j)),
        out_shape=jax.ShapeDtypeStruct((m, n_cols), F32),
        scratch_shapes=[pltpu.VMEM((tm, d), BF16)],
        compiler_params=_params(("parallel", "arbitrary")),
        name=name,
    )(x2, mod, gain, w)


def _rwprep_kernel(pc_ref, pp_ref, pn_ref, mu_ref, kk_ref, ka_ref, rk_ref, w0_ref, wup_ref,
                   a0_ref, aup_ref, gup_ref,
                   at_ref, bt_ref, kt_ref, rt_ref, v_ref, pt_ref, bonus_ref, g_ref, xs_ref,
                   *, tiles_per_seq):
    i = pl.program_id(0)
    tm = pc_ref.shape[0]
    w = RW_HEAD * (kk_ref.shape[1] // RW_HEAD)
    first = (i % tiles_per_seq) == 0
    last = (i % tiles_per_seq) == tiles_per_seq - 1
    base = 3 * w
    used = base + 4 * RANK_PAD + gup_ref.shape[0]
    x = pc_ref[:, 0:used]
    prev_row = jnp.where(first, 0.0, pp_ref[7:8, 0:used])
    next_row = jnp.where(last, 0.0, pn_ref[0:1, 0:used])
    mu = mu_ref[:, 0:used]
    keep = 1.0 - mu
    half = 0.5 * mu
    _store_stencil(xs_ref, x, prev_row, next_row, lambda a, c, b: c * keep + (a + b) * half)

    r = xs_ref[:, 0:w]
    k = xs_ref[:, w:2 * w]
    v = xs_ref[:, 2 * w:3 * w]
    gd = xs_ref[:, base + 4 * RANK_PAD:used]
    g_ref[...] = _mm(_sigmoid(gd), gup_ref[...])
    v_ref[...] = v.astype(BF16)

    kx = k * kk_ref[...]
    kk = kx * lax.rsqrt(jnp.maximum(_seg_sum(kx * kx, RW_HEAD, full_precision=False), 1e-12))

    ksum = jnp.zeros_like(k)
    for d in range(2):
        wd = xs_ref[:, base + d * RANK_PAD: base + (d + 1) * RANK_PAD]
        ad = xs_ref[:, base + (2 + d) * RANK_PAD: base + (3 + d) * RANK_PAD]
        wl = w0_ref[d:d + 1, :] + _mm(jnp.tanh(wd), wup_ref[d])
        lw = -math.exp(-0.5) * _sigmoid(wl)
        a = _sigmoid(a0_ref[d:d + 1, :] + _mm(ad, aup_ref[d]))
        kd = k * (1.0 + (a - 1.0) * ka_ref[...])
        ksum = ksum + kd
        tri = _tri(rev=(d == 1))
        for c in range(tm // CHUNK):
            rows = slice(c * CHUNK, (c + 1) * CHUNK)
            lwc = lw[rows]
            cum = _mm_sel_left(tri, lwc)
            p_in = jnp.exp(cum)
            p_inv = jnp.exp(-cum)
            p_ex = jnp.exp(cum - lwc)
            at_ref[d, rows, :] = (-kk[rows] * p_ex).astype(BF16)
            bt_ref[d, rows, :] = (kk[rows] * a[rows] * p_inv).astype(BF16)
            kt_ref[d, rows, :] = (kd[rows] * p_inv).astype(BF16)
            rt_ref[d, rows, :] = (r[rows] * p_in).astype(BF16)
            tot = cum[CHUNK - 1:CHUNK] if d == 0 else cum[0:1]
            pt_ref[d, c, :, :] = jnp.exp(tot)
    bonus_ref[...] = _seg_sum(r * ksum * rk_ref[...], RW_HEAD, full_precision=False) * v


def _rwprep(p, seq_len, wts):
    m = p.shape[0]
    mu, k_k, k_a, r_k, w0, w_up, a0, a_up, g_up = wts
    w = k_k.shape[1]
    blk = mu.shape[1]
    tm = _tile(seq_len, 256, CHUNK)
    tps = seq_len // tm
    nb8 = m // 8
    full = lambda a: pl.BlockSpec(a.shape, lambda i: (0,) * a.ndim)
    feat = jax.ShapeDtypeStruct((2, m, w), BF16)
    feat_spec = pl.BlockSpec((2, tm, w), lambda i: (0, i, 0))
    row_spec = pl.BlockSpec((tm, w), lambda i: (i, 0))
    return pl.pallas_call(
        functools.partial(_rwprep_kernel, tiles_per_seq=tps),
        grid=(m // tm,),
        in_specs=[pl.BlockSpec((tm, blk), lambda i: (i, 0)),
                  pl.BlockSpec((8, blk), lambda i: (jnp.maximum(i * (tm // 8) - 1, 0), 0)),
                  pl.BlockSpec((8, blk), lambda i: (jnp.minimum((i + 1) * (tm // 8), nb8 - 1), 0)),
                  full(mu), full(k_k), full(k_a), full(r_k), full(w0), full(w_up), full(a0),
                  full(a_up), full(g_up)],
        out_specs=[feat_spec, feat_spec, feat_spec, feat_spec, row_spec,
                   pl.BlockSpec((2, tm // CHUNK, 1, w), lambda i: (0, i, 0, 0)),
                   row_spec, row_spec],
        out_shape=[feat, feat, feat, feat, jax.ShapeDtypeStruct((m, w), BF16),
                   jax.ShapeDtypeStruct((2, m // CHUNK, 1, w), F32),
                   jax.ShapeDtypeStruct((m, w), F32), jax.ShapeDtypeStruct((m, w), F32)],
        scratch_shapes=[pltpu.VMEM((tm, 3 * w + 4 * RANK_PAD + g_up.shape[0]), F32)],
        compiler_params=_params(("parallel",)),
        name="rwprep",
    )(p, p, p, mu, k_k, k_a, r_k, w0, w_up, a0, a_up, g_up)


def _pair_blockdiag(x, m0):
    zero = jnp.zeros_like(x)
    return jnp.concatenate([jnp.where(m0, x, zero), jnp.where(m0, zero, x)], axis=0)


def _tri_inverse(a, eye, blk, m0, sign):
    ts = [eye + sign * jnp.where(blk[0], x, 0.0) for x in a]
    for lvl in range(1, len(blk)):
        xs = [_mm(t, _pair_blockdiag(jnp.where(blk[lvl], x, 0.0), m0)) for t, x in zip(ts, a)]
        ts = [t + sign * _mm(x, _pair_blockdiag(t, m0)) for t, x in zip(ts, xs)]
    return ts


def _pair_masks(head_cols):
    i = lax.broadcasted_iota(jnp.int32, (CHUNK, LANES), 0)
    lane = lax.broadcasted_iota(jnp.int32, (CHUNK, LANES), 1)
    j = lane % head_cols
    strict = (j < i, j > i)
    incl = (j <= i, j >= i)
    eye = (j == i).astype(F32)
    return strict, incl, eye, _level_masks(i, j), lane < head_cols


def _rwscan_kernel(*refs, want_out):
    ins, rest = refs[:12], refs[12:]
    s0_ref = rest[0]
    if want_out:
        o_refs, s_ref, h_ref = rest[1:3], rest[3], rest[4]
    else:
        s_ref, h_ref = rest[1], rest[2]
    c = pl.program_id(1)
    n_pairs = h_ref.shape[1]

    @pl.when(c == 0)
    def _():
        h_ref[...] = s0_ref[0]

    strict, incl, eye, blk, m0 = _pair_masks(RW_HEAD)
    r2 = lax.broadcasted_iota(jnp.int32, (LANES, LANES), 0) // RW_HEAD
    c2 = lax.broadcasted_iota(jnp.int32, (LANES, LANES), 1) // RW_HEAD
    diag2 = r2 == c2

    chains = [(d, p) for d in range(2) for p in range(n_pairs)]
    cols = lambda p: slice(p * LANES, (p + 1) * LANES)
    at = [ins[6 * d + 0][0, :, cols(p)] for d, p in chains]
    bt = [ins[6 * d + 1][0, :, cols(p)] for d, p in chains]
    kt = [ins[6 * d + 2][0, :, cols(p)] for d, p in chains]
    rt = [ins[6 * d + 3][0, :, cols(p)] for d, p in chains]
    v = [ins[6 * d + 4][:, cols(p)] for d, p in chains]
    pt = [ins[6 * d + 5][0, 0, :, cols(p)] for d, p in chains]
    n = len(chains)
    bd = lambda x: _pair_blockdiag(x, m0)

    s4 = [_mm_nt(jnp.concatenate([at[i], rt[i]], axis=0),
                 jnp.concatenate([bd(bt[i]), bd(kt[i])], axis=0)) for i in range(n)]
    a_ab = [jnp.where(strict[chains[i][0]], s4[i][:CHUNK, :LANES], 0.0) for i in range(n)]
    a_ak = [jnp.where(strict[chains[i][0]], s4[i][:CHUNK, LANES:], 0.0) for i in range(n)]
    t = _tri_inverse(a_ab, eye, blk, m0, 1.0)
    av = [_mm(a_ak[i], bd(v[i])) for i in range(n)]
    wu = [_mm(t[i], jnp.concatenate([bd(at[i]), bd(av[i].astype(BF16))], axis=1)) for i in range(n)]

    ht = [h_ref[d, p] for d, p in chains]
    if want_out:
        m1 = [_mm_nt(jnp.concatenate([wu[i][:, :LANES].astype(BF16), rt[i]], axis=0), ht[i])
              for i in range(n)]
        u = [m1[i][:CHUNK] + wu[i][:, LANES:] for i in range(n)]
    else:
        u = [_mm_nt(wu[i][:, :LANES], ht[i]) + wu[i][:, LANES:] for i in range(n)]
    ub = [x.astype(BF16) for x in u]
    if want_out:
        for i, (d, p) in enumerate(chains):
            m_r = jnp.where(jnp.concatenate([incl[d], incl[d]], axis=1), s4[i][CHUNK:], 0.0)
            o = m1[i][CHUNK:] + _mm(m_r, jnp.concatenate([bd(ub[i]), bd(v[i])], axis=0))
            o_refs[d][:, cols(p)] = o
    for i, (d, p) in enumerate(chains):
        upd = _mm_tn(jnp.concatenate([ub[i], v[i]], axis=0),
                     jnp.concatenate([bt[i], kt[i]], axis=0))
        h_ref[d, p] = (ht[i] + jnp.where(diag2, upd, 0.0)) * pt[i]

    @pl.when(c == pl.num_programs(1) - 1)
    def _():
        s_ref[0] = h_ref[...]


def _rwscan(at, bt, kt, rt, v, pt, s0, batch, want_out):
    m, w = v.shape
    n_chunks = m // batch // CHUNK
    n_pairs = w // LANES
    rows = (lambda b, c: b * n_chunks + c, lambda b, c: b * n_chunks + n_chunks - 1 - c)

    in_specs, args = [], []
    for d in range(2):
        feat_spec = pl.BlockSpec((1, CHUNK, w), lambda b, c, d=d: (d, rows[d](b, c), 0))
        in_specs += [feat_spec] * 4
        in_specs += [pl.BlockSpec((CHUNK, w), lambda b, c, d=d: (rows[d](b, c), 0)),
                     pl.BlockSpec((1, 1, 1, w), lambda b, c, d=d: (d, rows[d](b, c), 0, 0))]
        args += [at, bt, kt, rt, v, pt]
    state_spec = pl.BlockSpec((1, 2, n_pairs, LANES, LANES), lambda b, c: (b, 0, 0, 0, 0))
    out_specs = [state_spec]
    out_shape = [jax.ShapeDtypeStruct(s0.shape, F32)]
    if want_out:
        out_specs = [pl.BlockSpec((CHUNK, w), lambda b, c, d=d: (rows[d](b, c), 0))
                     for d in range(2)] + out_specs
        out_shape = [jax.ShapeDtypeStruct((m, w), F32)] * 2 + out_shape
    return pl.pallas_call(
        functools.partial(_rwscan_kernel, want_out=want_out),
        grid=(batch, n_chunks),
        in_specs=in_specs + [state_spec],
        out_specs=out_specs,
        out_shape=out_shape,
        scratch_shapes=[pltpu.VMEM((2, n_pairs, LANES, LANES), F32)],
        compiler_params=_params(("parallel", "arbitrary")),
        name="rwscan_out" if want_out else "rwscan_state",
    )(*args, s0)


def _rwread_kernel(of_ref, ob_ref, bonus_ref, g_ref, gng_ref, gnb_ref, y_ref):
    o = of_ref[...] + ob_ref[...]
    inv = 1.0 / RW_HEAD
    mean = _seg_sum(o, RW_HEAD) * inv
    cen = o - mean
    var = _seg_sum(cen * cen, RW_HEAD, full_precision=False) * inv
    on = cen * lax.rsqrt(var + RW_GN_EPS) * gng_ref[...] + gnb_ref[...]
    y_ref[...] = ((on + bonus_ref[...]) * g_ref[...]).astype(BF16)


def _rwread(o_f, o_b, bonus, g, gn_g, gn_b):
    m, w = o_f.shape
    tm = _tile(m, 512)
    row_spec = pl.BlockSpec((tm, w), lambda i: (i, 0))
    vec_spec = pl.BlockSpec((1, w), lambda i: (0, 0))
    return pl.pallas_call(
        _rwread_kernel,
        grid=(m // tm,),
        in_specs=[row_spec, row_spec, row_spec, row_spec, vec_spec, vec_spec],
        out_specs=row_spec,
        out_shape=jax.ShapeDtypeStruct((m, w), BF16),
        compiler_params=_params(("parallel",)),
        name="rwread",
    )(o_f, o_b, bonus, g, gn_g, gn_b)


def _gdprep_kernel(pc_ref, pp_ref, pn_ref, ab_ref, cw_ref, alog_ref, dtb_ref,
                   q_ref, k_ref, v_ref, gcum_ref, beta_ref, *, tiles_per_seq):
    i = pl.program_id(0)
    tm = pc_ref.shape[0]
    w = q_ref.shape[1]
    n_heads = w // GD_HEAD
    first = (i % tiles_per_seq) == 0
    last = (i % tiles_per_seq) == tiles_per_seq - 1
    x = pc_ref[:, 0:3 * w]
    prev_row = jnp.where(first, 0.0, pp_ref[7:8, 0:3 * w])
    next_row = jnp.where(last, 0.0, pn_ref[0:1, 0:3 * w])
    xm1, xp1 = _shift_rows(x, prev_row, next_row)
    y = xm1 * cw_ref[0:1, :] + x * cw_ref[1:2, :] + xp1 * cw_ref[2:3, :]
    y = y * _sigmoid(y)
    for h in range(n_heads):
        for part, ref, scale in ((0, q_ref, GD_HEAD ** -0.5), (1, k_ref, 1.0)):
            cols = slice(part * w + h * GD_HEAD, part * w + (h + 1) * GD_HEAD)
            t = y[:, cols]
            ss = jnp.sum(t * t, axis=-1, keepdims=True)
            ref[:, h * GD_HEAD:(h + 1) * GD_HEAD] = (
                t * (lax.rsqrt(jnp.maximum(ss, 1e-12)) * scale)).astype(BF16)
    v_ref[...] = y[:, 2 * w:3 * w].astype(BF16)

    ab = ab_ref[:, 6 * LANES:7 * LANES]
    a = ab[:, 0:2 * n_heads]
    b = ab[:, 2 * n_heads:4 * n_heads]
    glog = -jnp.exp(alog_ref[...]) * _softplus(a + dtb_ref[...])
    beta_ref[...] = _sigmoid(b)
    for c in range(tm // CHUNK):
        rows = slice(c * CHUNK, (c + 1) * CHUNK)
        gc = glog[rows]
        fwd = _mm_sel_left(_tri(False), gc)
        bwd = _mm_sel_left(_tri(True), gc)
        col = lax.broadcasted_iota(jnp.int32, gc.shape, 1)
        gcum_ref[rows, :] = jnp.where(col < n_heads, fwd, bwd)


def _gdprep(p, seq_len, conv_w, a_log, dt_bias, qkvz_block, ab_block, width):
    m = p.shape[0]
    blk = 4 * width
    tm = _tile(seq_len, 256, CHUNK)
    tps = seq_len // tm
    nb8 = m // 8
    n2h = a_log.shape[1]
    full = lambda a: pl.BlockSpec(a.shape, lambda i: (0,) * a.ndim)
    row_spec = pl.BlockSpec((tm, width), lambda i: (i, 0))
    small_spec = pl.BlockSpec((tm, n2h), lambda i: (i, 0))
    return pl.pallas_call(
        functools.partial(_gdprep_kernel, tiles_per_seq=tps),
        grid=(m // tm,),
        in_specs=[pl.BlockSpec((tm, blk), lambda i: (i, qkvz_block)),
                  pl.BlockSpec((8, blk), lambda i: (jnp.maximum(i * (tm // 8) - 1, 0), qkvz_block)),
                  pl.BlockSpec((8, blk),
                               lambda i: (jnp.minimum((i + 1) * (tm // 8), nb8 - 1), qkvz_block)),
                  pl.BlockSpec((tm, 8 * LANES), lambda i: (i, ab_block)),
                  full(conv_w), full(a_log), full(dt_bias)],
        out_specs=[row_spec, row_spec, row_spec, small_spec, small_spec],
        out_shape=[jax.ShapeDtypeStruct((m, width), BF16)] * 3
        + [jax.ShapeDtypeStruct((m, n2h), F32)] * 2,
        compiler_params=_params(("parallel",)),
        name="gdprep",
    )(p, p, p, p, conv_w, a_log, dt_bias)


def _gdscan_kernel(*refs, want_out):
    ins, rest = refs[:12], refs[12:]
    s0_ref = rest[0]
    if want_out:
        o_refs, s_ref, st_ref = rest[1:3], rest[3], rest[4]
    else:
        s_ref, st_ref = rest[1], rest[2]
    c = pl.program_id(1)
    n_pairs = st_ref.shape[1] // 2

    @pl.when(c == 0)
    def _():
        st_ref[...] = s0_ref[0]

    strict, incl, eye, blk, m0 = _pair_masks(CHUNK)
    bd = lambda x: _pair_blockdiag(x, m0)
    chains = [(d, p) for d in range(2) for p in range(n_pairs)]
    n = len(chains)
    hcols = lambda h: slice(h * GD_HEAD, (h + 1) * GD_HEAD)

    def head_vals(d, p, e):
        h = 2 * p + e
        q_ref, k_ref, v_ref, g_ref, _, beta_ref = ins[6 * d:6 * d + 6]
        return (q_ref[:, hcols(h)].astype(F32), k_ref[:, hcols(h)].astype(F32),
                v_ref[:, hcols(h)].astype(F32), g_ref[0, :, h:h + 1], beta_ref[0, :, h:h + 1])

    hv = [[head_vals(d, p, e) for e in range(2)] for d, p in chains]
    kb = [[hv[i][e][1] * hv[i][e][4] for e in range(2)] for i in range(n)]
    eg = [[jnp.exp(hv[i][e][3]) for e in range(2)] for i in range(n)]
    decay, s2 = [], []
    for i, (d, p) in enumerate(chains):
        gcol = jnp.where(m0, hv[i][0][3], hv[i][1][3])
        grow = ins[6 * d + 4][0, 0, p:p + 1, :]
        decay.append(jnp.where(incl[d], jnp.exp(jnp.where(incl[d], gcol - grow, 0.0)), 0.0))
        k0, k1 = hv[i][0][1], hv[i][1][1]
        zero = jnp.zeros_like(k0)
        lhs = jnp.concatenate([jnp.concatenate([kb[i][0], kb[i][1]], axis=1),
                               jnp.concatenate([hv[i][0][0], hv[i][1][0]], axis=1)], axis=0)
        rhs = jnp.concatenate([jnp.concatenate([k0, zero], axis=1),
                               jnp.concatenate([zero, k1], axis=1)], axis=0)
        s2.append(_mm_nt(lhs, rhs))
    a = [jnp.where(strict[chains[i][0]], s2[i][:CHUNK] * decay[i], 0.0) for i in range(n)]
    t = _tri_inverse(a, eye, blk, m0, -1.0)
    sol = [_mm(bd(t[i]), jnp.concatenate(
        [jnp.concatenate([hv[i][e][2] * hv[i][e][4], kb[i][e] * eg[i][e]], axis=1)
         for e in range(2)], axis=0)) for i in range(n)]

    st = [[st_ref[d, 2 * p + e] for e in range(2)] for d, p in chains]
    ws = [[_mm(jnp.concatenate([sol[i][e * CHUNK:(e + 1) * CHUNK, GD_HEAD:],
                                hv[i][e][0] * eg[i][e]], axis=0), st[i][e])
           for e in range(2)] for i in range(n)]
    v_new = [[sol[i][e * CHUNK:(e + 1) * CHUNK, :GD_HEAD] - ws[i][e][:CHUNK] for e in range(2)]
             for i in range(n)]
    if want_out:
        for i, (d, p) in enumerate(chains):
            intra = _mm(bd(s2[i][CHUNK:] * decay[i]),
                        jnp.concatenate([v_new[i][0], v_new[i][1]], axis=0))
            for e in range(2):
                o_refs[d][:, hcols(2 * p + e)] = (ws[i][e][CHUNK:]
                                                  + intra[e * CHUNK:(e + 1) * CHUNK])
    for i, (d, p) in enumerate(chains):
        for e in range(2):
            gcol = hv[i][e][3]
            g_last = jnp.min(gcol, axis=0, keepdims=True)
            k_dec = hv[i][e][1] * jnp.exp(g_last - gcol)
            st_ref[d, 2 * p + e] = st[i][e] * jnp.exp(g_last) + _mm_tn(k_dec, v_new[i][e])

    @pl.when(c == pl.num_programs(1) - 1)
    def _():
        s_ref[0] = st_ref[...]


def _gdscan(q, k, v, g, gt, beta, s0, batch, want_out):
    m, w = q.shape
    n_chunks = m // batch // CHUNK
    n_heads = w // GD_HEAD
    rows = (lambda b, c: b * n_chunks + c, lambda b, c: b * n_chunks + n_chunks - 1 - c)

    in_specs, args = [], []
    for d in range(2):
        row_spec = pl.BlockSpec((CHUNK, w), lambda b, c, d=d: (rows[d](b, c), 0))
        col_spec = pl.BlockSpec((1, CHUNK, n_heads), lambda b, c, d=d: (d, rows[d](b, c), 0))
        in_specs += [row_spec, row_spec, row_spec, col_spec,
                     pl.BlockSpec((1, 1, n_heads // 2, LANES),
                                  lambda b, c, d=d: (d, rows[d](b, c), 0, 0)),
                     col_spec]
        args += [q, k, v, g, gt, beta]
    state_spec = pl.BlockSpec((1, 2, n_heads, GD_HEAD, GD_HEAD), lambda b, c: (b, 0, 0, 0, 0))
    out_specs = [state_spec]
    out_shape = [jax.ShapeDtypeStruct(s0.shape, F32)]
    if want_out:
        out_specs = [pl.BlockSpec((CHUNK, w), lambda b, c, d=d: (rows[d](b, c), 0))
                     for d in range(2)] + out_specs
        out_shape = [jax.ShapeDtypeStruct((m, w), F32)] * 2 + out_shape
    return pl.pallas_call(
        functools.partial(_gdscan_kernel, want_out=want_out),
        grid=(batch, n_chunks),
        in_specs=in_specs + [state_spec],
        out_specs=out_specs,
        out_shape=out_shape,
        scratch_shapes=[pltpu.VMEM((2, n_heads, GD_HEAD, GD_HEAD), F32)],
        compiler_params=_params(("parallel", "arbitrary")),
        name="gdscan_out" if want_out else "gdscan_state",
    )(*args, s0)


def _gdread_kernel(of_ref, ob_ref, z_ref, ng_ref, y_ref):
    o = of_ref[...] + ob_ref[...]
    z = z_ref[...]
    gate = z * _sigmoid(z)
    for h in range(o.shape[1] // GD_HEAD):
        cols = slice(h * GD_HEAD, (h + 1) * GD_HEAD)
        oh = o[:, cols]
        ms = jnp.mean(oh * oh, axis=-1, keepdims=True)
        y_ref[:, cols] = (oh * lax.rsqrt(ms + NORM_EPS) * ng_ref[...] * gate[:, cols]).astype(BF16)


def _gdread(o_f, o_b, p, z_block, norm_g):
    m, w = o_f.shape
    tm = _tile(m, 512)
    return pl.pallas_call(
        _gdread_kernel,
        grid=(m // tm,),
        in_specs=[pl.BlockSpec((tm, w), lambda i: (i, 0)),
                  pl.BlockSpec((tm, w), lambda i: (i, 0)),
                  pl.BlockSpec((tm, w), lambda i: (i, z_block)),
                  pl.BlockSpec((1, GD_HEAD), lambda i: (0, 0))],
        out_specs=pl.BlockSpec((tm, w), lambda i: (i, 0)),
        out_shape=jax.ShapeDtypeStruct((m, w), BF16),
        compiler_params=_params(("parallel",)),
        name="gdread",
    )(o_f, o_b, p, norm_g)


def _merge1_kernel(ya_ref, yb_ref, wa_ref, wb_ref, ga_ref, gb_ref, o_ref):
    a = jnp.dot(ya_ref[...], wa_ref[...], preferred_element_type=F32)
    b = jnp.dot(yb_ref[...], wb_ref[...], preferred_element_type=F32)
    o_ref[...] = (_sigmoid(ga_ref[...]) * a + _sigmoid(gb_ref[...]) * b).astype(BF16)


def _merge1(ya, yb, wa, wb, p, gate_col0):
    m, ka = ya.shape
    kb = yb.shape[1]
    d = wa.shape[1]
    tm = _tile(m, 512)
    tn = d
    ga0 = gate_col0 // tn
    gb0 = (gate_col0 + d) // tn
    return pl.pallas_call(
        _merge1_kernel,
        grid=(m // tm, d // tn),
        in_specs=[pl.BlockSpec((tm, ka), lambda i, j: (i, 0)),
                  pl.BlockSpec((tm, kb), lambda i, j: (i, 0)),
                  pl.BlockSpec((ka, tn), lambda i, j: (0, j)),
                  pl.BlockSpec((kb, tn), lambda i, j: (0, j)),
                  pl.BlockSpec((tm, tn), lambda i, j: (i, ga0 + j)),
                  pl.BlockSpec((tm, tn), lambda i, j: (i, gb0 + j))],
        out_specs=pl.BlockSpec((tm, tn), lambda i, j: (i, j)),
        out_shape=jax.ShapeDtypeStruct((m, d), BF16),
        compiler_params=_params(("parallel", "arbitrary")),
        name="merge1",
    )(ya, yb, wa, wb, p, p)


def _merge2_kernel(mg_ref, wo_ref, x_ref, mod_ref, o_ref, *, gate_row):
    y = jnp.dot(mg_ref[...], wo_ref[...], preferred_element_type=F32)
    o_ref[...] = x_ref[...] + mod_ref[0, gate_row:gate_row + 1, :] * y


def _merge2(merged, wo, x2, mod, rows_per_mod, gate_row):
    m, d = x2.shape
    tm = _tile(rows_per_mod, 512)
    tn = d
    per = rows_per_mod // tm
    return pl.pallas_call(
        functools.partial(_merge2_kernel, gate_row=gate_row),
        grid=(m // tm, d // tn),
        in_specs=[pl.BlockSpec((tm, d), lambda i, j: (i, 0)),
                  pl.BlockSpec((d, tn), lambda i, j: (0, j)),
                  pl.BlockSpec((tm, tn), lambda i, j: (i, j)),
                  pl.BlockSpec((1, 8, tn), lambda i, j: (i // per, 0, j))],
        out_specs=pl.BlockSpec((tm, tn), lambda i, j: (i, j)),
        out_shape=jax.ShapeDtypeStruct((m, d), F32),
        compiler_params=_params(("parallel", "arbitrary")),
        name="merge2",
    )(merged, wo, x2, mod)


def _conv3x3_gelu(g, cw_ref):
    n = g.shape[0]
    col = lax.broadcasted_iota(jnp.int32, g.shape, 0) % GRID_W
    left = jnp.where(col > 0, pltpu.roll(g, 1, 0), 0.0)
    right = jnp.where(col < GRID_W - 1, pltpu.roll(g, n - 1, 0), 0.0)
    lines = [left * cw_ref[3 * kh:3 * kh + 1, :] + g * cw_ref[3 * kh + 1:3 * kh + 2, :]
             + right * cw_ref[3 * kh + 2:3 * kh + 3, :] for kh in range(3)]
    pad = jnp.zeros((GRID_W, g.shape[1]), F32)
    acc = (lines[1] + jnp.concatenate([pad, lines[0][:n - GRID_W]], axis=0)
           + jnp.concatenate([lines[2][GRID_W:], pad], axis=0))
    return 0.5 * acc * (1.0 + lax.erf(acc * (2.0 ** -0.5)))


def _ffn_act_kernel(x_ref, mod_ref, g_ref, w1g_ref, w1v_ref, cw_ref, o_ref, h_ref, gate_ref, val_ref,
                    *, sh_row, sc_row, prologue_rows, dot_rows):
    j = pl.program_id(1)

    @pl.when(j == 0)
    def _():
        sh = mod_ref[0, sh_row:sh_row + 1, :]
        sc = mod_ref[0, sc_row:sc_row + 1, :]

        def body(r, carry):
            rows = pl.ds(pl.multiple_of(r * prologue_rows, prologue_rows), prologue_rows)
            x = x_ref[rows, :]
            ms = jnp.mean(x * x, axis=-1, keepdims=True)
            y = x * lax.rsqrt(ms + NORM_EPS) * g_ref[...]
            h_ref[rows, :] = (y * (1.0 + sc) + sh).astype(BF16)
            return carry

        lax.fori_loop(0, x_ref.shape[0] // prologue_rows, body, 0)
        gate_ref[1] = jnp.zeros(gate_ref.shape[1:], F32)
        val_ref[1] = jnp.zeros(val_ref.shape[1:], F32)

    slot = j % 2
    o_ref[...] = (_conv3x3_gelu(gate_ref[1 - slot], cw_ref) * val_ref[1 - slot]).astype(BF16)
    w1g = w1g_ref[...].astype(BF16)
    w1v = w1v_ref[...].astype(BF16)
    for r0 in range(0, h_ref.shape[0], dot_rows):
        rows = slice(r0, r0 + dot_rows)
        hh = h_ref[rows, :]
        gate_ref[slot, rows, :] = jnp.dot(hh, w1g, preferred_element_type=F32)
        val_ref[slot, rows, :] = jnp.dot(hh, w1v, preferred_element_type=F32)


def _ffn_act(x1, mod, gain, w1, conv_w, batch, sh_row, sc_row):
    m, d = x1.shape
    dff = conv_w.shape[1]
    seq = m // batch
    tf = _tile(dff, 256, LANES)
    nblk = dff // tf
    cur = lambda j: jnp.minimum(j, nblk - 1)
    prev = lambda j: jnp.maximum(j - 1, 0)
    return pl.pallas_call(
        functools.partial(_ffn_act_kernel, sh_row=sh_row, sc_row=sc_row,
                          prologue_rows=_tile(seq, 256), dot_rows=_tile(seq, 2048)),
        grid=(batch, nblk + 1),
        in_specs=[pl.BlockSpec((seq, d), lambda b, j: (b, 0), pipeline_mode=pl.Buffered(1)),
                  pl.BlockSpec((1, 8, d), lambda b, j: (b, 0, 0)),
                  pl.BlockSpec((1, d), lambda b, j: (0, 0)),
                  pl.BlockSpec((d, tf), lambda b, j: (0, cur(j))),
                  pl.BlockSpec((d, tf), lambda b, j: (0, nblk + cur(j))),
                  pl.BlockSpec((conv_w.shape[0], tf), lambda b, j: (0, prev(j)))],
        out_specs=pl.BlockSpec((seq, tf), lambda b, j: (b, prev(j))),
        out_shape=jax.ShapeDtypeStruct((m, dff), BF16),
        scratch_shapes=[pltpu.VMEM((seq, d), BF16),
                        pltpu.VMEM((2, seq, tf), F32),
                        pltpu.VMEM((2, seq, tf), F32)],
        compiler_params=_params(("parallel", "arbitrary")),
        name="ffn_act",
    )(x1, mod, gain, w1, w1, conv_w)


def _ffn_out_kernel(act_ref, w2_ref, x_ref, mod_ref, g_ref, o_ref, acc_ref, *, gate_row):
    kstep = pl.program_id(1)

    @pl.when(kstep == 0)
    def _():
        acc_ref[...] = jnp.zeros_like(acc_ref)

    acc_ref[...] += jnp.dot(act_ref[...], w2_ref[...], preferred_element_type=F32)

    @pl.when(kstep == pl.num_programs(1) - 1)
    def _():
        y = x_ref[...] + mod_ref[0, gate_row:gate_row + 1, :] * acc_ref[...]
        ms = jnp.mean(y * y, axis=-1, keepdims=True)
        o_ref[...] = y * lax.rsqrt(ms + NORM_EPS) * g_ref[...]


def _ffn_out(act, w2, x1, mod, final_g, rows_per_mod, gate_row):
    m, d = x1.shape
    dff = act.shape[1]
    tm = _tile(rows_per_mod, 512)
    tk = _tile(dff, 1408, LANES)
    per = rows_per_mod // tm
    return pl.pallas_call(
        functools.partial(_ffn_out_kernel, gate_row=gate_row),
        grid=(m // tm, dff // tk),
        in_specs=[pl.BlockSpec((tm, tk), lambda i, k: (i, k)),
                  pl.BlockSpec((tk, d), lambda i, k: (k, 0)),
                  pl.BlockSpec((tm, d), lambda i, k: (i, 0)),
                  pl.BlockSpec((1, 8, d), lambda i, k: (i // per, 0, 0)),
                  pl.BlockSpec((1, d), lambda i, k: (0, 0))],
        out_specs=pl.BlockSpec((tm, d), lambda i, k: (i, 0)),
        out_shape=jax.ShapeDtypeStruct((m, d), F32),
        scratch_shapes=[pltpu.VMEM((tm, d), F32)],
        compiler_params=_params(("parallel", "arbitrary")),
        name="ffn_out",
    )(act, w2, x1, mod, final_g)


def _pad_cols(a, width):
    return jnp.pad(a, [(0, 0)] * (a.ndim - 1) + [(0, width - a.shape[-1])])


def _pad_rank(a):
    return jnp.pad(a, [(0, 0)] * (a.ndim - 2) + [(0, RANK_PAD - a.shape[-2]), (0, 0)])


def kernel(x, c, ctx, c_ctx, w_ada, b_ada, norm1_g, norm2_g, w_in, rw_mu, rw_k_k, rw_k_a, rw_r_k, rw_w0, rw_w_up, rw_a0, rw_a_up, rw_g_up, rw_gn_g, rw_gn_b, gd_conv_w, gd_a_log, gd_dt_bias, gd_norm_g, w_a_out, w_b_out, w_o, ffn_w1, ffn_conv_w, ffn_w2, final_norm_g):
    batch, seq, d = x.shape
    ctx_len = ctx.shape[1]
    assert w_ada.shape[0] == 1, "single layer only"
    rw_w = rw_k_k.shape[1]
    gd_w = w_b_out.shape[1]
    dec_rank = rw_w_up.shape[2]
    icl_rank = rw_a_up.shape[2]
    gate_rank = rw_g_up.shape[1]
    gd_heads = gd_a_log.shape[2]
    assert max(dec_rank, icl_rank) <= RANK_PAD and 4 * gd_heads <= 2 * LANES
    assert seq % CHUNK == 0 and ctx_len % CHUNK == 0 and seq % GRID_W == 0
    assert 3 * rw_w == 3 * gd_w and gate_rank <= 2 * LANES

    low_w = 8 * LANES
    blk0 = 3 * rw_w + low_w
    assert blk0 == 4 * gd_w
    wi = w_in[0]
    o_rw = 3 * rw_w
    o_gd = o_rw + 2 * dec_rank + 2 * icl_rank + gate_rank
    o_ab = o_gd + 4 * gd_w
    o_gate = o_ab + 4 * gd_heads

    def pack_cols(a):
        pieces = [a[..., :o_rw]]
        off = o_rw
        for r in (dec_rank, dec_rank, icl_rank, icl_rank):
            pieces.append(_pad_cols(a[..., off:off + r], RANK_PAD))
            off += r
        pieces.append(_pad_cols(a[..., off:off + gate_rank], 2 * LANES))
        return pieces

    w_pack = jnp.concatenate(
        pack_cols(wi) + [_pad_cols(wi[:, o_ab:o_gate], 2 * LANES), wi[:, o_gd:o_ab], wi[:, o_gate:]],
        axis=1).astype(BF16)
    n_ctx_cols = 2 * blk0
    gate_col0 = 2 * blk0
    mu_pack = jnp.concatenate(pack_cols(rw_mu) + [jnp.zeros((1, 2 * LANES), F32)], axis=1)

    cc = jnp.concatenate([c, c_ctx[None, :], jnp.zeros((16 - batch - 1, d), F32)], axis=0)
    mods = _mod(cc, w_ada[0], b_ada)
    mod_lat = _pad_rows8(mods[:batch].reshape(batch, 6, d))
    mod_ctx = _pad_rows8(mods[batch:batch + 1].reshape(1, 6, d))

    x2 = x.reshape(batch * seq, d)
    ctx2 = ctx.reshape(batch * ctx_len, d)
    p_lat = _normproj(x2, mod_lat, norm1_g, w_pack, seq, w_pack.shape[1], 0, 1, "inproj_lat")
    p_ctx = _normproj(ctx2, mod_ctx, norm1_g, w_pack, batch * ctx_len, n_ctx_cols, 0, 1, "inproj_ctx")

    rw_wts = (mu_pack, rw_k_k, rw_k_a, rw_r_k, rw_w0[0], _pad_rank(rw_w_up[0]).astype(BF16),
              rw_a0[0], _pad_rank(rw_a_up[0]).astype(BF16),
              jnp.pad(rw_g_up[0], ((0, 2 * LANES - gate_rank), (0, 0))).astype(BF16))
    n_pairs = rw_w // LANES
    s0 = jnp.zeros((batch, 2, n_pairs, LANES, LANES), F32)
    f_ctx = _rwprep(p_ctx, ctx_len, rw_wts)
    f_lat = _rwprep(p_lat, seq, rw_wts)
    (s_ctx,) = _rwscan(*f_ctx[:6], s0, batch, want_out=False)
    o_rw_f, o_rw_b, _ = _rwscan(*f_lat[:6], s_ctx, batch, want_out=True)
    ya = _rwread(o_rw_f, o_rw_b, f_lat[6], f_lat[7], rw_gn_g, rw_gn_b)

    s0g = jnp.zeros((batch, 2, gd_heads, GD_HEAD, GD_HEAD), F32)
    a_log2 = gd_a_log[0].reshape(1, 2 * gd_heads)
    dtb2 = gd_dt_bias[0].reshape(1, 2 * gd_heads)

    def gd_feats(p, seq_len):
        q, k, v, gcum, beta = _gdprep(p, seq_len, gd_conv_w[0], a_log2, dtb2, 1, 3, gd_w)
        m = p.shape[0]
        g3 = gcum.reshape(m, 2, gd_heads).transpose(1, 0, 2)
        gt = g3.reshape(2, m // CHUNK, CHUNK, gd_heads).transpose(0, 1, 3, 2)
        gt = gt.reshape(2, m // CHUNK, gd_heads // 2, 2 * CHUNK)
        b3 = beta.reshape(m, 2, gd_heads).transpose(1, 0, 2)
        return q, k, v, g3, gt, b3

    (sg_ctx,) = _gdscan(*gd_feats(p_ctx, ctx_len), s0g, batch, want_out=False)
    o_gd_f, o_gd_b, _ = _gdscan(*gd_feats(p_lat, seq), sg_ctx, batch, want_out=True)
    yb = _gdread(o_gd_f, o_gd_b, p_lat, 2 * blk0 // gd_w - 1, gd_norm_g)

    merged = _merge1(ya, yb, w_a_out[0].astype(BF16), w_b_out[0].astype(BF16), p_lat, gate_col0)
    x1 = _merge2(merged, w_o[0].astype(BF16), x2, mod_lat, seq, 2)

    act = _ffn_act(x1, mod_lat, norm2_g, ffn_w1[0],
                   ffn_conv_w[0].reshape(-1, ffn_conv_w.shape[-1]), batch, 3, 4)
    out = _ffn_out(act, ffn_w2[0].astype(BF16), x1, mod_lat, final_norm_g[None, :], seq, 5)
    return out.reshape(batch, seq, d)


def _pad_rows8(a):
    return jnp.pad(a, ((0, 0), (0, 8 - a.shape[1]), (0, 0)))
```

```python
import functools
import math

import jax
import jax.numpy as jnp
from jax import lax
from jax.experimental import pallas as pl
from jax.experimental.pallas import tpu as pltpu

F32 = jnp.float32
BF16 = jnp.bfloat16
HIGHEST = lax.Precision.HIGHEST

NORM_EPS = 1e-6
RW_GN_EPS = 64e-5
RW_HEAD = 64
GD_HEAD = 128
LANES = 128
CHUNK = 64
GRID_W = 64
RANK_PAD = 128
VMEM_LIMIT = 56 * 1024 * 1024


def _params(sem):
    return pltpu.CompilerParams(dimension_semantics=sem, vmem_limit_bytes=VMEM_LIMIT)


def _tile(n, pref, mult=8):
    if n <= pref:
        return n
    t = (pref // mult) * mult
    while t >= mult:
        if n % t == 0:
            return t
        t -= mult
    return n


def _mm(a, b):
    return jnp.dot(a.astype(BF16), b.astype(BF16), preferred_element_type=F32)


def _mm_nt(a, b):
    return lax.dot_general(a.astype(BF16), b.astype(BF16), (((1,), (1,)), ((), ())),
                           preferred_element_type=F32)


def _mm_tn(a, b):
    return lax.dot_general(a.astype(BF16), b.astype(BF16), (((0,), (0,)), ((), ())),
                           preferred_element_type=F32)


def _mm_hi(a, b):
    return jnp.dot(a, b, precision=HIGHEST, preferred_element_type=F32)


def _split3(x):
    x1 = x.astype(BF16)
    r1 = x - x1.astype(F32)
    x2 = r1.astype(BF16)
    x3 = (r1 - x2.astype(F32)).astype(BF16)
    return x1, x2, x3


def _mm_sel_left(c, x):
    cb = c.astype(BF16)
    return jnp.dot(jnp.concatenate([cb, cb, cb], axis=1), jnp.concatenate(_split3(x), axis=0),
                   preferred_element_type=F32)


def _mm_sel_right(x, c):
    cb = c.astype(BF16)
    return jnp.dot(jnp.concatenate(_split3(x), axis=1), jnp.concatenate([cb, cb, cb], axis=0),
                   preferred_element_type=F32)


def _softplus(x):
    return jnp.maximum(x, 0.0) + jnp.log(1.0 + jnp.exp(-jnp.abs(x)))


def _sigmoid(x):
    return jax.nn.sigmoid(x)


def _seg_ones(width):
    i = lax.broadcasted_iota(jnp.int32, (LANES, LANES), 0) // width
    j = lax.broadcasted_iota(jnp.int32, (LANES, LANES), 1) // width
    return (i == j).astype(F32)


def _seg_sum(x, width, full_precision=True):
    e = _seg_ones(width)
    n = x.shape[-1] // LANES
    mm = _mm_sel_right if full_precision else _mm
    parts = [mm(x[:, g * LANES:(g + 1) * LANES], e) for g in range(n)]
    return parts[0] if n == 1 else jnp.concatenate(parts, axis=-1)


def _tri(rev):
    i = lax.broadcasted_iota(jnp.int32, (CHUNK, CHUNK), 0)
    j = lax.broadcasted_iota(jnp.int32, (CHUNK, CHUNK), 1)
    return ((j >= i) if rev else (j <= i)).astype(F32)


def _level_masks(i, j):
    masks = [(i // 2) == (j // 2)]
    s = 2
    while s < CHUNK:
        masks.append(((i // (2 * s)) == (j // (2 * s))) & ((i // s) != (j // s)))
        s *= 2
    return masks


def _shift_rows(x, prev_row, next_row):
    n = x.shape[0]
    row = lax.broadcasted_iota(jnp.int32, x.shape, 0)
    xm1 = jnp.where(row == 0, prev_row, pltpu.roll(x, 1, 0))
    xp1 = jnp.where(row == n - 1, next_row, pltpu.roll(x, n - 1, 0))
    return xm1, xp1


def _store_stencil(dst_ref, x, prev_row, next_row, f):
    n = x.shape[0]
    dst_ref[...] = f(pltpu.roll(x, 1, 0), x, pltpu.roll(x, n - 1, 0))
    dst_ref[0:1, :] = f(prev_row, x[0:1], x[1:2])
    dst_ref[n - 1:n, :] = f(x[n - 2:n - 1], x[n - 1:n], next_row)


def _mod_kernel(c_ref, w_ref, b_ref, o_ref):
    c = c_ref[...]
    s = c * _sigmoid(c)
    o_ref[...] = _mm_hi(s, w_ref[...]) + b_ref[...]


def _mod(cc, w_ada, b_ada):
    rows, d = cc.shape
    n = w_ada.shape[1]
    tn = _tile(n, 1024, LANES)
    return pl.pallas_call(
        _mod_kernel,
        grid=(n // tn,),
        in_specs=[pl.BlockSpec((rows, d), lambda j: (0, 0)),
                  pl.BlockSpec((d, tn), lambda j: (0, j)),
                  pl.BlockSpec((1, tn), lambda j: (0, j))],
        out_specs=pl.BlockSpec((rows, tn), lambda j: (0, j)),
        out_shape=jax.ShapeDtypeStruct((rows, n), F32),
        compiler_params=_params(("arbitrary",)),
        name="mod",
    )(cc, w_ada, b_ada)


def _normproj_kernel(x_ref, mod_ref, g_ref, w_ref, o_ref, h_ref, *, sh_row, sc_row):
    @pl.when(pl.program_id(1) == 0)
    def _():
        x = x_ref[...]
        ms = jnp.mean(x * x, axis=-1, keepdims=True)
        y = x * lax.rsqrt(ms + NORM_EPS) * g_ref[...]
        sh = mod_ref[0, sh_row:sh_row + 1, :]
        sc = mod_ref[0, sc_row:sc_row + 1, :]
        h_ref[...] = (y * (1.0 + sc) + sh).astype(BF16)

    o_ref[...] = jnp.dot(h_ref[...], w_ref[...], preferred_element_type=F32)


def _normproj(x2, mod, gain, w, rows_per_mod, n_cols, sh_row, sc_row, name):
    m, d = x2.shape
    tm = _tile(rows_per_mod, 1024)
    tn = _tile(n_cols, 1024, LANES)
    per = rows_per_mod // tm
    return pl.pallas_call(
        functools.partial(_normproj_kernel, sh_row=sh_row, sc_row=sc_row),
        grid=(m // tm, n_cols // tn),
        in_specs=[pl.BlockSpec((tm, d), lambda i, j: (i, 0)),
                  pl.BlockSpec((1, 8, d), lambda i, j: (i // per, 0, 0)),
                  pl.BlockSpec((1, d), lambda i, j: (0, 0)),
                  pl.BlockSpec((d, tn), lambda i, j: (0, j))],
        out_specs=pl.BlockSpec((tm, tn), lambda i, j: (i, j)),
        out_shape=jax.ShapeDtypeStruct((m, n_cols), F32),
        scratch_shapes=[pltpu.VMEM((tm, d), BF16)],
        compiler_params=_params(("parallel", "arbitrary")),
        name=name,
    )(x2, mod, gain, w)


def _rwprep_kernel(pc_ref, pp_ref, pn_ref, mu_ref, kk_ref, ka_ref, rk_ref, w0_ref, wup_ref,
                   a0_ref, aup_ref, gup_ref,
                   at_ref, bt_ref, kt_ref, rt_ref, v_ref, pt_ref, bonus_ref, g_ref, xs_ref,
                   *, tiles_per_seq):
    i = pl.program_id(0)
    tm = pc_ref.shape[0]
    w = RW_HEAD * (kk_ref.shape[1] // RW_HEAD)
    first = (i % tiles_per_seq) == 0
    last = (i % tiles_per_seq) == tiles_per_seq - 1
    base = 3 * w
    used = base + 4 * RANK_PAD + gup_ref.shape[0]
    x = pc_ref[:, 0:used]
    prev_row = jnp.where(first, 0.0, pp_ref[7:8, 0:used])
    next_row = jnp.where(last, 0.0, pn_ref[0:1, 0:used])
    mu = mu_ref[:, 0:used]
    keep = 1.0 - mu
    half = 0.5 * mu
    _store_stencil(xs_ref, x, prev_row, next_row, lambda a, c, b: c * keep + (a + b) * half)

    r = xs_ref[:, 0:w]
    k = xs_ref[:, w:2 * w]
    v = xs_ref[:, 2 * w:3 * w]
    gd = xs_ref[:, base + 4 * RANK_PAD:used]
    g_ref[...] = _mm(_sigmoid(gd), gup_ref[...]).astype(BF16)
    v_ref[...] = v.astype(BF16)

    kx = k * kk_ref[...]
    kk = kx * lax.rsqrt(jnp.maximum(_seg_sum(kx * kx, RW_HEAD, full_precision=False), 1e-12))

    ksum = jnp.zeros_like(k)
    for d in range(2):
        wd = xs_ref[:, base + d * RANK_PAD: base + (d + 1) * RANK_PAD]
        ad = xs_ref[:, base + (2 + d) * RANK_PAD: base + (3 + d) * RANK_PAD]
        wl = w0_ref[d:d + 1, :] + _mm(jnp.tanh(wd), wup_ref[d])
        lw = -math.exp(-0.5) * _sigmoid(wl)
        a = _sigmoid(a0_ref[d:d + 1, :] + _mm(ad, aup_ref[d]))
        kd = k * (1.0 + (a - 1.0) * ka_ref[...])
        ksum = ksum + kd
        tri = _tri(rev=(d == 1))
        for c in range(tm // CHUNK):
            rows = slice(c * CHUNK, (c + 1) * CHUNK)
            lwc = lw[rows]
            cum = _mm_sel_left(tri, lwc)
            p_in = jnp.exp(cum)
            p_inv = jnp.exp(-cum)
            p_ex = jnp.exp(cum - lwc)
            at_ref[d, rows, :] = (-kk[rows] * p_ex).astype(BF16)
            bt_ref[d, rows, :] = (kk[rows] * a[rows] * p_inv).astype(BF16)
            kt_ref[d, rows, :] = (kd[rows] * p_inv).astype(BF16)
            rt_ref[d, rows, :] = (r[rows] * p_in).astype(BF16)
            tot = cum[CHUNK - 1:CHUNK] if d == 0 else cum[0:1]
            pt_ref[d, c, :, :] = jnp.exp(tot)
    bonus_ref[...] = (_seg_sum(r * ksum * rk_ref[...], RW_HEAD, full_precision=False) * v).astype(BF16)


def _rwprep(p, seq_len, wts):
    m = p.shape[0]
    mu, k_k, k_a, r_k, w0, w_up, a0, a_up, g_up = wts
    w = k_k.shape[1]
    blk = 3 * w + 4 * RANK_PAD + g_up.shape[0]
    tm = _tile(seq_len, 256, CHUNK)
    tps = seq_len // tm
    nb8 = m // 8
    full = lambda a: pl.BlockSpec(a.shape, lambda i: (0,) * a.ndim)
    feat = jax.ShapeDtypeStruct((2, m, w), BF16)
    feat_spec = pl.BlockSpec((2, tm, w), lambda i: (0, i, 0))
    row_spec = pl.BlockSpec((tm, w), lambda i: (i, 0))
    return pl.pallas_call(
        functools.partial(_rwprep_kernel, tiles_per_seq=tps),
        grid=(m // tm,),
        in_specs=[pl.BlockSpec((tm, blk), lambda i: (i, 0)),
                  pl.BlockSpec((8, blk), lambda i: (jnp.maximum(i * (tm // 8) - 1, 0), 0)),
                  pl.BlockSpec((8, blk), lambda i: (jnp.minimum((i + 1) * (tm // 8), nb8 - 1), 0)),
                  full(mu), full(k_k), full(k_a), full(r_k), full(w0), full(w_up), full(a0),
                  full(a_up), full(g_up)],
        out_specs=[feat_spec, feat_spec, feat_spec, feat_spec, row_spec,
                   pl.BlockSpec((2, tm // CHUNK, 1, w), lambda i: (0, i, 0, 0)),
                   row_spec, row_spec],
        out_shape=[feat, feat, feat, feat, jax.ShapeDtypeStruct((m, w), BF16),
                   jax.ShapeDtypeStruct((2, m // CHUNK, 1, w), F32),
                   jax.ShapeDtypeStruct((m, w), BF16), jax.ShapeDtypeStruct((m, w), BF16)],
        scratch_shapes=[pltpu.VMEM((tm, blk), F32)],
        compiler_params=_params(("parallel",)),
        name="rwprep",
    )(p, p, p, mu, k_k, k_a, r_k, w0, w_up, a0, a_up, g_up)


def _pair_blockdiag(x, m0):
    zero = jnp.zeros_like(x)
    return jnp.concatenate([jnp.where(m0, x, zero), jnp.where(m0, zero, x)], axis=0)


def _tri_inverse(a, eye, blk, m0, sign):
    ts = [eye + sign * jnp.where(blk[0], x, 0.0) for x in a]
    for lvl in range(1, len(blk)):
        xs = [_mm(t, _pair_blockdiag(jnp.where(blk[lvl], x, 0.0), m0)) for t, x in zip(ts, a)]
        ts = [t + sign * _mm(x, _pair_blockdiag(t, m0)) for t, x in zip(ts, xs)]
    return ts


def _pair_masks(head_cols):
    i = lax.broadcasted_iota(jnp.int32, (CHUNK, LANES), 0)
    lane = lax.broadcasted_iota(jnp.int32, (CHUNK, LANES), 1)
    j = lane % head_cols
    strict = (j < i, j > i)
    incl = (j <= i, j >= i)
    eye = (j == i).astype(F32)
    return strict, incl, eye, _level_masks(i, j), lane < head_cols


def _rwscan_kernel(*refs, want_out):
    ins, rest = refs[:12], refs[12:]
    s0_ref = rest[0]
    if want_out:
        o_refs, s_ref, h_ref = rest[1:3], rest[3], rest[4]
    else:
        s_ref, h_ref = rest[1], rest[2]
    c = pl.program_id(1)
    n_pairs = h_ref.shape[1]

    @pl.when(c == 0)
    def _():
        h_ref[...] = s0_ref[0]

    strict, incl, eye, blk, m0 = _pair_masks(RW_HEAD)
    r2 = lax.broadcasted_iota(jnp.int32, (LANES, LANES), 0) // RW_HEAD
    c2 = lax.broadcasted_iota(jnp.int32, (LANES, LANES), 1) // RW_HEAD
    diag2 = r2 == c2

    chains = [(d, p) for d in range(2) for p in range(n_pairs)]
    cols = lambda p: slice(p * LANES, (p + 1) * LANES)
    at = [ins[6 * d + 0][0, :, cols(p)] for d, p in chains]
    bt = [ins[6 * d + 1][0, :, cols(p)] for d, p in chains]
    kt = [ins[6 * d + 2][0, :, cols(p)] for d, p in chains]
    rt = [ins[6 * d + 3][0, :, cols(p)] for d, p in chains]
    v = [ins[6 * d + 4][:, cols(p)] for d, p in chains]
    pt = [ins[6 * d + 5][0, 0, :, cols(p)] for d, p in chains]
    n = len(chains)
    bd = lambda x: _pair_blockdiag(x, m0)

    s4 = [_mm_nt(jnp.concatenate([at[i], rt[i]], axis=0),
                 jnp.concatenate([bd(bt[i]), bd(kt[i])], axis=0)) for i in range(n)]
    a_ab = [jnp.where(strict[chains[i][0]], s4[i][:CHUNK, :LANES], 0.0) for i in range(n)]
    a_ak = [jnp.where(strict[chains[i][0]], s4[i][:CHUNK, LANES:], 0.0) for i in range(n)]
    t = _tri_inverse(a_ab, eye, blk, m0, 1.0)
    av = [_mm(a_ak[i], bd(v[i])) for i in range(n)]
    wu = [_mm(t[i], jnp.concatenate([bd(at[i]), bd(av[i].astype(BF16))], axis=1)) for i in range(n)]

    ht = [h_ref[d, p] for d, p in chains]
    if want_out:
        m1 = [_mm_nt(jnp.concatenate([wu[i][:, :LANES].astype(BF16), rt[i]], axis=0), ht[i])
              for i in range(n)]
        u = [m1[i][:CHUNK] + wu[i][:, LANES:] for i in range(n)]
    else:
        u = [_mm_nt(wu[i][:, :LANES], ht[i]) + wu[i][:, LANES:] for i in range(n)]
    ub = [x.astype(BF16) for x in u]
    if want_out:
        for i, (d, p) in enumerate(chains):
            m_r = jnp.where(jnp.concatenate([incl[d], incl[d]], axis=1), s4[i][CHUNK:], 0.0)
            o = m1[i][CHUNK:] + _mm(m_r, jnp.concatenate([bd(ub[i]), bd(v[i])], axis=0))
            o_refs[d][:, cols(p)] = o
    for i, (d, p) in enumerate(chains):
        upd = _mm_tn(jnp.concatenate([ub[i], v[i]], axis=0),
                     jnp.concatenate([bt[i], kt[i]], axis=0))
        h_ref[d, p] = (ht[i] + jnp.where(diag2, upd, 0.0)) * pt[i]

    @pl.when(c == pl.num_programs(1) - 1)
    def _():
        s_ref[0] = h_ref[...]


def _rwscan(at, bt, kt, rt, v, pt, s0, batch, want_out):
    m, w = v.shape
    n_chunks = m // batch // CHUNK
    n_pairs = w // LANES
    rows = (lambda b, c: b * n_chunks + c, lambda b, c: b * n_chunks + n_chunks - 1 - c)

    in_specs, args = [], []
    for d in range(2):
        feat_spec = pl.BlockSpec((1, CHUNK, w), lambda b, c, d=d: (d, rows[d](b, c), 0))
        in_specs += [feat_spec] * 4
        in_specs += [pl.BlockSpec((CHUNK, w), lambda b, c, d=d: (rows[d](b, c), 0)),
                     pl.BlockSpec((1, 1, 1, w), lambda b, c, d=d: (d, rows[d](b, c), 0, 0))]
        args += [at, bt, kt, rt, v, pt]
    state_spec = pl.BlockSpec((1, 2, n_pairs, LANES, LANES), lambda b, c: (b, 0, 0, 0, 0))
    out_specs = [state_spec]
    out_shape = [jax.ShapeDtypeStruct(s0.shape, F32)]
    if want_out:
        out_specs = [pl.BlockSpec((CHUNK, w), lambda b, c, d=d: (rows[d](b, c), 0))
                     for d in range(2)] + out_specs
        out_shape = [jax.ShapeDtypeStruct((m, w), F32)] * 2 + out_shape
    return pl.pallas_call(
        functools.partial(_rwscan_kernel, want_out=want_out),
        grid=(batch, n_chunks),
        in_specs=in_specs + [state_spec],
        out_specs=out_specs,
        out_shape=out_shape,
        scratch_shapes=[pltpu.VMEM((2, n_pairs, LANES, LANES), F32)],
        compiler_params=_params(("parallel", "arbitrary")),
        name="rwscan_out" if want_out else "rwscan_state",
    )(*args, s0)


def _rwread_kernel(of_ref, ob_ref, bonus_ref, g_ref, gng_ref, gnb_ref, y_ref):
    o = of_ref[...] + ob_ref[...]
    inv = 1.0 / RW_HEAD
    mean = _seg_sum(o, RW_HEAD) * inv
    cen = o - mean
    var = _seg_sum(cen * cen, RW_HEAD, full_precision=False) * inv
    on = cen * lax.rsqrt(var + RW_GN_EPS) * gng_ref[...] + gnb_ref[...]
    y_ref[...] = ((on + bonus_ref[...]) * g_ref[...]).astype(BF16)


def _rwread(o_f, o_b, bonus, g, gn_g, gn_b):
    m, w = o_f.shape
    tm = _tile(m, 512)
    row_spec = pl.BlockSpec((tm, w), lambda i: (i, 0))
    vec_spec = pl.BlockSpec((1, w), lambda i: (0, 0))
    return pl.pallas_call(
        _rwread_kernel,
        grid=(m // tm,),
        in_specs=[row_spec, row_spec, row_spec, row_spec, vec_spec, vec_spec],
        out_specs=row_spec,
        out_shape=jax.ShapeDtypeStruct((m, w), BF16),
        compiler_params=_params(("parallel",)),
        name="rwread",
    )(o_f, o_b, bonus, g, gn_g, gn_b)


def _gdprep_kernel(pc_ref, pp_ref, pn_ref, ab_ref, cw_ref, alog_ref, dtb_ref,
                   q_ref, k_ref, v_ref, gcum_ref, beta_ref, *, tiles_per_seq):
    i = pl.program_id(0)
    tm = pc_ref.shape[0]
    w = q_ref.shape[1]
    n_heads = w // GD_HEAD
    first = (i % tiles_per_seq) == 0
    last = (i % tiles_per_seq) == tiles_per_seq - 1
    x = pc_ref[:, 0:3 * w]
    prev_row = jnp.where(first, 0.0, pp_ref[7:8, 0:3 * w])
    next_row = jnp.where(last, 0.0, pn_ref[0:1, 0:3 * w])
    xm1, xp1 = _shift_rows(x, prev_row, next_row)
    y = xm1 * cw_ref[0:1, :] + x * cw_ref[1:2, :] + xp1 * cw_ref[2:3, :]
    y = y * _sigmoid(y)
    for h in range(n_heads):
        for part, ref, scale in ((0, q_ref, GD_HEAD ** -0.5), (1, k_ref, 1.0)):
            cols = slice(part * w + h * GD_HEAD, part * w + (h + 1) * GD_HEAD)
            t = y[:, cols]
            ss = jnp.sum(t * t, axis=-1, keepdims=True)
            ref[:, h * GD_HEAD:(h + 1) * GD_HEAD] = (
                t * (lax.rsqrt(jnp.maximum(ss, 1e-12)) * scale)).astype(BF16)
    v_ref[...] = y[:, 2 * w:3 * w].astype(BF16)

    ab = ab_ref[...]
    a = ab[:, 0:2 * n_heads]
    b = ab[:, 2 * n_heads:4 * n_heads]
    glog = -jnp.exp(alog_ref[...]) * _softplus(a + dtb_ref[...])
    beta_ref[...] = _sigmoid(b)
    for c in range(tm // CHUNK):
        rows = slice(c * CHUNK, (c + 1) * CHUNK)
        gc = glog[rows]
        fwd = _mm_sel_left(_tri(False), gc)
        bwd = _mm_sel_left(_tri(True), gc)
        col = lax.broadcasted_iota(jnp.int32, gc.shape, 1)
        gcum_ref[rows, :] = jnp.where(col < n_heads, fwd, bwd)


def _gdprep(p, seq_len, conv_w, a_log, dt_bias, qkvz_block, ab_block, width):
    m = p.shape[0]
    blk = 4 * width
    tm = _tile(seq_len, 256, CHUNK)
    tps = seq_len // tm
    nb8 = m // 8
    n2h = a_log.shape[1]
    full = lambda a: pl.BlockSpec(a.shape, lambda i: (0,) * a.ndim)
    row_spec = pl.BlockSpec((tm, width), lambda i: (i, 0))
    small_spec = pl.BlockSpec((tm, n2h), lambda i: (i, 0))
    return pl.pallas_call(
        functools.partial(_gdprep_kernel, tiles_per_seq=tps),
        grid=(m // tm,),
        in_specs=[pl.BlockSpec((tm, blk), lambda i: (i, qkvz_block)),
                  pl.BlockSpec((8, blk), lambda i: (jnp.maximum(i * (tm // 8) - 1, 0), qkvz_block)),
                  pl.BlockSpec((8, blk),
                               lambda i: (jnp.minimum((i + 1) * (tm // 8), nb8 - 1), qkvz_block)),
                  pl.BlockSpec((tm, LANES), lambda i: (i, ab_block)),
                  full(conv_w), full(a_log), full(dt_bias)],
        out_specs=[row_spec, row_spec, row_spec, small_spec, small_spec],
        out_shape=[jax.ShapeDtypeStruct((m, width), BF16)] * 3
        + [jax.ShapeDtypeStruct((m, n2h), F32)] * 2,
        compiler_params=_params(("parallel",)),
        name="gdprep",
    )(p, p, p, p, conv_w, a_log, dt_bias)


def _gdscan_kernel(*refs, want_out):
    ins, rest = refs[:12], refs[12:]
    s0_ref = rest[0]
    if want_out:
        o_refs, s_ref, st_ref = rest[1:3], rest[3], rest[4]
    else:
        s_ref, st_ref = rest[1], rest[2]
    c = pl.program_id(1)
    n_pairs = st_ref.shape[1] // 2

    @pl.when(c == 0)
    def _():
        st_ref[...] = s0_ref[0]

    strict, incl, eye, blk, m0 = _pair_masks(CHUNK)
    bd = lambda x: _pair_blockdiag(x, m0)
    chains = [(d, p) for d in range(2) for p in range(n_pairs)]
    n = len(chains)
    hcols = lambda h: slice(h * GD_HEAD, (h + 1) * GD_HEAD)

    def head_vals(d, p, e):
        h = 2 * p + e
        q_ref, k_ref, v_ref, g_ref, _, beta_ref = ins[6 * d:6 * d + 6]
        return (q_ref[:, hcols(h)].astype(F32), k_ref[:, hcols(h)].astype(F32),
                v_ref[:, hcols(h)].astype(F32), g_ref[0, :, h:h + 1], beta_ref[0, :, h:h + 1])

    hv = [[head_vals(d, p, e) for e in range(2)] for d, p in chains]
    kb = [[hv[i][e][1] * hv[i][e][4] for e in range(2)] for i in range(n)]
    eg = [[jnp.exp(hv[i][e][3]) for e in range(2)] for i in range(n)]
    decay, s2 = [], []
    for i, (d, p) in enumerate(chains):
        gcol = jnp.where(m0, hv[i][0][3], hv[i][1][3])
        grow = ins[6 * d + 4][0, 0, p:p + 1, :]
        decay.append(jnp.where(incl[d], jnp.exp(jnp.where(incl[d], gcol - grow, 0.0)), 0.0))
        k0, k1 = hv[i][0][1], hv[i][1][1]
        zero = jnp.zeros_like(k0)
        lhs = jnp.concatenate([jnp.concatenate([kb[i][0], kb[i][1]], axis=1),
                               jnp.concatenate([hv[i][0][0], hv[i][1][0]], axis=1)], axis=0)
        rhs = jnp.concatenate([jnp.concatenate([k0, zero], axis=1),
                               jnp.concatenate([zero, k1], axis=1)], axis=0)
        s2.append(_mm_nt(lhs, rhs))
    a = [jnp.where(strict[chains[i][0]], s2[i][:CHUNK] * decay[i], 0.0) for i in range(n)]
    t = _tri_inverse(a, eye, blk, m0, -1.0)
    sol = [_mm(bd(t[i]), jnp.concatenate(
        [jnp.concatenate([hv[i][e][2] * hv[i][e][4], kb[i][e] * eg[i][e]], axis=1)
         for e in range(2)], axis=0)) for i in range(n)]

    st = [[st_ref[d, 2 * p + e] for e in range(2)] for d, p in chains]
    ws = [[_mm(jnp.concatenate([sol[i][e * CHUNK:(e + 1) * CHUNK, GD_HEAD:],
                                hv[i][e][0] * eg[i][e]], axis=0), st[i][e])
           for e in range(2)] for i in range(n)]
    v_new = [[sol[i][e * CHUNK:(e + 1) * CHUNK, :GD_HEAD] - ws[i][e][:CHUNK] for e in range(2)]
             for i in range(n)]
    if want_out:
        for i, (d, p) in enumerate(chains):
            intra = _mm(bd(s2[i][CHUNK:] * decay[i]),
                        jnp.concatenate([v_new[i][0], v_new[i][1]], axis=0))
            for e in range(2):
                o_refs[d][:, hcols(2 * p + e)] = (ws[i][e][CHUNK:]
                                                  + intra[e * CHUNK:(e + 1) * CHUNK])
    for i, (d, p) in enumerate(chains):
        for e in range(2):
            gcol = hv[i][e][3]
            g_last = jnp.min(gcol, axis=0, keepdims=True)
            k_dec = hv[i][e][1] * jnp.exp(g_last - gcol)
            st_ref[d, 2 * p + e] = st[i][e] * jnp.exp(g_last) + _mm_tn(k_dec, v_new[i][e])

    @pl.when(c == pl.num_programs(1) - 1)
    def _():
        s_ref[0] = st_ref[...]


def _gdscan(q, k, v, g, gt, beta, s0, batch, want_out):
    m, w = q.shape
    n_chunks = m // batch // CHUNK
    n_heads = w // GD_HEAD
    rows = (lambda b, c: b * n_chunks + c, lambda b, c: b * n_chunks + n_chunks - 1 - c)

    in_specs, args = [], []
    for d in range(2):
        row_spec = pl.BlockSpec((CHUNK, w), lambda b, c, d=d: (rows[d](b, c), 0))
        col_spec = pl.BlockSpec((1, CHUNK, n_heads), lambda b, c, d=d: (d, rows[d](b, c), 0))
        in_specs += [row_spec, row_spec, row_spec, col_spec,
                     pl.BlockSpec((1, 1, n_heads // 2, LANES),
                                  lambda b, c, d=d: (d, rows[d](b, c), 0, 0)),
                     col_spec]
        args += [q, k, v, g, gt, beta]
    state_spec = pl.BlockSpec((1, 2, n_heads, GD_HEAD, GD_HEAD), lambda b, c: (b, 0, 0, 0, 0))
    out_specs = [state_spec]
    out_shape = [jax.ShapeDtypeStruct(s0.shape, F32)]
    if want_out:
        out_specs = [pl.BlockSpec((CHUNK, w), lambda b, c, d=d: (rows[d](b, c), 0))
                     for d in range(2)] + out_specs
        out_shape = [jax.ShapeDtypeStruct((m, w), F32)] * 2 + out_shape
    return pl.pallas_call(
        functools.partial(_gdscan_kernel, want_out=want_out),
        grid=(batch, n_chunks),
        in_specs=in_specs + [state_spec],
        out_specs=out_specs,
        out_shape=out_shape,
        scratch_shapes=[pltpu.VMEM((2, n_heads, GD_HEAD, GD_HEAD), F32)],
        compiler_params=_params(("parallel", "arbitrary")),
        name="gdscan_out" if want_out else "gdscan_state",
    )(*args, s0)


def _gdread_kernel(of_ref, ob_ref, z_ref, ng_ref, y_ref):
    o = of_ref[...] + ob_ref[...]
    z = z_ref[...]
    gate = z * _sigmoid(z)
    for h in range(o.shape[1] // GD_HEAD):
        cols = slice(h * GD_HEAD, (h + 1) * GD_HEAD)
        oh = o[:, cols]
        ms = jnp.mean(oh * oh, axis=-1, keepdims=True)
        y_ref[:, cols] = (oh * lax.rsqrt(ms + NORM_EPS) * ng_ref[...] * gate[:, cols]).astype(BF16)


def _gdread(o_f, o_b, p, z_block, norm_g):
    m, w = o_f.shape
    tm = _tile(m, 512)
    return pl.pallas_call(
        _gdread_kernel,
        grid=(m // tm,),
        in_specs=[pl.BlockSpec((tm, w), lambda i: (i, 0)),
                  pl.BlockSpec((tm, w), lambda i: (i, 0)),
                  pl.BlockSpec((tm, w), lambda i: (i, z_block)),
                  pl.BlockSpec((1, GD_HEAD), lambda i: (0, 0))],
        out_specs=pl.BlockSpec((tm, w), lambda i: (i, 0)),
        out_shape=jax.ShapeDtypeStruct((m, w), BF16),
        compiler_params=_params(("parallel",)),
        name="gdread",
    )(o_f, o_b, p, norm_g)


def _merge1_kernel(ya_ref, yb_ref, wa_ref, wb_ref, ga_ref, gb_ref, o_ref):
    a = jnp.dot(ya_ref[...], wa_ref[...], preferred_element_type=F32)
    b = jnp.dot(yb_ref[...], wb_ref[...], preferred_element_type=F32)
    o_ref[...] = (_sigmoid(ga_ref[...]) * a + _sigmoid(gb_ref[...]) * b).astype(BF16)


def _merge1(ya, yb, wa, wb, p, gate_col0):
    m, ka = ya.shape
    kb = yb.shape[1]
    d = wa.shape[1]
    tm = _tile(m, 512)
    tn = d
    ga0 = gate_col0 // tn
    gb0 = (gate_col0 + d) // tn
    return pl.pallas_call(
        _merge1_kernel,
        grid=(m // tm, d // tn),
        in_specs=[pl.BlockSpec((tm, ka), lambda i, j: (i, 0)),
                  pl.BlockSpec((tm, kb), lambda i, j: (i, 0)),
                  pl.BlockSpec((ka, tn), lambda i, j: (0, j)),
                  pl.BlockSpec((kb, tn), lambda i, j: (0, j)),
                  pl.BlockSpec((tm, tn), lambda i, j: (i, ga0 + j)),
                  pl.BlockSpec((tm, tn), lambda i, j: (i, gb0 + j))],
        out_specs=pl.BlockSpec((tm, tn), lambda i, j: (i, j)),
        out_shape=jax.ShapeDtypeStruct((m, d), BF16),
        compiler_params=_params(("parallel", "arbitrary")),
        name="merge1",
    )(ya, yb, wa, wb, p, p)


def _merge2_kernel(mg_ref, wo_ref, x_ref, mod_ref, o_ref, *, gate_row):
    y = jnp.dot(mg_ref[...], wo_ref[...], preferred_element_type=F32)
    o_ref[...] = x_ref[...] + mod_ref[0, gate_row:gate_row + 1, :] * y


def _merge2(merged, wo, x2, mod, rows_per_mod, gate_row):
    m, d = x2.shape
    tm = _tile(rows_per_mod, 512)
    tn = d
    per = rows_per_mod // tm
    return pl.pallas_call(
        functools.partial(_merge2_kernel, gate_row=gate_row),
        grid=(m // tm, d // tn),
        in_specs=[pl.BlockSpec((tm, d), lambda i, j: (i, 0)),
                  pl.BlockSpec((d, tn), lambda i, j: (0, j)),
                  pl.BlockSpec((tm, tn), lambda i, j: (i, j)),
                  pl.BlockSpec((1, 8, tn), lambda i, j: (i // per, 0, j))],
        out_specs=pl.BlockSpec((tm, tn), lambda i, j: (i, j)),
        out_shape=jax.ShapeDtypeStruct((m, d), F32),
        compiler_params=_params(("parallel", "arbitrary")),
        name="merge2",
    )(merged, wo, x2, mod)


def _conv3x3_gelu(g, cw_ref):
    n = g.shape[0]
    col = lax.broadcasted_iota(jnp.int32, g.shape, 0) % GRID_W
    left = jnp.where(col > 0, pltpu.roll(g, 1, 0), 0.0)
    right = jnp.where(col < GRID_W - 1, pltpu.roll(g, n - 1, 0), 0.0)
    lines = [left * cw_ref[3 * kh:3 * kh + 1, :] + g * cw_ref[3 * kh + 1:3 * kh + 2, :]
             + right * cw_ref[3 * kh + 2:3 * kh + 3, :] for kh in range(3)]
    pad = jnp.zeros((GRID_W, g.shape[1]), F32)
    acc = (lines[1] + jnp.concatenate([pad, lines[0][:n - GRID_W]], axis=0)
           + jnp.concatenate([lines[2][GRID_W:], pad], axis=0))
    return 0.5 * acc * (1.0 + lax.erf(acc * (2.0 ** -0.5)))


def _ffn_act_kernel(x_ref, mod_ref, g_ref, w1g_ref, w1v_ref, cw_ref, o_ref, h_ref, gate_ref, val_ref,
                    *, sh_row, sc_row, prologue_rows, dot_rows):
    j = pl.program_id(1)

    @pl.when(j == 0)
    def _():
        sh = mod_ref[0, sh_row:sh_row + 1, :]
        sc = mod_ref[0, sc_row:sc_row + 1, :]

        def body(r, carry):
            rows = pl.ds(pl.multiple_of(r * prologue_rows, prologue_rows), prologue_rows)
            x = x_ref[rows, :]
            ms = jnp.mean(x * x, axis=-1, keepdims=True)
            y = x * lax.rsqrt(ms + NORM_EPS) * g_ref[...]
            h_ref[rows, :] = (y * (1.0 + sc) + sh).astype(BF16)
            return carry

        lax.fori_loop(0, x_ref.shape[0] // prologue_rows, body, 0)
        gate_ref[1] = jnp.zeros(gate_ref.shape[1:], F32)
        val_ref[1] = jnp.zeros(val_ref.shape[1:], F32)

    slot = j % 2
    o_ref[...] = (_conv3x3_gelu(gate_ref[1 - slot], cw_ref) * val_ref[1 - slot]).astype(BF16)
    w1g = w1g_ref[...].astype(BF16)
    w1v = w1v_ref[...].astype(BF16)
    for r0 in range(0, h_ref.shape[0], dot_rows):
        rows = slice(r0, r0 + dot_rows)
        hh = h_ref[rows, :]
        gate_ref[slot, rows, :] = jnp.dot(hh, w1g, preferred_element_type=F32)
        val_ref[slot, rows, :] = jnp.dot(hh, w1v, preferred_element_type=F32)


def _ffn_act(x1, mod, gain, w1, conv_w, batch, sh_row, sc_row):
    m, d = x1.shape
    dff = conv_w.shape[1]
    seq = m // batch
    tf = _tile(dff, 256, LANES)
    nblk = dff // tf
    cur = lambda j: jnp.minimum(j, nblk - 1)
    prev = lambda j: jnp.maximum(j - 1, 0)
    return pl.pallas_call(
        functools.partial(_ffn_act_kernel, sh_row=sh_row, sc_row=sc_row,
                          prologue_rows=_tile(seq, 256), dot_rows=_tile(seq, 2048)),
        grid=(batch, nblk + 1),
        in_specs=[pl.BlockSpec((seq, d), lambda b, j: (b, 0), pipeline_mode=pl.Buffered(1)),
                  pl.BlockSpec((1, 8, d), lambda b, j: (b, 0, 0)),
                  pl.BlockSpec((1, d), lambda b, j: (0, 0)),
                  pl.BlockSpec((d, tf), lambda b, j: (0, cur(j))),
                  pl.BlockSpec((d, tf), lambda b, j: (0, nblk + cur(j))),
                  pl.BlockSpec((conv_w.shape[0], tf), lambda b, j: (0, prev(j)))],
        out_specs=pl.BlockSpec((seq, tf), lambda b, j: (b, prev(j))),
        out_shape=jax.ShapeDtypeStruct((m, dff), BF16),
        scratch_shapes=[pltpu.VMEM((seq, d), BF16),
                        pltpu.VMEM((2, seq, tf), F32),
                        pltpu.VMEM((2, seq, tf), F32)],
        compiler_params=_params(("parallel", "arbitrary")),
        name="ffn_act",
    )(x1, mod, gain, w1, w1, conv_w)


def _ffn_out_kernel(act_ref, w2_ref, x_ref, mod_ref, g_ref, o_ref, acc_ref, *, gate_row):
    kstep = pl.program_id(1)

    @pl.when(kstep == 0)
    def _():
        acc_ref[...] = jnp.zeros_like(acc_ref)

    acc_ref[...] += jnp.dot(act_ref[...], w2_ref[...], preferred_element_type=F32)

    @pl.when(kstep == pl.num_programs(1) - 1)
    def _():
        y = x_ref[...] + mod_ref[0, gate_row:gate_row + 1, :] * acc_ref[...]
        ms = jnp.mean(y * y, axis=-1, keepdims=True)
        o_ref[...] = y * lax.rsqrt(ms + NORM_EPS) * g_ref[...]


def _ffn_out(act, w2, x1, mod, final_g, rows_per_mod, gate_row):
    m, d = x1.shape
    dff = act.shape[1]
    tm = _tile(rows_per_mod, 512)
    tk = _tile(dff, 1408, LANES)
    per = rows_per_mod // tm
    return pl.pallas_call(
        functools.partial(_ffn_out_kernel, gate_row=gate_row),
        grid=(m // tm, dff // tk),
        in_specs=[pl.BlockSpec((tm, tk), lambda i, k: (i, k)),
                  pl.BlockSpec((tk, d), lambda i, k: (k, 0)),
                  pl.BlockSpec((tm, d), lambda i, k: (i, 0)),
                  pl.BlockSpec((1, 8, d), lambda i, k: (i // per, 0, 0)),
                  pl.BlockSpec((1, d), lambda i, k: (0, 0))],
        out_specs=pl.BlockSpec((tm, d), lambda i, k: (i, 0)),
        out_shape=jax.ShapeDtypeStruct((m, d), F32),
        scratch_shapes=[pltpu.VMEM((tm, d), F32)],
        compiler_params=_params(("parallel", "arbitrary")),
        name="ffn_out",
    )(act, w2, x1, mod, final_g)


def _pad_cols(a, width):
    return jnp.pad(a, [(0, 0)] * (a.ndim - 1) + [(0, width - a.shape[-1])])


def _pad_rank(a):
    return jnp.pad(a, [(0, 0)] * (a.ndim - 2) + [(0, RANK_PAD - a.shape[-2]), (0, 0)])


def kernel(x, c, ctx, c_ctx, w_ada, b_ada, norm1_g, norm2_g, w_in, rw_mu, rw_k_k, rw_k_a, rw_r_k, rw_w0, rw_w_up, rw_a0, rw_a_up, rw_g_up, rw_gn_g, rw_gn_b, gd_conv_w, gd_a_log, gd_dt_bias, gd_norm_g, w_a_out, w_b_out, w_o, ffn_w1, ffn_conv_w, ffn_w2, final_norm_g):
    batch, seq, d = x.shape
    ctx_len = ctx.shape[1]
    assert w_ada.shape[0] == 1, "single layer only"
    rw_w = rw_k_k.shape[1]
    gd_w = w_b_out.shape[1]
    dec_rank = rw_w_up.shape[2]
    icl_rank = rw_a_up.shape[2]
    gate_rank = rw_g_up.shape[1]
    gd_heads = gd_a_log.shape[2]
    assert max(dec_rank, icl_rank) <= RANK_PAD and 4 * gd_heads <= 2 * LANES
    assert seq % CHUNK == 0 and ctx_len % CHUNK == 0 and seq % GRID_W == 0
    assert 3 * rw_w == 3 * gd_w and gate_rank <= 2 * LANES

    low_w = 8 * LANES
    blk0 = 3 * rw_w + low_w
    assert blk0 == 4 * gd_w
    wi = w_in[0]
    o_rw = 3 * rw_w
    o_gd = o_rw + 2 * dec_rank + 2 * icl_rank + gate_rank
    o_ab = o_gd + 4 * gd_w
    o_gate = o_ab + 4 * gd_heads

    def pack_cols(a):
        pieces = [a[..., :o_rw]]
        off = o_rw
        for r in (dec_rank, dec_rank, icl_rank, icl_rank):
            pieces.append(_pad_cols(a[..., off:off + r], RANK_PAD))
            off += r
        pieces.append(_pad_cols(a[..., off:off + gate_rank], 2 * LANES))
        return pieces

    w_pack = jnp.concatenate(
        pack_cols(wi) + [_pad_cols(wi[:, o_ab:o_gate], 2 * LANES), wi[:, o_gd:o_ab], wi[:, o_gate:]],
        axis=1).astype(BF16)
    n_ctx_cols = 2 * blk0
    gate_col0 = 2 * blk0
    mu_pack = jnp.concatenate(pack_cols(rw_mu) + [jnp.zeros((1, 2 * LANES), F32)], axis=1)

    cc = jnp.concatenate([c, c_ctx[None, :], jnp.zeros((16 - batch - 1, d), F32)], axis=0)
    mods = _mod(cc, w_ada[0], b_ada)
    mod_lat = _pad_rows8(mods[:batch].reshape(batch, 6, d))
    mod_ctx = _pad_rows8(mods[batch:batch + 1].reshape(1, 6, d))

    x2 = x.reshape(batch * seq, d)
    ctx2 = ctx.reshape(batch * ctx_len, d)
    p_lat = _normproj(x2, mod_lat, norm1_g, w_pack, seq, w_pack.shape[1], 0, 1, "inproj_lat")
    p_ctx = _normproj(ctx2, mod_ctx, norm1_g, w_pack, batch * ctx_len, n_ctx_cols, 0, 1, "inproj_ctx")

    rw_wts = (mu_pack, rw_k_k, rw_k_a, rw_r_k, rw_w0[0], _pad_rank(rw_w_up[0]).astype(BF16),
              rw_a0[0], _pad_rank(rw_a_up[0]).astype(BF16),
              jnp.pad(rw_g_up[0], ((0, 2 * LANES - gate_rank), (0, 0))).astype(BF16))
    n_pairs = rw_w // LANES
    s0 = jnp.zeros((batch, 2, n_pairs, LANES, LANES), F32)
    f_ctx = _rwprep(p_ctx, ctx_len, rw_wts)
    f_lat = _rwprep(p_lat, seq, rw_wts)
    (s_ctx,) = _rwscan(*f_ctx[:6], s0, batch, want_out=False)
    o_rw_f, o_rw_b, _ = _rwscan(*f_lat[:6], s_ctx, batch, want_out=True)
    ya = _rwread(o_rw_f, o_rw_b, f_lat[6], f_lat[7], rw_gn_g, rw_gn_b)

    s0g = jnp.zeros((batch, 2, gd_heads, GD_HEAD, GD_HEAD), F32)
    a_log2 = gd_a_log[0].reshape(1, 2 * gd_heads)
    dtb2 = gd_dt_bias[0].reshape(1, 2 * gd_heads)

    def gd_feats(p, seq_len):
        ab_block = (3 * rw_w + 4 * RANK_PAD + 2 * LANES) // LANES
        q, k, v, gcum, beta = _gdprep(p, seq_len, gd_conv_w[0], a_log2, dtb2, 1, ab_block, gd_w)
        m = p.shape[0]
        g3 = gcum.reshape(m, 2, gd_heads).transpose(1, 0, 2)
        gt = g3.reshape(2, m // CHUNK, CHUNK, gd_heads).transpose(0, 1, 3, 2)
        gt = gt.reshape(2, m // CHUNK, gd_heads // 2, 2 * CHUNK)
        b3 = beta.reshape(m, 2, gd_heads).transpose(1, 0, 2)
        return q, k, v, g3, gt, b3

    (sg_ctx,) = _gdscan(*gd_feats(p_ctx, ctx_len), s0g, batch, want_out=False)
    o_gd_f, o_gd_b, _ = _gdscan(*gd_feats(p_lat, seq), sg_ctx, batch, want_out=True)
    yb = _gdread(o_gd_f, o_gd_b, p_lat, 2 * blk0 // gd_w - 1, gd_norm_g)

    merged = _merge1(ya, yb, w_a_out[0].astype(BF16), w_b_out[0].astype(BF16), p_lat, gate_col0)
    x1 = _merge2(merged, w_o[0].astype(BF16), x2, mod_lat, seq, 2)

    act = _ffn_act(x1, mod_lat, norm2_g, ffn_w1[0],
                   ffn_conv_w[0].reshape(-1, ffn_conv_w.shape[-1]), batch, 3, 4)
    out = _ffn_out(act, ffn_w2[0].astype(BF16), x1, mod_lat, final_norm_g[None, :], seq, 5)
    return out.reshape(batch, seq, d)


def _pad_rows8(a):
    return jnp.pad(a, ((0, 0), (0, 8 - a.shape[1]), (0, 0)))
```

```python
import functools
import math

import jax
import jax.numpy as jnp
from jax import lax
from jax.experimental import pallas as pl
from jax.experimental.pallas import tpu as pltpu

F32 = jnp.float32
BF16 = jnp.bfloat16
HIGHEST = lax.Precision.HIGHEST

NORM_EPS = 1e-6
RW_GN_EPS = 64e-5
RW_HEAD = 64
GD_HEAD = 128
LANES = 128
CHUNK = 64
GRID_W = 64
RANK_PAD = 128
VMEM_LIMIT = 56 * 1024 * 1024


def _params(sem):
    return pltpu.CompilerParams(dimension_semantics=sem, vmem_limit_bytes=VMEM_LIMIT)


def _tile(n, pref, mult=8):
    if n <= pref:
        return n
    t = (pref // mult) * mult
    while t >= mult:
        if n % t == 0:
            return t
        t -= mult
    return n


def _mm(a, b):
    return jnp.dot(a.astype(BF16), b.astype(BF16), preferred_element_type=F32)


def _mm_nt(a, b):
    return lax.dot_general(a.astype(BF16), b.astype(BF16), (((1,), (1,)), ((), ())),
                           preferred_element_type=F32)


def _mm_tn(a, b):
    return lax.dot_general(a.astype(BF16), b.astype(BF16), (((0,), (0,)), ((), ())),
                           preferred_element_type=F32)


def _mm_hi(a, b):
    return jnp.dot(a, b, precision=HIGHEST, preferred_element_type=F32)


def _split3(x):
    x1 = x.astype(BF16)
    r1 = x - x1.astype(F32)
    x2 = r1.astype(BF16)
    x3 = (r1 - x2.astype(F32)).astype(BF16)
    return x1, x2, x3


def _mm_sel_left(c, x):
    cb = c.astype(BF16)
    return jnp.dot(jnp.concatenate([cb, cb, cb], axis=1), jnp.concatenate(_split3(x), axis=0),
                   preferred_element_type=F32)


def _mm_sel_right(x, c):
    cb = c.astype(BF16)
    return jnp.dot(jnp.concatenate(_split3(x), axis=1), jnp.concatenate([cb, cb, cb], axis=0),
                   preferred_element_type=F32)


def _softplus(x):
    return jnp.maximum(x, 0.0) + jnp.log(1.0 + jnp.exp(-jnp.abs(x)))


def _sigmoid(x):
    return jax.nn.sigmoid(x)


def _seg_ones(width):
    i = lax.broadcasted_iota(jnp.int32, (LANES, LANES), 0) // width
    j = lax.broadcasted_iota(jnp.int32, (LANES, LANES), 1) // width
    return (i == j).astype(F32)


def _seg_sum(x, width, full_precision=True):
    e = _seg_ones(width)
    n = x.shape[-1] // LANES
    mm = _mm_sel_right if full_precision else _mm
    parts = [mm(x[:, g * LANES:(g + 1) * LANES], e) for g in range(n)]
    return parts[0] if n == 1 else jnp.concatenate(parts, axis=-1)


def _tri(rev):
    i = lax.broadcasted_iota(jnp.int32, (CHUNK, CHUNK), 0)
    j = lax.broadcasted_iota(jnp.int32, (CHUNK, CHUNK), 1)
    return ((j >= i) if rev else (j <= i)).astype(F32)


def _level_masks(i, j):
    masks = [(i // 2) == (j // 2)]
    s = 2
    while s < CHUNK:
        masks.append(((i // (2 * s)) == (j // (2 * s))) & ((i // s) != (j // s)))
        s *= 2
    return masks


def _shift_rows(x, prev_row, next_row):
    n = x.shape[0]
    row = lax.broadcasted_iota(jnp.int32, x.shape, 0)
    xm1 = jnp.where(row == 0, prev_row, pltpu.roll(x, 1, 0))
    xp1 = jnp.where(row == n - 1, next_row, pltpu.roll(x, n - 1, 0))
    return xm1, xp1


def _store_stencil(dst_ref, x, prev_row, next_row, f):
    n = x.shape[0]
    dst_ref[...] = f(pltpu.roll(x, 1, 0), x, pltpu.roll(x, n - 1, 0))
    dst_ref[0:1, :] = f(prev_row, x[0:1], x[1:2])
    dst_ref[n - 1:n, :] = f(x[n - 2:n - 1], x[n - 1:n], next_row)


def _mod_kernel(c_ref, w_ref, b_ref, o_ref):
    c = c_ref[...]
    s = c * _sigmoid(c)
    o_ref[...] = _mm_hi(s, w_ref[...]) + b_ref[...]


def _mod(cc, w_ada, b_ada):
    rows, d = cc.shape
    n = w_ada.shape[1]
    tn = _tile(n, 1024, LANES)
    return pl.pallas_call(
        _mod_kernel,
        grid=(n // tn,),
        in_specs=[pl.BlockSpec((rows, d), lambda j: (0, 0)),
                  pl.BlockSpec((d, tn), lambda j: (0, j)),
                  pl.BlockSpec((1, tn), lambda j: (0, j))],
        out_specs=pl.BlockSpec((rows, tn), lambda j: (0, j)),
        out_shape=jax.ShapeDtypeStruct((rows, n), F32),
        compiler_params=_params(("arbitrary",)),
        name="mod",
    )(cc, w_ada, b_ada)


def _normproj_kernel(x_ref, mod_ref, g_ref, w_ref, o_ref, h_ref, *, sh_row, sc_row):
    @pl.when(pl.program_id(1) == 0)
    def _():
        x = x_ref[...]
        ms = jnp.mean(x * x, axis=-1, keepdims=True)
        y = x * lax.rsqrt(ms + NORM_EPS) * g_ref[...]
        sh = mod_ref[0, sh_row:sh_row + 1, :]
        sc = mod_ref[0, sc_row:sc_row + 1, :]
        h_ref[...] = (y * (1.0 + sc) + sh).astype(BF16)

    o_ref[...] = jnp.dot(h_ref[...], w_ref[...], preferred_element_type=F32)


def _normproj(x2, mod, gain, w, rows_per_mod, n_cols, sh_row, sc_row, name):
    m, d = x2.shape
    tm = _tile(rows_per_mod, 1024)
    tn = _tile(n_cols, 1024, LANES)
    per = rows_per_mod // tm
    return pl.pallas_call(
        functools.partial(_normproj_kernel, sh_row=sh_row, sc_row=sc_row),
        grid=(m // tm, n_cols // tn),
        in_specs=[pl.BlockSpec((tm, d), lambda i, j: (i, 0)),
                  pl.BlockSpec((1, 8, d), lambda i, j: (i // per, 0, 0)),
                  pl.BlockSpec((1, d), lambda i, j: (0, 0)),
                  pl.BlockSpec((d, tn), lambda i, j: (0, j))],
        out_specs=pl.BlockSpec((tm, tn), lambda i, j: (i, j)),
        out_shape=jax.ShapeDtypeStruct((m, n_cols), F32),
        scratch_shapes=[pltpu.VMEM((tm, d), BF16)],
        compiler_params=_params(("parallel", "arbitrary")),
        name=name,
    )(x2, mod, gain, w)


def _rwprep_kernel(pc_ref, pp_ref, pn_ref, mu_ref, kk_ref, ka_ref, rk_ref, w0_ref, wup_ref,
                   a0_ref, aup_ref, gup_ref,
                   at_ref, bt_ref, kt_ref, rt_ref, v_ref, pt_ref, bonus_ref, g_ref, xs_ref,
                   *, tiles_per_seq):
    i = pl.program_id(0)
    tm = pc_ref.shape[0]
    w = RW_HEAD * (kk_ref.shape[1] // RW_HEAD)
    first = (i % tiles_per_seq) == 0
    last = (i % tiles_per_seq) == tiles_per_seq - 1
    base = 3 * w
    used = base + 4 * RANK_PAD + gup_ref.shape[0]
    x = pc_ref[:, 0:used]
    prev_row = jnp.where(first, 0.0, pp_ref[7:8, 0:used])
    next_row = jnp.where(last, 0.0, pn_ref[0:1, 0:used])
    mu = mu_ref[:, 0:used]
    keep = 1.0 - mu
    half = 0.5 * mu
    _store_stencil(xs_ref, x, prev_row, next_row, lambda a, c, b: c * keep + (a + b) * half)

    r = xs_ref[:, 0:w]
    k = xs_ref[:, w:2 * w]
    v = xs_ref[:, 2 * w:3 * w]
    gd = xs_ref[:, base + 4 * RANK_PAD:used]
    g_ref[...] = _mm(_sigmoid(gd), gup_ref[...]).astype(BF16)
    v_ref[...] = v.astype(BF16)

    kx = k * kk_ref[...]
    kk = kx * lax.rsqrt(jnp.maximum(_seg_sum(kx * kx, RW_HEAD, full_precision=False), 1e-12))

    ksum = jnp.zeros_like(k)
    for d in range(2):
        wd = xs_ref[:, base + d * RANK_PAD: base + (d + 1) * RANK_PAD]
        ad = xs_ref[:, base + (2 + d) * RANK_PAD: base + (3 + d) * RANK_PAD]
        wl = w0_ref[d:d + 1, :] + _mm(jnp.tanh(wd), wup_ref[d])
        lw = -math.exp(-0.5) * _sigmoid(wl)
        a = _sigmoid(a0_ref[d:d + 1, :] + _mm(ad, aup_ref[d]))
        kd = k * (1.0 + (a - 1.0) * ka_ref[...])
        ksum = ksum + kd
        tri = _tri(rev=(d == 1))
        for c in range(tm // CHUNK):
            rows = slice(c * CHUNK, (c + 1) * CHUNK)
            lwc = lw[rows]
            cum = _mm_sel_left(tri, lwc)
            p_in = jnp.exp(cum)
            p_inv = jnp.exp(-cum)
            p_ex = jnp.exp(cum - lwc)
            at_ref[d, rows, :] = (-kk[rows] * p_ex).astype(BF16)
            bt_ref[d, rows, :] = (kk[rows] * a[rows] * p_inv).astype(BF16)
            kt_ref[d, rows, :] = (kd[rows] * p_inv).astype(BF16)
            rt_ref[d, rows, :] = (r[rows] * p_in).astype(BF16)
            tot = cum[CHUNK - 1:CHUNK] if d == 0 else cum[0:1]
            pt_ref[d, c, :, :] = jnp.exp(tot)
    bonus_ref[...] = (_seg_sum(r * ksum * rk_ref[...], RW_HEAD, full_precision=False) * v).astype(BF16)


def _rwprep(p, seq_len, wts):
    m = p.shape[0]
    mu, k_k, k_a, r_k, w0, w_up, a0, a_up, g_up = wts
    w = k_k.shape[1]
    blk = 3 * w + 4 * RANK_PAD + g_up.shape[0]
    tm = _tile(seq_len, 256, CHUNK)
    tps = seq_len // tm
    nb8 = m // 8
    full = lambda a: pl.BlockSpec(a.shape, lambda i: (0,) * a.ndim)
    feat = jax.ShapeDtypeStruct((2, m, w), BF16)
    feat_spec = pl.BlockSpec((2, tm, w), lambda i: (0, i, 0))
    row_spec = pl.BlockSpec((tm, w), lambda i: (i, 0))
    return pl.pallas_call(
        functools.partial(_rwprep_kernel, tiles_per_seq=tps),
        grid=(m // tm,),
        in_specs=[pl.BlockSpec((tm, blk), lambda i: (i, 0)),
                  pl.BlockSpec((8, blk), lambda i: (jnp.maximum(i * (tm // 8) - 1, 0), 0)),
                  pl.BlockSpec((8, blk), lambda i: (jnp.minimum((i + 1) * (tm // 8), nb8 - 1), 0)),
                  full(mu), full(k_k), full(k_a), full(r_k), full(w0), full(w_up), full(a0),
                  full(a_up), full(g_up)],
        out_specs=[feat_spec, feat_spec, feat_spec, feat_spec, row_spec,
                   pl.BlockSpec((2, tm // CHUNK, 1, w), lambda i: (0, i, 0, 0)),
                   row_spec, row_spec],
        out_shape=[feat, feat, feat, feat, jax.ShapeDtypeStruct((m, w), BF16),
                   jax.ShapeDtypeStruct((2, m // CHUNK, 1, w), F32),
                   jax.ShapeDtypeStruct((m, w), BF16), jax.ShapeDtypeStruct((m, w), BF16)],
        scratch_shapes=[pltpu.VMEM((tm, blk), F32)],
        compiler_params=_params(("parallel",)),
        name="rwprep",
    )(p, p, p, mu, k_k, k_a, r_k, w0, w_up, a0, a_up, g_up)


def _pair_blockdiag(x, m0):
    zero = jnp.zeros_like(x)
    return jnp.concatenate([jnp.where(m0, x, zero), jnp.where(m0, zero, x)], axis=0)


def _tri_inverse(a, eye, blk, m0, sign):
    ts = [eye + sign * jnp.where(blk[0], x, 0.0) for x in a]
    for lvl in range(1, len(blk)):
        xs = [_mm(t, _pair_blockdiag(jnp.where(blk[lvl], x, 0.0), m0)) for t, x in zip(ts, a)]
        ts = [t + sign * _mm(x, _pair_blockdiag(t, m0)) for t, x in zip(ts, xs)]
    return ts


def _pair_masks(head_cols):
    i = lax.broadcasted_iota(jnp.int32, (CHUNK, LANES), 0)
    lane = lax.broadcasted_iota(jnp.int32, (CHUNK, LANES), 1)
    j = lane % head_cols
    strict = (j < i, j > i)
    incl = (j <= i, j >= i)
    eye = (j == i).astype(F32)
    return strict, incl, eye, _level_masks(i, j), lane < head_cols


def _rwscan_kernel(*refs, want_out):
    ins, rest = refs[:12], refs[12:]
    s0_ref = rest[0]
    if want_out:
        o_refs, s_ref, h_ref = rest[1:3], rest[3], rest[4]
    else:
        s_ref, h_ref = rest[1], rest[2]
    c = pl.program_id(1)
    n_pairs = h_ref.shape[1]

    @pl.when(c == 0)
    def _():
        h_ref[...] = s0_ref[0]

    strict, incl, eye, blk, m0 = _pair_masks(RW_HEAD)
    r2 = lax.broadcasted_iota(jnp.int32, (LANES, LANES), 0) // RW_HEAD
    c2 = lax.broadcasted_iota(jnp.int32, (LANES, LANES), 1) // RW_HEAD
    diag2 = r2 == c2

    chains = [(d, p) for d in range(2) for p in range(n_pairs)]
    cols = lambda p: slice(p * LANES, (p + 1) * LANES)
    at = [ins[6 * d + 0][0, :, cols(p)] for d, p in chains]
    bt = [ins[6 * d + 1][0, :, cols(p)] for d, p in chains]
    kt = [ins[6 * d + 2][0, :, cols(p)] for d, p in chains]
    rt = [ins[6 * d + 3][0, :, cols(p)] for d, p in chains]
    v = [ins[6 * d + 4][:, cols(p)] for d, p in chains]
    pt = [ins[6 * d + 5][0, 0, :, cols(p)] for d, p in chains]
    n = len(chains)
    bd = lambda x: _pair_blockdiag(x, m0)

    s4 = [_mm_nt(jnp.concatenate([at[i], rt[i]], axis=0),
                 jnp.concatenate([bd(bt[i]), bd(kt[i])], axis=0)) for i in range(n)]
    a_ab = [jnp.where(strict[chains[i][0]], s4[i][:CHUNK, :LANES], 0.0) for i in range(n)]
    a_ak = [jnp.where(strict[chains[i][0]], s4[i][:CHUNK, LANES:], 0.0) for i in range(n)]
    t = _tri_inverse(a_ab, eye, blk, m0, 1.0)
    av = [_mm(a_ak[i], bd(v[i])) for i in range(n)]
    wu = [_mm(t[i], jnp.concatenate([bd(at[i]), bd(av[i].astype(BF16))], axis=1)) for i in range(n)]

    ht = [h_ref[d, p] for d, p in chains]
    if want_out:
        m1 = [_mm_nt(jnp.concatenate([wu[i][:, :LANES].astype(BF16), rt[i]], axis=0), ht[i])
              for i in range(n)]
        u = [m1[i][:CHUNK] + wu[i][:, LANES:] for i in range(n)]
    else:
        u = [_mm_nt(wu[i][:, :LANES], ht[i]) + wu[i][:, LANES:] for i in range(n)]
    ub = [x.astype(BF16) for x in u]
    if want_out:
        for i, (d, p) in enumerate(chains):
            m_r = jnp.where(jnp.concatenate([incl[d], incl[d]], axis=1), s4[i][CHUNK:], 0.0)
            o = m1[i][CHUNK:] + _mm(m_r, jnp.concatenate([bd(ub[i]), bd(v[i])], axis=0))
            o_refs[d][:, cols(p)] = o
    for i, (d, p) in enumerate(chains):
        upd = _mm_tn(jnp.concatenate([ub[i], v[i]], axis=0),
                     jnp.concatenate([bt[i], kt[i]], axis=0))
        h_ref[d, p] = (ht[i] + jnp.where(diag2, upd, 0.0)) * pt[i]

    @pl.when(c == pl.num_programs(1) - 1)
    def _():
        s_ref[0] = h_ref[...]


def _rwscan(at, bt, kt, rt, v, pt, s0, batch, want_out):
    m, w = v.shape
    n_chunks = m // batch // CHUNK
    n_pairs = w // LANES
    rows = (lambda b, c: b * n_chunks + c, lambda b, c: b * n_chunks + n_chunks - 1 - c)

    in_specs, args = [], []
    for d in range(2):
        feat_spec = pl.BlockSpec((1, CHUNK, w), lambda b, c, d=d: (d, rows[d](b, c), 0))
        in_specs += [feat_spec] * 4
        in_specs += [pl.BlockSpec((CHUNK, w), lambda b, c, d=d: (rows[d](b, c), 0)),
                     pl.BlockSpec((1, 1, 1, w), lambda b, c, d=d: (d, rows[d](b, c), 0, 0))]
        args += [at, bt, kt, rt, v, pt]
    state_spec = pl.BlockSpec((1, 2, n_pairs, LANES, LANES), lambda b, c: (b, 0, 0, 0, 0))
    out_specs = [state_spec]
    out_shape = [jax.ShapeDtypeStruct(s0.shape, F32)]
    if want_out:
        out_specs = [pl.BlockSpec((CHUNK, w), lambda b, c, d=d: (rows[d](b, c), 0))
                     for d in range(2)] + out_specs
        out_shape = [jax.ShapeDtypeStruct((m, w), F32)] * 2 + out_shape
    return pl.pallas_call(
        functools.partial(_rwscan_kernel, want_out=want_out),
        grid=(batch, n_chunks),
        in_specs=in_specs + [state_spec],
        out_specs=out_specs,
        out_shape=out_shape,
        scratch_shapes=[pltpu.VMEM((2, n_pairs, LANES, LANES), F32)],
        compiler_params=_params(("parallel", "arbitrary")),
        name="rwscan_out" if want_out else "rwscan_state",
    )(*args, s0)


def _rwread_kernel(of_ref, ob_ref, bonus_ref, g_ref, gng_ref, gnb_ref, y_ref):
    o = of_ref[...] + ob_ref[...]
    inv = 1.0 / RW_HEAD
    mean = _seg_sum(o, RW_HEAD) * inv
    cen = o - mean
    var = _seg_sum(cen * cen, RW_HEAD, full_precision=False) * inv
    on = cen * lax.rsqrt(var + RW_GN_EPS) * gng_ref[...] + gnb_ref[...]
    y_ref[...] = ((on + bonus_ref[...]) * g_ref[...]).astype(BF16)


def _rwread(o_f, o_b, bonus, g, gn_g, gn_b):
    m, w = o_f.shape
    tm = _tile(m, 512)
    row_spec = pl.BlockSpec((tm, w), lambda i: (i, 0))
    vec_spec = pl.BlockSpec((1, w), lambda i: (0, 0))
    return pl.pallas_call(
        _rwread_kernel,
        grid=(m // tm,),
        in_specs=[row_spec, row_spec, row_spec, row_spec, vec_spec, vec_spec],
        out_specs=row_spec,
        out_shape=jax.ShapeDtypeStruct((m, w), BF16),
        compiler_params=_params(("parallel",)),
        name="rwread",
    )(o_f, o_b, bonus, g, gn_g, gn_b)


def _gdprep_kernel(pc_ref, pp_ref, pn_ref, ab_ref, cw_ref, alog_ref, dtb_ref,
                   q_ref, k_ref, v_ref, gcum_ref, beta_ref, *, tiles_per_seq):
    i = pl.program_id(0)
    tm = pc_ref.shape[0]
    w = q_ref.shape[1]
    n_heads = w // GD_HEAD
    first = (i % tiles_per_seq) == 0
    last = (i % tiles_per_seq) == tiles_per_seq - 1
    x = pc_ref[:, 0:3 * w]
    prev_row = jnp.where(first, 0.0, pp_ref[7:8, 0:3 * w])
    next_row = jnp.where(last, 0.0, pn_ref[0:1, 0:3 * w])
    xm1, xp1 = _shift_rows(x, prev_row, next_row)
    y = xm1 * cw_ref[0:1, :] + x * cw_ref[1:2, :] + xp1 * cw_ref[2:3, :]
    y = y * _sigmoid(y)
    for h in range(n_heads):
        for part, ref, scale in ((0, q_ref, GD_HEAD ** -0.5), (1, k_ref, 1.0)):
            cols = slice(part * w + h * GD_HEAD, part * w + (h + 1) * GD_HEAD)
            t = y[:, cols]
            ss = jnp.sum(t * t, axis=-1, keepdims=True)
            ref[:, h * GD_HEAD:(h + 1) * GD_HEAD] = (
                t * (lax.rsqrt(jnp.maximum(ss, 1e-12)) * scale)).astype(BF16)
    v_ref[...] = y[:, 2 * w:3 * w].astype(BF16)

    ab = ab_ref[...]
    a = ab[:, 0:2 * n_heads]
    b = ab[:, 2 * n_heads:4 * n_heads]
    glog = -jnp.exp(alog_ref[...]) * _softplus(a + dtb_ref[...])
    beta_ref[...] = _sigmoid(b)
    for c in range(tm // CHUNK):
        rows = slice(c * CHUNK, (c + 1) * CHUNK)
        gc = glog[rows]
        fwd = _mm_sel_left(_tri(False), gc)
        bwd = _mm_sel_left(_tri(True), gc)
        col = lax.broadcasted_iota(jnp.int32, gc.shape, 1)
        gcum_ref[rows, :] = jnp.where(col < n_heads, fwd, bwd)


def _gdprep(p, seq_len, conv_w, a_log, dt_bias, qkvz_block, ab_block, width):
    m = p.shape[0]
    blk = 4 * width
    tm = _tile(seq_len, 256, CHUNK)
    tps = seq_len // tm
    nb8 = m // 8
    n2h = a_log.shape[1]
    full = lambda a: pl.BlockSpec(a.shape, lambda i: (0,) * a.ndim)
    row_spec = pl.BlockSpec((tm, width), lambda i: (i, 0))
    small_spec = pl.BlockSpec((tm, n2h), lambda i: (i, 0))
    return pl.pallas_call(
        functools.partial(_gdprep_kernel, tiles_per_seq=tps),
        grid=(m // tm,),
        in_specs=[pl.BlockSpec((tm, blk), lambda i: (i, qkvz_block)),
                  pl.BlockSpec((8, blk), lambda i: (jnp.maximum(i * (tm // 8) - 1, 0), qkvz_block)),
                  pl.BlockSpec((8, blk),
                               lambda i: (jnp.minimum((i + 1) * (tm // 8), nb8 - 1), qkvz_block)),
                  pl.BlockSpec((tm, LANES), lambda i: (i, ab_block)),
                  full(conv_w), full(a_log), full(dt_bias)],
        out_specs=[row_spec, row_spec, row_spec, small_spec, small_spec],
        out_shape=[jax.ShapeDtypeStruct((m, width), BF16)] * 3
        + [jax.ShapeDtypeStruct((m, n2h), F32)] * 2,
        compiler_params=_params(("parallel",)),
        name="gdprep",
    )(p, p, p, p, conv_w, a_log, dt_bias)


def _gdscan_kernel(*refs, want_out):
    ins, rest = refs[:12], refs[12:]
    s0_ref = rest[0]
    if want_out:
        o_refs, s_ref, st_ref = rest[1:3], rest[3], rest[4]
    else:
        s_ref, st_ref = rest[1], rest[2]
    c = pl.program_id(1)
    n_pairs = st_ref.shape[1] // 2

    @pl.when(c == 0)
    def _():
        st_ref[...] = s0_ref[0]

    strict, incl, eye, blk, m0 = _pair_masks(CHUNK)
    bd = lambda x: _pair_blockdiag(x, m0)
    chains = [(d, p) for d in range(2) for p in range(n_pairs)]
    n = len(chains)
    hcols = lambda h: slice(h * GD_HEAD, (h + 1) * GD_HEAD)

    def head_vals(d, p, e):
        h = 2 * p + e
        q_ref, k_ref, v_ref, g_ref, _, beta_ref = ins[6 * d:6 * d + 6]
        return (q_ref[:, hcols(h)].astype(F32), k_ref[:, hcols(h)].astype(F32),
                v_ref[:, hcols(h)].astype(F32), g_ref[0, :, h:h + 1], beta_ref[0, :, h:h + 1])

    hv = [[head_vals(d, p, e) for e in range(2)] for d, p in chains]
    kb = [[hv[i][e][1] * hv[i][e][4] for e in range(2)] for i in range(n)]
    eg = [[jnp.exp(hv[i][e][3]) for e in range(2)] for i in range(n)]
    decay, s2 = [], []
    for i, (d, p) in enumerate(chains):
        gcol = jnp.where(m0, hv[i][0][3], hv[i][1][3])
        grow = ins[6 * d + 4][0, 0, p:p + 1, :]
        decay.append(jnp.where(incl[d], jnp.exp(jnp.where(incl[d], gcol - grow, 0.0)), 0.0))
        k0, k1 = hv[i][0][1], hv[i][1][1]
        zero = jnp.zeros_like(k0)
        lhs = jnp.concatenate([jnp.concatenate([kb[i][0], kb[i][1]], axis=1),
                               jnp.concatenate([hv[i][0][0], hv[i][1][0]], axis=1)], axis=0)
        rhs = jnp.concatenate([jnp.concatenate([k0, zero], axis=1),
                               jnp.concatenate([zero, k1], axis=1)], axis=0)
        s2.append(_mm_nt(lhs, rhs))
    a = [jnp.where(strict[chains[i][0]], s2[i][:CHUNK] * decay[i], 0.0) for i in range(n)]
    t = _tri_inverse(a, eye, blk, m0, -1.0)
    sol = [_mm(bd(t[i]), jnp.concatenate(
        [jnp.concatenate([hv[i][e][2] * hv[i][e][4], kb[i][e] * eg[i][e]], axis=1)
         for e in range(2)], axis=0)) for i in range(n)]

    st = [[st_ref[d, 2 * p + e] for e in range(2)] for d, p in chains]
    ws = [[_mm(jnp.concatenate([sol[i][e * CHUNK:(e + 1) * CHUNK, GD_HEAD:],
                                hv[i][e][0] * eg[i][e]], axis=0), st[i][e])
           for e in range(2)] for i in range(n)]
    v_new = [[sol[i][e * CHUNK:(e + 1) * CHUNK, :GD_HEAD] - ws[i][e][:CHUNK] for e in range(2)]
             for i in range(n)]
    if want_out:
        for i, (d, p) in enumerate(chains):
            intra = _mm(bd(s2[i][CHUNK:] * decay[i]),
                        jnp.concatenate([v_new[i][0], v_new[i][1]], axis=0))
            for e in range(2):
                o_refs[d][:, hcols(2 * p + e)] = (ws[i][e][CHUNK:]
                                                  + intra[e * CHUNK:(e + 1) * CHUNK])
    for i, (d, p) in enumerate(chains):
        for e in range(2):
            gcol = hv[i][e][3]
            g_last = jnp.min(gcol, axis=0, keepdims=True)
            k_dec = hv[i][e][1] * jnp.exp(g_last - gcol)
            st_ref[d, 2 * p + e] = st[i][e] * jnp.exp(g_last) + _mm_tn(k_dec, v_new[i][e])

    @pl.when(c == pl.num_programs(1) - 1)
    def _():
        s_ref[0] = st_ref[...]


def _gdscan(q, k, v, g, gt, beta, s0, batch, want_out):
    m, w = q.shape
    n_chunks = m // batch // CHUNK
    n_heads = w // GD_HEAD
    rows = (lambda b, c: b * n_chunks + c, lambda b, c: b * n_chunks + n_chunks - 1 - c)

    in_specs, args = [], []
    for d in range(2):
        row_spec = pl.BlockSpec((CHUNK, w), lambda b, c, d=d: (rows[d](b, c), 0))
        col_spec = pl.BlockSpec((1, CHUNK, n_heads), lambda b, c, d=d: (d, rows[d](b, c), 0))
        in_specs += [row_spec, row_spec, row_spec, col_spec,
                     pl.BlockSpec((1, 1, n_heads // 2, LANES),
                                  lambda b, c, d=d: (d, rows[d](b, c), 0, 0)),
                     col_spec]
        args += [q, k, v, g, gt, beta]
    state_spec = pl.BlockSpec((1, 2, n_heads, GD_HEAD, GD_HEAD), lambda b, c: (b, 0, 0, 0, 0))
    out_specs = [state_spec]
    out_shape = [jax.ShapeDtypeStruct(s0.shape, F32)]
    if want_out:
        out_specs = [pl.BlockSpec((CHUNK, w), lambda b, c, d=d: (rows[d](b, c), 0))
                     for d in range(2)] + out_specs
        out_shape = [jax.ShapeDtypeStruct((m, w), F32)] * 2 + out_shape
    return pl.pallas_call(
        functools.partial(_gdscan_kernel, want_out=want_out),
        grid=(batch, n_chunks),
        in_specs=in_specs + [state_spec],
        out_specs=out_specs,
        out_shape=out_shape,
        scratch_shapes=[pltpu.VMEM((2, n_heads, GD_HEAD, GD_HEAD), F32)],
        compiler_params=_params(("parallel", "arbitrary")),
        name="gdscan_out" if want_out else "gdscan_state",
    )(*args, s0)


def _gdread_kernel(of_ref, ob_ref, z_ref, ng_ref, y_ref):
    o = of_ref[...] + ob_ref[...]
    z = z_ref[...]
    gate = z * _sigmoid(z)
    for h in range(o.shape[1] // GD_HEAD):
        cols = slice(h * GD_HEAD, (h + 1) * GD_HEAD)
        oh = o[:, cols]
        ms = jnp.mean(oh * oh, axis=-1, keepdims=True)
        y_ref[:, cols] = (oh * lax.rsqrt(ms + NORM_EPS) * ng_ref[...] * gate[:, cols]).astype(BF16)


def _gdread(o_f, o_b, p, z_block, norm_g):
    m, w = o_f.shape
    tm = _tile(m, 512)
    return pl.pallas_call(
        _gdread_kernel,
        grid=(m // tm,),
        in_specs=[pl.BlockSpec((tm, w), lambda i: (i, 0)),
                  pl.BlockSpec((tm, w), lambda i: (i, 0)),
                  pl.BlockSpec((tm, w), lambda i: (i, z_block)),
                  pl.BlockSpec((1, GD_HEAD), lambda i: (0, 0))],
        out_specs=pl.BlockSpec((tm, w), lambda i: (i, 0)),
        out_shape=jax.ShapeDtypeStruct((m, w), BF16),
        compiler_params=_params(("parallel",)),
        name="gdread",
    )(o_f, o_b, p, norm_g)


def _merge1_kernel(ya_ref, yb_ref, wa_ref, wb_ref, ga_ref, gb_ref, o_ref):
    a = jnp.dot(ya_ref[...], wa_ref[...], preferred_element_type=F32)
    b = jnp.dot(yb_ref[...], wb_ref[...], preferred_element_type=F32)
    o_ref[...] = (_sigmoid(ga_ref[...]) * a + _sigmoid(gb_ref[...]) * b).astype(BF16)


def _merge1(ya, yb, wa, wb, p, gate_col0):
    m, ka = ya.shape
    kb = yb.shape[1]
    d = wa.shape[1]
    tm = _tile(m, 512)
    tn = d
    ga0 = gate_col0 // tn
    gb0 = (gate_col0 + d) // tn
    return pl.pallas_call(
        _merge1_kernel,
        grid=(m // tm, d // tn),
        in_specs=[pl.BlockSpec((tm, ka), lambda i, j: (i, 0)),
                  pl.BlockSpec((tm, kb), lambda i, j: (i, 0)),
                  pl.BlockSpec((ka, tn), lambda i, j: (0, j)),
                  pl.BlockSpec((kb, tn), lambda i, j: (0, j)),
                  pl.BlockSpec((tm, tn), lambda i, j: (i, ga0 + j)),
                  pl.BlockSpec((tm, tn), lambda i, j: (i, gb0 + j))],
        out_specs=pl.BlockSpec((tm, tn), lambda i, j: (i, j)),
        out_shape=jax.ShapeDtypeStruct((m, d), BF16),
        compiler_params=_params(("parallel", "arbitrary")),
        name="merge1",
    )(ya, yb, wa, wb, p, p)


def _merge2_kernel(mg_ref, wo_ref, x_ref, mod_ref, o_ref, *, gate_row):
    y = jnp.dot(mg_ref[...], wo_ref[...], preferred_element_type=F32)
    o_ref[...] = x_ref[...] + mod_ref[0, gate_row:gate_row + 1, :] * y


def _merge2(merged, wo, x2, mod, rows_per_mod, gate_row):
    m, d = x2.shape
    tm = _tile(rows_per_mod, 512)
    tn = d
    per = rows_per_mod // tm
    return pl.pallas_call(
        functools.partial(_merge2_kernel, gate_row=gate_row),
        grid=(m // tm, d // tn),
        in_specs=[pl.BlockSpec((tm, d), lambda i, j: (i, 0)),
                  pl.BlockSpec((d, tn), lambda i, j: (0, j)),
                  pl.BlockSpec((tm, tn), lambda i, j: (i, j)),
                  pl.BlockSpec((1, 8, tn), lambda i, j: (i // per, 0, j))],
        out_specs=pl.BlockSpec((tm, tn), lambda i, j: (i, j)),
        out_shape=jax.ShapeDtypeStruct((m, d), F32),
        compiler_params=_params(("parallel", "arbitrary")),
        name="merge2",
    )(merged, wo, x2, mod)


def _conv3x3_gelu(g, cw_ref):
    n = g.shape[0]
    col = lax.broadcasted_iota(jnp.int32, g.shape, 0) % GRID_W
    left = jnp.where(col > 0, pltpu.roll(g, 1, 0), 0.0)
    right = jnp.where(col < GRID_W - 1, pltpu.roll(g, n - 1, 0), 0.0)
    lines = [left * cw_ref[3 * kh:3 * kh + 1, :] + g * cw_ref[3 * kh + 1:3 * kh + 2, :]
             + right * cw_ref[3 * kh + 2:3 * kh + 3, :] for kh in range(3)]
    pad = jnp.zeros((GRID_W, g.shape[1]), F32)
    acc = (lines[1] + jnp.concatenate([pad, lines[0][:n - GRID_W]], axis=0)
           + jnp.concatenate([lines[2][GRID_W:], pad], axis=0))
    return 0.5 * acc * (1.0 + lax.erf(acc * (2.0 ** -0.5)))


def _ffn_act_kernel(x_ref, mod_ref, g_ref, w1g_ref, w1v_ref, cw_ref, o_ref, h_ref, gate_ref, val_ref,
                    *, sh_row, sc_row, prologue_rows):
    j = pl.program_id(1)

    @pl.when(j == 0)
    def _():
        sh = mod_ref[0, sh_row:sh_row + 1, :]
        sc = mod_ref[0, sc_row:sc_row + 1, :]

        def body(r, carry):
            rows = pl.ds(pl.multiple_of(r * prologue_rows, prologue_rows), prologue_rows)
            x = x_ref[rows, :]
            ms = jnp.mean(x * x, axis=-1, keepdims=True)
            y = x * lax.rsqrt(ms + NORM_EPS) * g_ref[...]
            h_ref[rows, :] = (y * (1.0 + sc) + sh).astype(BF16)
            return carry

        lax.fori_loop(0, x_ref.shape[0] // prologue_rows, body, 0)
        gate_ref[1] = jnp.zeros(gate_ref.shape[1:], F32)
        val_ref[1] = jnp.zeros(val_ref.shape[1:], F32)

    slot = j % 2
    o_ref[...] = (_conv3x3_gelu(gate_ref[1 - slot], cw_ref) * val_ref[1 - slot]).astype(BF16)
    hh = h_ref[...]
    gate_ref[slot] = jnp.dot(hh, w1g_ref[...].astype(BF16), preferred_element_type=F32)
    val_ref[slot] = jnp.dot(hh, w1v_ref[...].astype(BF16), preferred_element_type=F32)


def _ffn_act(x1, mod, gain, w1, conv_w, batch, sh_row, sc_row):
    m, d = x1.shape
    dff = conv_w.shape[1]
    seq = m // batch
    tf = _tile(dff, 256, LANES)
    nblk = dff // tf
    cur = lambda j: jnp.minimum(j, nblk - 1)
    prev = lambda j: jnp.maximum(j - 1, 0)
    return pl.pallas_call(
        functools.partial(_ffn_act_kernel, sh_row=sh_row, sc_row=sc_row,
                          prologue_rows=_tile(seq, 256)),
        grid=(batch, nblk + 1),
        in_specs=[pl.BlockSpec((seq, d), lambda b, j: (b, 0), pipeline_mode=pl.Buffered(1)),
                  pl.BlockSpec((1, 8, d), lambda b, j: (b, 0, 0)),
                  pl.BlockSpec((1, d), lambda b, j: (0, 0)),
                  pl.BlockSpec((d, tf), lambda b, j: (0, cur(j))),
                  pl.BlockSpec((d, tf), lambda b, j: (0, nblk + cur(j))),
                  pl.BlockSpec((conv_w.shape[0], tf), lambda b, j: (0, prev(j)))],
        out_specs=pl.BlockSpec((seq, tf), lambda b, j: (b, prev(j))),
        out_shape=jax.ShapeDtypeStruct((m, dff), BF16),
        scratch_shapes=[pltpu.VMEM((seq, d), BF16),
                        pltpu.VMEM((2, seq, tf), F32),
                        pltpu.VMEM((2, seq, tf), F32)],
        compiler_params=_params(("parallel", "arbitrary")),
        name="ffn_act",
    )(x1, mod, gain, w1, w1, conv_w)


def _ffn_out_kernel(act_ref, w2_ref, x_ref, mod_ref, g_ref, o_ref, acc_ref, *, gate_row):
    kstep = pl.program_id(1)

    @pl.when(kstep == 0)
    def _():
        acc_ref[...] = jnp.zeros_like(acc_ref)

    acc_ref[...] += jnp.dot(act_ref[...], w2_ref[...], preferred_element_type=F32)

    @pl.when(kstep == pl.num_programs(1) - 1)
    def _():
        y = x_ref[...] + mod_ref[0, gate_row:gate_row + 1, :] * acc_ref[...]
        ms = jnp.mean(y * y, axis=-1, keepdims=True)
        o_ref[...] = y * lax.rsqrt(ms + NORM_EPS) * g_ref[...]


def _ffn_out(act, w2, x1, mod, final_g, rows_per_mod, gate_row):
    m, d = x1.shape
    dff = act.shape[1]
    tm = _tile(rows_per_mod, 512)
    tk = _tile(dff, 1408, LANES)
    per = rows_per_mod // tm
    return pl.pallas_call(
        functools.partial(_ffn_out_kernel, gate_row=gate_row),
        grid=(m // tm, dff // tk),
        in_specs=[pl.BlockSpec((tm, tk), lambda i, k: (i, k)),
                  pl.BlockSpec((tk, d), lambda i, k: (k, 0)),
                  pl.BlockSpec((tm, d), lambda i, k: (i, 0)),
                  pl.BlockSpec((1, 8, d), lambda i, k: (i // per, 0, 0)),
                  pl.BlockSpec((1, d), lambda i, k: (0, 0))],
        out_specs=pl.BlockSpec((tm, d), lambda i, k: (i, 0)),
        out_shape=jax.ShapeDtypeStruct((m, d), F32),
        scratch_shapes=[pltpu.VMEM((tm, d), F32)],
        compiler_params=_params(("parallel", "arbitrary")),
        name="ffn_out",
    )(act, w2, x1, mod, final_g)


def _pad_cols(a, width):
    return jnp.pad(a, [(0, 0)] * (a.ndim - 1) + [(0, width - a.shape[-1])])


def _pad_rank(a):
    return jnp.pad(a, [(0, 0)] * (a.ndim - 2) + [(0, RANK_PAD - a.shape[-2]), (0, 0)])


def kernel(x, c, ctx, c_ctx, w_ada, b_ada, norm1_g, norm2_g, w_in, rw_mu, rw_k_k, rw_k_a, rw_r_k, rw_w0, rw_w_up, rw_a0, rw_a_up, rw_g_up, rw_gn_g, rw_gn_b, gd_conv_w, gd_a_log, gd_dt_bias, gd_norm_g, w_a_out, w_b_out, w_o, ffn_w1, ffn_conv_w, ffn_w2, final_norm_g):
    batch, seq, d = x.shape
    ctx_len = ctx.shape[1]
    assert w_ada.shape[0] == 1, "single layer only"
    rw_w = rw_k_k.shape[1]
    gd_w = w_b_out.shape[1]
    dec_rank = rw_w_up.shape[2]
    icl_rank = rw_a_up.shape[2]
    gate_rank = rw_g_up.shape[1]
    gd_heads = gd_a_log.shape[2]
    assert max(dec_rank, icl_rank) <= RANK_PAD and 4 * gd_heads <= 2 * LANES
    assert seq % CHUNK == 0 and ctx_len % CHUNK == 0 and seq % GRID_W == 0
    assert gate_rank <= 2 * LANES

    low_w = 8 * LANES
    blk0 = 3 * rw_w + low_w
    assert blk0 == 4 * gd_w
    wi = w_in[0]
    o_rw = 3 * rw_w
    o_gd = o_rw + 2 * dec_rank + 2 * icl_rank + gate_rank
    o_ab = o_gd + 4 * gd_w
    o_gate = o_ab + 4 * gd_heads

    def pack_cols(a):
        pieces = [a[..., :o_rw]]
        off = o_rw
        for r in (dec_rank, dec_rank, icl_rank, icl_rank):
            pieces.append(_pad_cols(a[..., off:off + r], RANK_PAD))
            off += r
        pieces.append(_pad_cols(a[..., off:off + gate_rank], 2 * LANES))
        return pieces

    w_pack = jnp.concatenate(
        pack_cols(wi) + [_pad_cols(wi[:, o_ab:o_gate], 2 * LANES), wi[:, o_gd:o_ab], wi[:, o_gate:]],
        axis=1).astype(BF16)
    n_ctx_cols = 2 * blk0
    gate_col0 = 2 * blk0
    mu_pack = jnp.concatenate(pack_cols(rw_mu) + [jnp.zeros((1, 2 * LANES), F32)], axis=1)

    cc = jnp.concatenate([c, c_ctx[None, :], jnp.zeros((16 - batch - 1, d), F32)], axis=0)
    mods = _mod(cc, w_ada[0], b_ada)
    mod_lat = _pad_rows8(mods[:batch].reshape(batch, 6, d))
    mod_ctx = _pad_rows8(mods[batch:batch + 1].reshape(1, 6, d))

    x2 = x.reshape(batch * seq, d)
    ctx2 = ctx.reshape(batch * ctx_len, d)
    p_lat = _normproj(x2, mod_lat, norm1_g, w_pack, seq, w_pack.shape[1], 0, 1, "inproj_lat")
    p_ctx = _normproj(ctx2, mod_ctx, norm1_g, w_pack, batch * ctx_len, n_ctx_cols, 0, 1, "inproj_ctx")

    rw_wts = (mu_pack, rw_k_k, rw_k_a, rw_r_k, rw_w0[0], _pad_rank(rw_w_up[0]).astype(BF16),
              rw_a0[0], _pad_rank(rw_a_up[0]).astype(BF16),
              jnp.pad(rw_g_up[0], ((0, 2 * LANES - gate_rank), (0, 0))).astype(BF16))
    n_pairs = rw_w // LANES
    s0 = jnp.zeros((batch, 2, n_pairs, LANES, LANES), F32)
    f_ctx = _rwprep(p_ctx, ctx_len, rw_wts)
    f_lat = _rwprep(p_lat, seq, rw_wts)
    (s_ctx,) = _rwscan(*f_ctx[:6], s0, batch, want_out=False)
    o_rw_f, o_rw_b, _ = _rwscan(*f_lat[:6], s_ctx, batch, want_out=True)
    ya = _rwread(o_rw_f, o_rw_b, f_lat[6], f_lat[7], rw_gn_g, rw_gn_b)

    s0g = jnp.zeros((batch, 2, gd_heads, GD_HEAD, GD_HEAD), F32)
    a_log2 = gd_a_log[0].reshape(1, 2 * gd_heads)
    dtb2 = gd_dt_bias[0].reshape(1, 2 * gd_heads)

    def gd_feats(p, seq_len):
        ab_block = (3 * rw_w + 4 * RANK_PAD + 2 * LANES) // LANES
        q, k, v, gcum, beta = _gdprep(p, seq_len, gd_conv_w[0], a_log2, dtb2, 1, ab_block, gd_w)
        m = p.shape[0]
        g3 = gcum.reshape(m, 2, gd_heads).transpose(1, 0, 2)
        gt = g3.reshape(2, m // CHUNK, CHUNK, gd_heads).transpose(0, 1, 3, 2)
        gt = gt.reshape(2, m // CHUNK, gd_heads // 2, 2 * CHUNK)
        b3 = beta.reshape(m, 2, gd_heads).transpose(1, 0, 2)
        return q, k, v, g3, gt, b3

    (sg_ctx,) = _gdscan(*gd_feats(p_ctx, ctx_len), s0g, batch, want_out=False)
    o_gd_f, o_gd_b, _ = _gdscan(*gd_feats(p_lat, seq), sg_ctx, batch, want_out=True)
    yb = _gdread(o_gd_f, o_gd_b, p_lat, 2 * blk0 // gd_w - 1, gd_norm_g)

    merged = _merge1(ya, yb, w_a_out[0].astype(BF16), w_b_out[0].astype(BF16), p_lat, gate_col0)
    x1 = _merge2(merged, w_o[0].astype(BF16), x2, mod_lat, seq, 2)

    act = _ffn_act(x1, mod_lat, norm2_g, ffn_w1[0],
                   ffn_conv_w[0].reshape(-1, ffn_conv_w.shape[-1]), batch, 3, 4)
    out = _ffn_out(act, ffn_w2[0].astype(BF16), x1, mod_lat, final_norm_g[None, :], seq, 5)
    return out.reshape(batch, seq, d)


def _pad_rows8(a):
    return jnp.pad(a, ((0, 0), (0, 8 - a.shape[1]), (0, 0)))
```
